```python
import math
import jax, jax.numpy as jnp
from jax import lax
import numpy as np


D_MODEL = 1024
BATCH = 16
SEQ = 256
DEPTH = 2
DEC_BATCH = 4
DEC_SEQ = 2048
PAST_LEN = 512

GRID_W = 64
N_EVEN = (DEPTH + 1) // 2
N_ODD = DEPTH // 2
DA_HEADS = 4
DA_QK = 64
DA_V = 2 * DA_QK
DA_WIDTH = DA_HEADS * DA_V
S5_WIDTH = D_MODEL - DA_WIDTH
S5_CH = 16
S5_GROUPS = S5_WIDTH // S5_CH
S5_STATE = 64
IN_WIDTH = 3 * DA_WIDTH + S5_WIDTH
POOL_WINDOWS = (2, 4, 8, 16)
POOL_GROUPS = len(POOL_WINDOWS)
POOL_CH = D_MODEL // POOL_GROUPS
FF_HIDDEN = 4 * D_MODEL
N_MOD = 6
ROPE_BASE = 10000.0
Q_BLOCK = 128
EPS = 1e-6

kernel_name = 'hybrid_diffattn_s5_pool_diffusion_step'


def rmsnorm(x, g):
    xf = x.astype(jnp.float32)
    y = xf * lax.rsqrt(jnp.mean(xf * xf, axis=-1, keepdims=True) + EPS)
    return y.astype(x.dtype) * g


def modulate(h, shift, scale):
    return h * (1 + scale) + shift


def axial_angles(L):
    rows = L // GRID_W
    row = jnp.repeat(jnp.arange(rows), GRID_W).astype(jnp.float32)
    col = jnp.tile(jnp.arange(GRID_W), rows).astype(jnp.float32)
    half = DA_QK // 2
    inv = 1.0 / (ROPE_BASE ** (jnp.arange(0, half, 2, dtype=jnp.float32) / half))
    return row[:, None] * inv, col[:, None] * inv


def rotate(seg, ang):
    cos = jnp.cos(ang)[None, :, None, None, :].astype(seg.dtype)
    sin = jnp.sin(ang)[None, :, None, None, :].astype(seg.dtype)
    x1, x2 = jnp.split(seg, 2, axis=-1)
    return jnp.concatenate([x1 * cos - x2 * sin, x2 * cos + x1 * sin], axis=-1)


def axial_rope(x):
    ang_r, ang_c = axial_angles(x.shape[1])
    half = DA_QK // 2
    return jnp.concatenate([rotate(x[..., :half], ang_r), rotate(x[..., half:], ang_c)], axis=-1)


def split_in(h, w_in):
    B, L, _ = h.shape
    z = h @ w_in
    q = z[..., :DA_WIDTH].reshape(B, L, DA_HEADS, 2, DA_QK)
    k = z[..., DA_WIDTH:2 * DA_WIDTH].reshape(B, L, DA_HEADS, 2, DA_QK)
    v = z[..., 2 * DA_WIDTH:3 * DA_WIDTH].reshape(B, L, DA_HEADS, DA_V)
    u = z[..., 3 * DA_WIDTH:]
    return q, k, v, u


def diff_lambda(lambda_qk, lam_init):
    lq1, lk1, lq2, lk2 = lambda_qk.astype(jnp.float32)
    return jnp.exp(jnp.sum(lq1 * lk1)) - jnp.exp(jnp.sum(lq2 * lk2)) + lam_init


def diff_attention(q, k, v, lam):
    B, Lq = q.shape[0], q.shape[1]
    nblk = Lq // Q_BLOCK
    scale = DA_QK ** -0.5
    qb = q.reshape(B, nblk, Q_BLOCK, DA_HEADS, 2, DA_QK).transpose(1, 0, 2, 3, 4, 5)

    def one_block(qblk):
        s = jnp.einsum('bqhmd,bkhmd->bhmqk', qblk, k).astype(jnp.float32) * scale
        p = jax.nn.softmax(s, axis=-1)
        w = p[:, :, 0] - lam * p[:, :, 1]
        return jnp.einsum('bhqk,bkhe->bqhe', w.astype(v.dtype), v)

    out = lax.map(one_block, qb)
    return out.transpose(1, 0, 2, 3, 4).reshape(B, Lq, DA_HEADS, DA_V)


def diff_head_out(o, subln_g, lam_init):
    B, L = o.shape[0], o.shape[1]
    o = rmsnorm(o, subln_g) * (1.0 - lam_init)
    return o.reshape(B, L, DA_WIDTH)


def s5_discretize(lam_re, lam_im, log_dt, b_re, b_im):
    lam_re = lam_re.astype(jnp.float32)
    lam_im = lam_im.astype(jnp.float32)
    dt = jnp.exp(log_dt.astype(jnp.float32))[:, None]
    mag = jnp.exp(lam_re * dt)
    ang = lam_im * dt
    a_re = mag * jnp.cos(ang)
    a_im = mag * jnp.sin(ang)
    den = lam_re * lam_re + lam_im * lam_im
    f_re = ((a_re - 1.0) * lam_re + a_im * lam_im) / den
    f_im = (a_im * lam_re - (a_re - 1.0) * lam_im) / den
    b_re = b_re.astype(jnp.float32)
    b_im = b_im.astype(jnp.float32)
    bb_re = f_re[..., None] * b_re - f_im[..., None] * b_im
    bb_im = f_re[..., None] * b_im + f_im[..., None] * b_re
    return a_re, a_im, bb_re, bb_im


def complex_affine_combine(e1, e2):
    a1r, a1i, b1r, b1i = e1
    a2r, a2i, b2r, b2i = e2
    return (a2r * a1r - a2i * a1i,
            a2r * a1i + a2i * a1r,
            a2r * b1r - a2i * b1i + b2r,
            a2r * b1i + a2i * b1r + b2i)


def s5_direction(uf, lam_re, lam_im, log_dt, b_re, b_im, c_re, c_im, h0, reverse):
    a_re, a_im, bb_re, bb_im = s5_discretize(lam_re, lam_im, log_dt, b_re, b_im)
    bu_re = jnp.einsum('blgh,gph->blgp', uf, bb_re)
    bu_im = jnp.einsum('blgh,gph->blgp', uf, bb_im)
    if h0 is not None:
        h0_re, h0_im = h0
        pos = -1 if reverse else 0
        bu_re = bu_re.at[:, pos].add(a_re * h0_re - a_im * h0_im)
        bu_im = bu_im.at[:, pos].add(a_re * h0_im + a_im * h0_re)
    elems = (jnp.broadcast_to(a_re, bu_re.shape), jnp.broadcast_to(a_im, bu_re.shape), bu_re, bu_im)
    _, _, h_re, h_im = lax.associative_scan(complex_affine_combine, elems, reverse=reverse, axis=1)
    y = (jnp.einsum('blgp,ghp->blgh', h_re, c_re.astype(jnp.float32))
         - jnp.einsum('blgp,ghp->blgh', h_im, c_im.astype(jnp.float32)))
    last = 0 if reverse else -1
    return y, h_re[:, last], h_im[:, last]


def s5_mixer(u, lam_re, lam_im, log_dt, b_re, b_im, c_re, c_im, d_skip, w_glu, h0):
    B, L, _ = u.shape
    uf = u.astype(jnp.float32).reshape(B, L, S5_GROUPS, S5_CH)
    ys, finals = [], []
    for dr in (0, 1):
        init = None if h0 is None else (h0[:, dr, :, :, 0].astype(jnp.float32), h0[:, dr, :, :, 1].astype(jnp.float32))
        y, fr, fi = s5_direction(uf, lam_re[dr], lam_im[dr], log_dt[dr], b_re[dr], b_im[dr],
                                 c_re[dr], c_im[dr], init, dr == 1)
        ys.append(y)
        finals.append(jnp.stack([fr, fi], axis=-1))
    y = (ys[0] + ys[1]).reshape(B, L, S5_WIDTH) + d_skip.astype(jnp.float32) * u.astype(jnp.float32)
    g = jax.nn.gelu(y).astype(u.dtype)
    out = g * jax.nn.sigmoid(g @ w_glu)
    state = jnp.stack(finals, axis=1).astype(u.dtype) if h0 is None else None
    return out, state


def pool_mixer(h, pool_w, pool_scale):
    B, L, D = h.shape
    hf = h.astype(jnp.float32)
    csum = jnp.concatenate([jnp.zeros((B, 1, D), jnp.float32), jnp.cumsum(hf, axis=1)], axis=1)
    t = jnp.arange(L)
    outs = []
    for g, w in enumerate(POOL_WINDOWS):
        lo = jnp.clip(t - w // 2, 0, L)
        hi = jnp.clip(t + w // 2, 0, L)
        seg = csum[..., g * POOL_CH:(g + 1) * POOL_CH]
        mean = (seg[:, hi] - seg[:, lo]) / (hi - lo).astype(jnp.float32)[None, :, None]
        outs.append(mean - hf[..., g * POOL_CH:(g + 1) * POOL_CH])
    z = jnp.stack(outs, axis=2).astype(h.dtype)
    z = jnp.einsum('blgc,gcd->blgd', z, pool_w).reshape(B, L, D)
    return z * pool_scale


def sq_relu_mlp(h, w1, w2):
    return jnp.square(jax.nn.relu(h @ w1)) @ w2


def setup_inputs(seed: int = 0) -> dict:
    key = jax.random.key(seed)
    ks = jax.random.split(key, 32)
    nrm = jax.random.normal
    f32 = jnp.float32
    lam_im_base = jnp.pi * jnp.arange(S5_STATE, dtype=f32)
    return {
        'x_prompt': nrm(ks[0], (BATCH, SEQ, D_MODEL), f32),
        'x_sample': nrm(ks[1], (DEC_BATCH, DEC_SEQ, D_MODEL), f32),
        'cache_k': nrm(ks[2], (DEC_BATCH, N_EVEN, PAST_LEN, DA_HEADS, 2 * DA_QK), f32),
        'cache_v': nrm(ks[3], (DEC_BATCH, N_EVEN, PAST_LEN, DA_HEADS, DA_V), f32),
        'state_s5': 0.5 * nrm(ks[4], (DEC_BATCH, N_EVEN, 2, S5_GROUPS, S5_STATE, 2), f32),
        'c': nrm(ks[5], (DEC_BATCH, D_MODEL), f32),
        'c_ctx': nrm(ks[6], (D_MODEL,), f32),
        'mod_w': 0.5 * D_MODEL ** -0.5 * nrm(ks[7], (DEPTH, D_MODEL, N_MOD * D_MODEL), f32),
        'mod_b': 0.01 * nrm(ks[8], (DEPTH, N_MOD * D_MODEL), f32),
        'norm_g': 1.0 + 0.02 * nrm(ks[9], (DEPTH, 2, D_MODEL), f32),
        'mix_w_in': D_MODEL ** -0.5 * nrm(ks[10], (N_EVEN, D_MODEL, IN_WIDTH), f32),
        'mix_w_out': D_MODEL ** -0.5 * nrm(ks[11], (N_EVEN, D_MODEL, D_MODEL), f32),
        'diff_lambda_qk': 0.1 * nrm(ks[12], (N_EVEN, 4, DA_QK), f32),
        'diff_subln_g': 1.0 + 0.02 * nrm(ks[13], (N_EVEN, DA_V), f32),
        's5_lambda_re': -0.5 + 0.01 * nrm(ks[14], (N_EVEN, 2, S5_GROUPS, S5_STATE), f32),
        's5_lambda_im': lam_im_base + 0.01 * nrm(ks[15], (N_EVEN, 2, S5_GROUPS, S5_STATE), f32),
        's5_log_dt': jax.random.uniform(ks[16], (N_EVEN, 2, S5_GROUPS), f32, math.log(1e-3), math.log(1e-1)),
        's5_b_re': (2 * S5_CH) ** -0.5 * nrm(ks[17], (N_EVEN, 2, S5_GROUPS, S5_STATE, S5_CH), f32),
        's5_b_im': (2 * S5_CH) ** -0.5 * nrm(ks[18], (N_EVEN, 2, S5_GROUPS, S5_STATE, S5_CH), f32),
        's5_c_re': (2 * S5_STATE) ** -0.5 * nrm(ks[19], (N_EVEN, 2, S5_GROUPS, S5_CH, S5_STATE), f32),
        's5_c_im': (2 * S5_STATE) ** -0.5 * nrm(ks[20], (N_EVEN, 2, S5_GROUPS, S5_CH, S5_STATE), f32),
        's5_d': nrm(ks[21], (N_EVEN, S5_WIDTH), f32),
        's5_w_glu': S5_WIDTH ** -0.5 * nrm(ks[22], (N_EVEN, S5_WIDTH, S5_WIDTH), f32),
        'pool_w': POOL_CH ** -0.5 * nrm(ks[23], (N_ODD, POOL_GROUPS, POOL_CH, POOL_CH), f32),
        'pool_scale': 1.0 + 0.02 * nrm(ks[24], (N_ODD, D_MODEL), f32),
        'ff_w1': D_MODEL ** -0.5 * nrm(ks[25], (DEPTH, D_MODEL, FF_HIDDEN), f32),
        'ff_w2': FF_HIDDEN ** -0.5 * nrm(ks[26], (DEPTH, FF_HIDDEN, D_MODEL), f32),
        'final_norm_g': 1.0 + 0.02 * nrm(ks[27], (D_MODEL,), f32),
    }


def reference(x_prompt, x_sample, cache_k, cache_v, state_s5, c, c_ctx, mod_w, mod_b, norm_g,
              mix_w_in, mix_w_out, diff_lambda_qk, diff_subln_g, s5_lambda_re, s5_lambda_im, s5_log_dt,
              s5_b_re, s5_b_im, s5_c_re, s5_c_im, s5_d, s5_w_glu, pool_w, pool_scale, ff_w1, ff_w2,
              final_norm_g):
    ctx = x_prompt
    lat = x_sample
    new_k, new_v, new_s = [], [], []
    for i in range(DEPTH):
        m_ctx = (jax.nn.silu(c_ctx) @ mod_w[i] + mod_b[i]).reshape(N_MOD, D_MODEL)
        m_lat = (jax.nn.silu(c) @ mod_w[i] + mod_b[i]).reshape(-1, N_MOD, 1, D_MODEL)
        hc = modulate(rmsnorm(ctx, norm_g[i, 0]), m_ctx[0], m_ctx[1])
        hl = modulate(rmsnorm(lat, norm_g[i, 0]), m_lat[:, 0], m_lat[:, 1])
        if i % 2 == 0:
            j = i // 2
            lam_init = 0.8 - 0.6 * math.exp(-0.3 * i)
            lam = diff_lambda(diff_lambda_qk[j], lam_init)
            s5p = (s5_lambda_re[j], s5_lambda_im[j], s5_log_dt[j], s5_b_re[j], s5_b_im[j],
                   s5_c_re[j], s5_c_im[j], s5_d[j], s5_w_glu[j])
            qc, kc, vc, uc = split_in(hc, mix_w_in[j])
            ac = diff_attention(qc, kc, vc, lam)
            sc, st_c = s5_mixer(uc, *s5p, None)
            oc = jnp.concatenate([diff_head_out(ac, diff_subln_g[j], lam_init), sc], axis=-1) @ mix_w_out[j]
            new_k.append(kc.reshape(kc.shape[0], kc.shape[1], DA_HEADS, 2 * DA_QK))
            new_v.append(vc)
            new_s.append(st_c)
            ql, kl, vl, ul = split_in(hl, mix_w_in[j])
            ql, kl = axial_rope(ql), axial_rope(kl)
            ck = cache_k[:, j].reshape(cache_k.shape[0], cache_k.shape[2], DA_HEADS, 2, DA_QK)
            k_all = jnp.concatenate([kl, ck], axis=1)
            v_all = jnp.concatenate([vl, cache_v[:, j]], axis=1)
            al = diff_attention(ql, k_all, v_all, lam)
            sl, _ = s5_mixer(ul, *s5p, state_s5[:, j])
            ol = jnp.concatenate([diff_head_out(al, diff_subln_g[j], lam_init), sl], axis=-1) @ mix_w_out[j]
        else:
            j = i // 2
            oc = pool_mixer(hc, pool_w[j], pool_scale[j])
            ol = pool_mixer(hl, pool_w[j], pool_scale[j])
        ctx = ctx + m_ctx[2] * oc
        lat = lat + m_lat[:, 2] * ol
        hc = modulate(rmsnorm(ctx, norm_g[i, 1]), m_ctx[3], m_ctx[4])
        hl = modulate(rmsnorm(lat, norm_g[i, 1]), m_lat[:, 3], m_lat[:, 4])
        ctx = ctx + m_ctx[5] * sq_relu_mlp(hc, ff_w1[i], ff_w2[i])
        lat = lat + m_lat[:, 5] * sq_relu_mlp(hl, ff_w1[i], ff_w2[i])
    y_prompt = rmsnorm(ctx, final_norm_g)
    y_sample = rmsnorm(lat, final_norm_g)
    new_cache_k = jnp.stack(new_k, axis=1)
    new_cache_v = jnp.stack(new_v, axis=1)
    new_state_s5 = jnp.stack(new_s, axis=1)
    return (y_prompt, y_sample, new_cache_k, new_cache_v, new_state_s5)
```

```python
import functools
import math

import jax
import jax.numpy as jnp
from jax import lax
from jax.experimental import pallas as pl
from jax.experimental.pallas import tpu as pltpu

D_MODEL = 1024
DEPTH = 2
GRID_W = 64
DA_HEADS = 4
DA_QK = 64
DA_V = 2 * DA_QK
DA_WIDTH = DA_HEADS * DA_V
S5_WIDTH = D_MODEL - DA_WIDTH
S5_CH = 16
S5_GROUPS = S5_WIDTH // S5_CH
S5_STATE = 64
IN_WIDTH = 3 * DA_WIDTH + S5_WIDTH
POOL_WINDOWS = (2, 4, 8, 16)
POOL_CH = D_MODEL // len(POOL_WINDOWS)
FF_HIDDEN = 4 * D_MODEL
N_MOD = 6
ROPE_BASE = 10000.0
EPS = 1e-6

LANES = 128
S5_T = 8
S5_OCT = S5_WIDTH // LANES
S5_GPO = LANES // S5_CH
S5_SW = 2 * S5_GPO * S5_STATE
POOL_HALO = 16
FF_CHUNK = 1024
VMEM_LIMIT = 56 * 1024 * 1024

F32 = jnp.float32
BF16 = jnp.bfloat16


def _const_spec(shape):
    nd = len(shape)
    return pl.BlockSpec(shape, lambda *_: (0,) * nd, pipeline_mode=pl.Buffered(1))


def _params(n_axes):
    return pltpu.CompilerParams(dimension_semantics=("arbitrary",) * n_axes,
                                vmem_limit_bytes=VMEM_LIMIT)


def _rms(x, g):
    return x * lax.rsqrt(jnp.mean(x * x, axis=-1, keepdims=True) + EPS) * g


def _dot(a, b):
    return jnp.dot(a, b, preferred_element_type=F32)


def _mod_kernel(c_ref, w_ref, b_ref, o_ref):
    c = c_ref[...]
    s = (c * jax.nn.sigmoid(c)).astype(BF16)
    o_ref[0] = _dot(s, w_ref[0].astype(BF16)) + b_ref[0]


def _modulation(cc, mod_w, mod_b):
    rows = cc.shape[0]
    tn = 1536
    n = N_MOD * D_MODEL
    return pl.pallas_call(
        _mod_kernel,
        grid=(DEPTH, n // tn),
        in_specs=[pl.BlockSpec((rows, D_MODEL), lambda l, j: (0, 0)),
                  pl.BlockSpec((1, D_MODEL, tn), lambda l, j: (l, 0, j)),
                  pl.BlockSpec((1, 1, tn), lambda l, j: (l, 0, j))],
        out_specs=pl.BlockSpec((1, rows, tn), lambda l, j: (l, 0, j)),
        out_shape=jax.ShapeDtypeStruct((DEPTH, rows, n), F32),
        compiler_params=_params(2),
        name="modulation",
    )(cc, mod_w, mod_b.reshape(DEPTH, 1, n))


def _premix_kernel(*refs, rope, cache_out):
    x_ref, mod_ref, g_ref, w_ref = refs[:4]
    refs = refs[4:]
    if rope:
        cos_ref, sa_ref, sb_ref = refs[:3]
        refs = refs[3:]
    q_ref, k_ref, v_ref, u_ref = refs[:4]
    x = x_ref[0]
    h = _rms(x, g_ref[...]) * (1.0 + mod_ref[0, 1:2, :]) + mod_ref[0, 0:1, :]
    z = _dot(h.astype(BF16), w_ref[...])
    n_slab = 2 * DA_HEADS

    def rot(s):
        if not rope:
            return s
        return (s * cos_ref[...] + pltpu.roll(s, LANES - 16, axis=1) * sa_ref[...]
                + pltpu.roll(s, 16, axis=1) * sb_ref[...])

    for j in range(n_slab):
        s = rot(z[:, j * LANES:(j + 1) * LANES])
        if j < DA_HEADS:
            q_ref[0, :, j * LANES:(j + 1) * LANES] = (s * DA_QK ** -0.5).astype(BF16)
        else:
            jj = j - DA_HEADS
            k_ref[0, :, jj * LANES:(jj + 1) * LANES] = s.astype(BF16)
    v = z[:, 2 * DA_WIDTH:3 * DA_WIDTH]
    v_ref[0] = v.astype(BF16)
    u_ref[0] = z[:, 3 * DA_WIDTH:]
    if cache_out:
        kf_ref, vf_ref = refs[4:6]
        kf_ref[0] = z[:, DA_WIDTH:2 * DA_WIDTH]
        vf_ref[0] = v


def _premix(x, mods, mod_row, norm_g, w_in, rope_tabs, tm, cache_out):
    B, L, _ = x.shape
    rope = rope_tabs is not None
    row_spec = lambda w: pl.BlockSpec((1, tm, w), lambda b, i: (b, i, 0))
    in_specs = [row_spec(D_MODEL),
                pl.BlockSpec((1, N_MOD, D_MODEL), lambda b, i: (mod_row(b), 0, 0)),
                _const_spec((1, D_MODEL)),
                _const_spec((D_MODEL, IN_WIDTH))]
    args = [x, mods, norm_g.reshape(1, D_MODEL), w_in]
    if rope:
        in_specs += [pl.BlockSpec((tm, LANES), lambda b, i: (i, 0))] * 3
        args += list(rope_tabs)
    out_specs = [row_spec(DA_WIDTH)] * 3 + [row_spec(S5_WIDTH)]
    out_shape = [jax.ShapeDtypeStruct((B, L, DA_WIDTH), BF16)] * 3 + [
        jax.ShapeDtypeStruct((B, L, S5_WIDTH), F32)]
    if cache_out:
        out_specs += [row_spec(DA_WIDTH)] * 2
        out_shape += [jax.ShapeDtypeStruct((B, L, DA_WIDTH), F32)] * 2
    return pl.pallas_call(
        functools.partial(_premix_kernel, rope=rope, cache_out=cache_out),
        grid=(B, L // tm),
        in_specs=in_specs, out_specs=out_specs, out_shape=out_shape,
        compiler_params=_params(2),
        name="premix_rope" if rope else "premix",
    )(*args)


def _rope_tables(L):
    rows = L // GRID_W
    row = jnp.repeat(jnp.arange(rows), GRID_W).astype(F32)
    col = jnp.tile(jnp.arange(GRID_W), rows).astype(F32)
    half = DA_QK // 2
    inv = 1.0 / (ROPE_BASE ** (jnp.arange(0, half, 2, dtype=F32) / half))
    ang_r, ang_c = row[:, None] * inv, col[:, None] * inv
    zero = jnp.zeros_like(ang_r)
    cos64 = jnp.concatenate([jnp.cos(ang_r)] * 2 + [jnp.cos(ang_c)] * 2, axis=1)
    sa64 = jnp.concatenate([-jnp.sin(ang_r), zero, -jnp.sin(ang_c), zero], axis=1)
    sb64 = jnp.concatenate([zero, jnp.sin(ang_r), zero, jnp.sin(ang_c)], axis=1)
    return tuple(jnp.tile(t, (1, LANES // DA_QK)) for t in (cos64, sa64, sb64))


def _attn_kernel(*refs, lam_init, has_cache):
    lqk_ref, g_ref, q_ref, k_ref, v_ref = refs[:5]
    if has_cache:
        ck_ref, cv_ref, o_ref = refs[5:8]
    else:
        o_ref = refs[5]
    lqk = lqk_ref[...]
    lam = (jnp.exp(jnp.sum(lqk[0:1] * lqk[1:2], axis=-1, keepdims=True))
           - jnp.exp(jnp.sum(lqk[2:3] * lqk[3:4], axis=-1, keepdims=True)) + lam_init)
    q = q_ref[0]
    lane = lax.broadcasted_iota(jnp.int32, q.shape, 1)
    zero = jnp.zeros_like(q)
    segs = [(k_ref[0], v_ref[0])]
    if has_cache:
        segs.append((ck_ref[0].astype(BF16), cv_ref[0].astype(BF16)))
    nt = (((1,), (1,)), ((), ()))
    outs = []
    for qm in (jnp.where(lane < DA_QK, q, zero), jnp.where(lane >= DA_QK, q, zero)):
        ss = [lax.dot_general(qm, k, nt, preferred_element_type=F32) for k, _ in segs]
        m = functools.reduce(jnp.maximum, [jnp.max(s, axis=-1, keepdims=True) for s in ss])
        es = [jnp.exp(s - m) for s in ss]
        l = functools.reduce(jnp.add, [jnp.sum(e, axis=-1, keepdims=True) for e in es])
        o = functools.reduce(jnp.add, [_dot(e.astype(BF16), v) for e, (_, v) in zip(es, segs)])
        outs.append(o / l)
    o = outs[0] - lam * outs[1]
    o_ref[0] = (_rms(o, g_ref[...]) * (1.0 - lam_init)).astype(BF16)


def _attention(q, k, v, cache, lambda_qk, subln_g, lam_init, tq):
    B, L, _ = q.shape
    has_cache = cache is not None
    q_spec = pl.BlockSpec((1, tq, LANES), lambda b, h, i: (b, i, h))
    kv_spec = lambda n: pl.BlockSpec((1, n, LANES), lambda b, h, i: (b, 0, h))
    in_specs = [_const_spec((4, DA_QK)), _const_spec((1, DA_V)), q_spec, kv_spec(L), kv_spec(L)]
    args = [lambda_qk, subln_g.reshape(1, DA_V), q, k, v]
    if has_cache:
        P = cache[0].shape[1]
        in_specs += [kv_spec(P), kv_spec(P)]
        args += list(cache)
    return pl.pallas_call(
        functools.partial(_attn_kernel, lam_init=lam_init, has_cache=has_cache),
        grid=(B, DA_HEADS, L // tq),
        in_specs=in_specs, out_specs=q_spec,
        out_shape=jax.ShapeDtypeStruct((B, L, DA_WIDTH), BF16),
        compiler_params=_params(3),
        name="diff_attn_cache" if has_cache else "diff_attn",
    )(*args)


def _s5_weights(lam_re, lam_im, log_dt, b_re, b_im, c_re, c_im):
    T, G, P = S5_T, S5_GROUPS, S5_STATE
    hp = lax.Precision.HIGHEST
    dt = jnp.exp(log_dt)[..., None]
    lr, li = lam_re * dt, lam_im * dt
    kk = jnp.arange(T + 1, dtype=F32)[:, None, None, None]
    pw_re = jnp.exp(kk * lr) * jnp.cos(kk * li)
    pw_im = jnp.exp(kk * lr) * jnp.sin(kk * li)
    a_re, a_im = pw_re[1], pw_im[1]
    den = lam_re * lam_re + lam_im * lam_im
    f_re = ((a_re - 1.0) * lam_re + a_im * lam_im) / den
    f_im = (a_im * lam_re - (a_re - 1.0) * lam_im) / den
    bb_re = f_re[..., None] * b_re - f_im[..., None] * b_im
    bb_im = f_re[..., None] * b_im + f_im[..., None] * b_re

    eye = jnp.eye(S5_GPO, dtype=F32)

    def in_to_state(d, pows):
        pr, pi = pw_re[pows, d], pw_im[pows, d]
        re = pr[..., None] * bb_re[d][None] - pi[..., None] * bb_im[d][None]
        im = pr[..., None] * bb_im[d][None] + pi[..., None] * bb_re[d][None]
        f = jnp.stack([re, im], axis=0)
        f = f.reshape(2, T, S5_OCT, S5_GPO, P, S5_CH)
        full = jnp.einsum('rsogph,gk->osghrkp', f, eye)
        return full.reshape(S5_OCT, T * LANES, S5_SW)

    def state_to_out(d, pows):
        pr, pi = pw_re[pows, d], pw_im[pows, d]
        cr, ci = c_re[d], c_im[d]
        re = cr[None] * pr[:, :, None, :] - ci[None] * pi[:, :, None, :]
        im = cr[None] * pi[:, :, None, :] + ci[None] * pr[:, :, None, :]
        e = jnp.stack([re, -im], axis=0)
        e = e.reshape(2, T, S5_OCT, S5_GPO, S5_CH, P)
        full = jnp.einsum('rtoghp,gk->orgptkh', e, eye)
        return full.reshape(S5_OCT, S5_SW, T * LANES), (re, im)

    lags = jnp.arange(T)
    ff = in_to_state(0, T - 1 - lags)
    fb = in_to_state(1, lags)
    ef, _ = state_to_out(0, lags + 1)
    eb, _ = state_to_out(1, T - lags)

    def lag_kernels(d):
        pr, pi = pw_re[:T, d], pw_im[:T, d]
        cr, ci = c_re[d], c_im[d]
        car = cr[None] * pr[:, :, None, :] - ci[None] * pi[:, :, None, :]
        cai = cr[None] * pi[:, :, None, :] + ci[None] * pr[:, :, None, :]
        return (jnp.einsum('tgip,gpj->tgji', car, bb_re[d], precision=hp)
                - jnp.einsum('tgip,gpj->tgji', cai, bb_im[d], precision=hp))

    kf, kb = lag_kernels(0), lag_kernels(1)
    s_idx, t_idx = lags[:, None], lags[None, :]
    lag = jnp.abs(t_idx - s_idx)
    sel = (t_idx - s_idx)[:, :, None, None, None]
    kst = jnp.where(sel > 0, kf[lag], jnp.where(sel < 0, kb[lag], (kf[0] + kb[0])[None, None]))
    kst = kst.reshape(T, T, S5_OCT, S5_GPO, S5_CH, S5_CH)
    toep = jnp.einsum('stoghi,gk->osghtki', kst, eye).reshape(S5_OCT, T * LANES, T * LANES)

    at = jnp.stack([pw_re[T], pw_im[T]], axis=1)
    at = at.reshape(2, 2, S5_OCT, S5_GPO * P).transpose(0, 2, 1, 3)
    return tuple(w.astype(BF16) for w in (toep, ff, fb, ef, eb)) + (at,)


def _s5_kernel(*refs, n_seq, n_chunk):
    T = S5_T
    x_refs = refs[:T]
    toep_ref, ff_ref, fb_ref, ef_ref, eb_ref, at_ref, h0_ref, d_ref = refs[T:T + 8]
    out_refs = refs[T + 8:2 * T + 8]
    fin_ref, sf_ref, sb_ref = refs[2 * T + 8:]

    xs = [r[...] for r in x_refs]
    xb = jnp.concatenate([x.astype(BF16) for x in xs], axis=1)
    n_tile = S5_SW // LANES
    nh = n_tile // 2

    def to_tiles(ref, val):
        for j in range(n_tile):
            ref[j] = val[:, j * LANES:(j + 1) * LANES]

    def from_tiles(ref):
        return jnp.concatenate([ref[j].astype(BF16) for j in range(n_tile)], axis=1)

    to_tiles(sf_ref, _dot(xb, ff_ref[0]))
    to_tiles(sb_ref, _dot(xb, fb_ref[0]))

    def coeff(d):
        return [(at_ref[d, 0, 0:1, j * LANES:(j + 1) * LANES],
                 at_ref[d, 0, 1:2, j * LANES:(j + 1) * LANES]) for j in range(nh)]

    def advance(ref, rows, a, state):
        out = []
        for j in range(nh):
            (ar, ai), (hr, hi) = a[j], state[j]
            sr, si = ref[j, rows, :], ref[nh + j, rows, :]
            ref[j, rows, :] = hr
            ref[nh + j, rows, :] = hi
            out.append((ar * hr - ai * hi + sr, ar * hi + ai * hr + si))
        return out

    af, ab = coeff(0), coeff(1)

    def step(i, carry):
        hf, hb = carry
        hf = advance(sf_ref, pl.ds(i, n_seq, stride=n_chunk), af, hf)
        hb = advance(sb_ref, pl.ds(n_chunk - 1 - i, n_seq, stride=n_chunk), ab, hb)
        return hf, hb

    def initial(d):
        return [(h0_ref[d, 0, :, j * LANES:(j + 1) * LANES],
                 h0_ref[d, 0, :, (nh + j) * LANES:(nh + j + 1) * LANES]) for j in range(nh)]

    finals = lax.fori_loop(0, n_chunk, step, (initial(0), initial(1)))
    for d in range(2):
        for j in range(nh):
            fin_ref[d, 0, :, j * LANES:(j + 1) * LANES] = finals[d][j][0]
            fin_ref[d, 0, :, (nh + j) * LANES:(nh + j + 1) * LANES] = finals[d][j][1]

    y = (_dot(xb, toep_ref[0]) + _dot(from_tiles(sf_ref), ef_ref[0])
         + _dot(from_tiles(sb_ref), eb_ref[0]))
    d = d_ref[0]
    for t in range(T):
        yt = y[:, t * LANES:(t + 1) * LANES] + d * xs[t]
        out_refs[t][...] = jax.nn.gelu(yt).astype(BF16)


def _s5_mixer(u, weights, h0, d_skip):
    B, L, _ = u.shape
    T = S5_T
    n_chunk = L // T
    M = B * n_chunk
    toep, ff, fb, ef, eb, at = weights
    u2 = u.reshape(M, T * S5_WIDTH)
    col_spec = lambda s: pl.BlockSpec((M, LANES), lambda o: (0, s * S5_OCT + o))
    w_spec = lambda r, c: pl.BlockSpec((1, r, c), lambda o: (o, 0, 0))
    TL = T * LANES
    in_specs = [col_spec(s) for s in range(T)] + [
        w_spec(TL, TL), w_spec(TL, S5_SW), w_spec(TL, S5_SW), w_spec(S5_SW, TL), w_spec(S5_SW, TL),
        pl.BlockSpec((2, 1, 2, S5_SW // 2), lambda o: (0, o, 0, 0)),
        pl.BlockSpec((2, 1, B, S5_SW), lambda o: (0, o, 0, 0)),
        pl.BlockSpec((1, 1, LANES), lambda o: (o, 0, 0))]
    out_specs = [pl.BlockSpec((M, LANES), lambda o: (0, o))] * T + [
        pl.BlockSpec((2, 1, B, S5_SW), lambda o: (0, o, 0, 0))]
    out_shape = [jax.ShapeDtypeStruct((M, S5_WIDTH), BF16)] * T + [
        jax.ShapeDtypeStruct((2, S5_OCT, B, S5_SW), F32)]
    res = pl.pallas_call(
        functools.partial(_s5_kernel, n_seq=B, n_chunk=n_chunk),
        grid=(S5_OCT,),
        in_specs=in_specs, out_specs=out_specs, out_shape=out_shape,
        scratch_shapes=[pltpu.VMEM((S5_SW // LANES, M, LANES), F32)] * 2,
        compiler_params=_params(1),
        name="s5_mixer",
    )(*([u2] * T), toep, ff, fb, ef, eb, at, h0, d_skip.reshape(S5_OCT, 1, LANES))
    return res[:T], res[T]


def _mlp(h, w1_ref, w2_ref):
    hb = h.astype(BF16)
    acc = None
    for j in range(FF_HIDDEN // FF_CHUNK):
        a = _dot(hb, w1_ref[:, j * FF_CHUNK:(j + 1) * FF_CHUNK])
        a = jnp.square(jnp.maximum(a, 0.0)).astype(BF16)
        p = _dot(a, w2_ref[j * FF_CHUNK:(j + 1) * FF_CHUNK, :])
        acc = p if acc is None else acc + p
    return acc


def _post0_kernel(x_ref, attn_ref, g_ref, mod_ref, ng_ref, wglu_ref, wout_ref, w1_ref, w2_ref, o_ref):
    g = g_ref[0]
    s5 = g.astype(F32) * jax.nn.sigmoid(_dot(g, wglu_ref[...]))
    oc = (_dot(attn_ref[0], wout_ref[0:DA_WIDTH, :])
          + _dot(s5.astype(BF16), wout_ref[DA_WIDTH:D_MODEL, :]))
    x1 = x_ref[0] + mod_ref[0, 2:3, :] * oc
    h = _rms(x1, ng_ref[...]) * (1.0 + mod_ref[0, 4:5, :]) + mod_ref[0, 3:4, :]
    o_ref[0] = x1 + mod_ref[0, 5:6, :] * _mlp(h, w1_ref, w2_ref)


def _post0(x, attn, g, mods, mod_row, norm_g, w_glu, w_out, w1, w2, tm):
    B, L, _ = x.shape
    row_spec = lambda w: pl.BlockSpec((1, tm, w), lambda b, i: (b, i, 0))
    return pl.pallas_call(
        _post0_kernel,
        grid=(B, L // tm),
        in_specs=[row_spec(D_MODEL), row_spec(DA_WIDTH), row_spec(S5_WIDTH),
                  pl.BlockSpec((1, N_MOD, D_MODEL), lambda b, i: (mod_row(b), 0, 0)),
                  _const_spec((1, D_MODEL)),
                  _const_spec((S5_WIDTH, S5_WIDTH)),
                  _const_spec((D_MODEL, D_MODEL)),
                  _const_spec((D_MODEL, FF_HIDDEN)),
                  _const_spec((FF_HIDDEN, D_MODEL))],
        out_specs=row_spec(D_MODEL),
        out_shape=jax.ShapeDtypeStruct((B, L, D_MODEL), F32),
        compiler_params=_params(2),
        name="mix_out_mlp",
    )(x, attn, g, mods, norm_g.reshape(1, D_MODEL), w_glu, w_out, w1, w2)


def _layer1_kernel(x_ref, xp_ref, xn_ref, mod_ref, ng1_ref, ng2_ref, pw_ref, ps_ref,
                   w1_ref, w2_ref, fg_ref, o_ref, *, tm, seq_len):
    i = pl.program_id(1)
    H = POOL_HALO
    n = tm + 2 * H
    x = x_ref[0]
    xe = jnp.concatenate([xp_ref[0], x, xn_ref[0]], axis=0)
    he = _rms(xe, ng1_ref[...]) * (1.0 + mod_ref[0, 1:2, :]) + mod_ref[0, 0:1, :]
    pos_e = i * tm - H + lax.broadcasted_iota(jnp.int32, (n, 1), 0)
    he = jnp.where((pos_e >= 0) & (pos_e < seq_len), he, 0.0)
    pos = i * tm + lax.broadcasted_iota(jnp.int32, (tm, 1), 0)
    zs = []
    for gi, w in enumerate(POOL_WINDOWS):
        hg = he[:, gi * POOL_CH:(gi + 1) * POOL_CH]
        acc = hg + pltpu.roll(hg, 1, axis=0)
        r = 1
        while 2 * r < w:
            acc = pltpu.roll(acc, r, axis=0) + pltpu.roll(acc, n - r, axis=0)
            r *= 2
        lo = jnp.clip(pos - w // 2, 0, seq_len)
        hi = jnp.clip(pos + w // 2, 0, seq_len)
        mean = acc[H:H + tm] / (hi - lo).astype(F32)
        z = mean - hg[H:H + tm]
        zs.append(_dot(z.astype(BF16), pw_ref[gi]))
    oc = jnp.concatenate(zs, axis=1) * ps_ref[...]
    x1 = x + mod_ref[0, 2:3, :] * oc
    h = _rms(x1, ng2_ref[...]) * (1.0 + mod_ref[0, 4:5, :]) + mod_ref[0, 3:4, :]
    x2 = x1 + mod_ref[0, 5:6, :] * _mlp(h, w1_ref, w2_ref)
    o_ref[0] = _rms(x2, fg_ref[...])


def _layer1(x, mods, mod_row, norm_g, pool_w, pool_scale, w1, w2, final_g, tm):
    B, L, _ = x.shape
    H = POOL_HALO
    per = tm // H
    last = L // H - 1
    row_spec = pl.BlockSpec((1, tm, D_MODEL), lambda b, i: (b, i, 0))
    return pl.pallas_call(
        functools.partial(_layer1_kernel, tm=tm, seq_len=L),
        grid=(B, L // tm),
        in_specs=[row_spec,
                  pl.BlockSpec((1, H, D_MODEL), lambda b, i: (b, jnp.maximum(i * per - 1, 0), 0)),
                  pl.BlockSpec((1, H, D_MODEL), lambda b, i: (b, jnp.minimum((i + 1) * per, last), 0)),
                  pl.BlockSpec((1, N_MOD, D_MODEL), lambda b, i: (mod_row(b), 0, 0)),
                  _const_spec((1, D_MODEL)), _const_spec((1, D_MODEL)),
                  _const_spec((len(POOL_WINDOWS), POOL_CH, POOL_CH)),
                  _const_spec((1, D_MODEL)),
                  _const_spec((D_MODEL, FF_HIDDEN)),
                  _const_spec((FF_HIDDEN, D_MODEL)),
                  _const_spec((1, D_MODEL))],
        out_specs=row_spec,
        out_shape=jax.ShapeDtypeStruct((B, L, D_MODEL), F32),
        compiler_params=_params(2),
        name="pool_mlp_norm",
    )(x, x, x, mods, norm_g[0].reshape(1, D_MODEL), norm_g[1].reshape(1, D_MODEL),
      pool_w, pool_scale.reshape(1, D_MODEL), w1, w2, final_g.reshape(1, D_MODEL))


def _merge_columns(parts, B, L):
    return jnp.stack(parts, axis=1).reshape(B, L, S5_WIDTH)


def kernel(x_prompt, x_sample, cache_k, cache_v, state_s5, c, c_ctx, mod_w, mod_b, norm_g, mix_w_in, mix_w_out, diff_lambda_qk, diff_subln_g, s5_lambda_re, s5_lambda_im, s5_log_dt, s5_b_re, s5_b_im, s5_c_re, s5_c_im, s5_d, s5_w_glu, pool_w, pool_scale, ff_w1, ff_w2, final_norm_g):
    B_ctx, L_ctx, _ = x_prompt.shape
    B_lat, L_lat, _ = x_sample.shape
    ctx_row = B_lat
    n_rows = 8
    cc = jnp.concatenate([c, c_ctx[None], jnp.zeros((n_rows - B_lat - 1, D_MODEL), F32)], axis=0)
    mods = _modulation(cc, mod_w, mod_b).reshape(DEPTH, n_rows, N_MOD, D_MODEL)
    lat_row = lambda b: b
    ctx_mod_row = lambda b: ctx_row

    j = 0
    lam_init = 0.8 - 0.6 * math.exp(-0.3 * 0)
    w_in = mix_w_in[j].astype(BF16)
    w_out = mix_w_out[j].astype(BF16)
    w_glu = s5_w_glu[j].astype(BF16)
    w1, w2 = ff_w1[0].astype(BF16), ff_w2[0].astype(BF16)
    s5w = _s5_weights(s5_lambda_re[j], s5_lambda_im[j], s5_log_dt[j], s5_b_re[j], s5_b_im[j],
                      s5_c_re[j], s5_c_im[j])

    qc, kc, vc, uc, kc32, vc32 = _premix(x_prompt, mods[0], ctx_mod_row, norm_g[0, 0], w_in,
                                         None, 256, True)
    ql, kl, vl, ul = _premix(x_sample, mods[0], lat_row, norm_g[0, 0], w_in,
                             _rope_tables(L_lat), 512, False)

    ac = _attention(qc, kc, vc, None, diff_lambda_qk[j], diff_subln_g[j], lam_init, 256)
    P = cache_k.shape[2]
    cache = (cache_k[:, j].reshape(B_lat, P, DA_WIDTH), cache_v[:, j].reshape(B_lat, P, DA_WIDTH))
    al = _attention(ql, kl, vl, cache, diff_lambda_qk[j], diff_subln_g[j], lam_init, 256)

    h0_ctx = jnp.zeros((2, S5_OCT, B_ctx, S5_SW), F32)
    h0_lat = state_s5[:, j].reshape(B_lat, 2, S5_OCT, S5_GPO, S5_STATE, 2)
    h0_lat = h0_lat.transpose(1, 2, 0, 5, 3, 4).reshape(2, S5_OCT, B_lat, S5_SW)
    gc_parts, fin_c = _s5_mixer(uc, s5w, h0_ctx, s5_d[j])
    gl_parts, _ = _s5_mixer(ul, s5w, h0_lat, s5_d[j])
    gc = _merge_columns(gc_parts, B_ctx, L_ctx)
    gl = _merge_columns(gl_parts, B_lat, L_lat)

    ctx = _post0(x_prompt, ac, gc, mods[0], ctx_mod_row, norm_g[0, 1], w_glu, w_out, w1, w2, 256)
    lat = _post0(x_sample, al, gl, mods[0], lat_row, norm_g[0, 1], w_glu, w_out, w1, w2, 512)

    w1, w2 = ff_w1[1].astype(BF16), ff_w2[1].astype(BF16)
    pw = pool_w[0].astype(BF16)
    y_prompt = _layer1(ctx, mods[1], ctx_mod_row, norm_g[1], pw, pool_scale[0], w1, w2,
                       final_norm_g, 256)
    y_sample = _layer1(lat, mods[1], lat_row, norm_g[1], pw, pool_scale[0], w1, w2,
                       final_norm_g, 512)

    new_cache_k = kc32.reshape(B_ctx, 1, L_ctx, DA_HEADS, 2 * DA_QK)
    new_cache_v = vc32.reshape(B_ctx, 1, L_ctx, DA_HEADS, DA_V)
    st = fin_c.reshape(2, S5_OCT, B_ctx, 2, S5_GPO, S5_STATE)
    new_state = st.transpose(2, 0, 1, 4, 5, 3).reshape(B_ctx, 1, 2, S5_GROUPS, S5_STATE, 2)
    return (y_prompt, y_sample, new_cache_k, new_cache_v, new_state)
```

```python
import functools
import math

import jax
import jax.numpy as jnp
from jax import lax
from jax.experimental import pallas as pl
from jax.experimental.pallas import tpu as pltpu

D_MODEL = 1024
DEPTH = 2
GRID_W = 64
DA_HEADS = 4
DA_QK = 64
DA_V = 2 * DA_QK
DA_WIDTH = DA_HEADS * DA_V
S5_WIDTH = D_MODEL - DA_WIDTH
S5_CH = 16
S5_GROUPS = S5_WIDTH // S5_CH
S5_STATE = 64
IN_WIDTH = 3 * DA_WIDTH + S5_WIDTH
POOL_WINDOWS = (2, 4, 8, 16)
POOL_CH = D_MODEL // len(POOL_WINDOWS)
FF_HIDDEN = 4 * D_MODEL
N_MOD = 6
ROPE_BASE = 10000.0
EPS = 1e-6

LANES = 128
S5_T = 8
S5_OCT = S5_WIDTH // LANES
S5_GPO = LANES // S5_CH
S5_SW = 2 * S5_GPO * S5_STATE
POOL_HALO = 16
FF_CHUNK = 1024
VMEM_LIMIT = 56 * 1024 * 1024

F32 = jnp.float32
BF16 = jnp.bfloat16


def _const_spec(shape):
    nd = len(shape)
    return pl.BlockSpec(shape, lambda *_: (0,) * nd, pipeline_mode=pl.Buffered(1))


def _params(n_axes):
    return pltpu.CompilerParams(dimension_semantics=("arbitrary",) * n_axes,
                                vmem_limit_bytes=VMEM_LIMIT)


def _rms(x, g):
    return x * lax.rsqrt(jnp.mean(x * x, axis=-1, keepdims=True) + EPS) * g


def _dot(a, b):
    return jnp.dot(a, b, preferred_element_type=F32)


def _mod_kernel(c_ref, w_ref, b_ref, o_ref):
    c = c_ref[...]
    s = (c * jax.nn.sigmoid(c)).astype(BF16)
    o_ref[0] = _dot(s, w_ref[0].astype(BF16)) + b_ref[0]


def _modulation(cc, mod_w, mod_b):
    rows = cc.shape[0]
    tn = 1536
    n = N_MOD * D_MODEL
    return pl.pallas_call(
        _mod_kernel,
        grid=(DEPTH, n // tn),
        in_specs=[pl.BlockSpec((rows, D_MODEL), lambda l, j: (0, 0)),
                  pl.BlockSpec((1, D_MODEL, tn), lambda l, j: (l, 0, j)),
                  pl.BlockSpec((1, 1, tn), lambda l, j: (l, 0, j))],
        out_specs=pl.BlockSpec((1, rows, tn), lambda l, j: (l, 0, j)),
        out_shape=jax.ShapeDtypeStruct((DEPTH, rows, n), F32),
        compiler_params=_params(2),
        name="modulation",
    )(cc, mod_w, mod_b.reshape(DEPTH, 1, n))


def _premix_kernel(*refs, rope, cache_out):
    x_ref, mod_ref, g_ref, w_ref = refs[:4]
    refs = refs[4:]
    if rope:
        cos_ref, sa_ref, sb_ref = refs[:3]
        refs = refs[3:]
    q_ref, k_ref, v_ref, u_ref = refs[:4]
    x = x_ref[0]
    h = _rms(x, g_ref[...]) * (1.0 + mod_ref[0, 1:2, :]) + mod_ref[0, 0:1, :]
    z = _dot(h.astype(BF16), w_ref[...])
    n_slab = 2 * DA_HEADS

    def rot(s):
        if not rope:
            return s
        return (s * cos_ref[...] + pltpu.roll(s, LANES - 16, axis=1) * sa_ref[...]
                + pltpu.roll(s, 16, axis=1) * sb_ref[...])

    for j in range(n_slab):
        s = rot(z[:, j * LANES:(j + 1) * LANES])
        if j < DA_HEADS:
            q_ref[0, :, j * LANES:(j + 1) * LANES] = (s * DA_QK ** -0.5).astype(BF16)
        else:
            jj = j - DA_HEADS
            k_ref[0, :, jj * LANES:(jj + 1) * LANES] = s.astype(BF16)
    v = z[:, 2 * DA_WIDTH:3 * DA_WIDTH]
    v_ref[0] = v.astype(BF16)
    for o in range(S5_OCT):
        uo = z[:, 3 * DA_WIDTH + o * LANES:3 * DA_WIDTH + (o + 1) * LANES]
        u_ref[:, o] = uo.reshape(uo.shape[0] // S5_T, S5_T, LANES)
    if cache_out:
        kf_ref, vf_ref = refs[4:6]
        kf_ref[0] = z[:, DA_WIDTH:2 * DA_WIDTH]
        vf_ref[0] = v


def _chunk_spec(L, tm):
    return pl.BlockSpec((tm // S5_T, S5_OCT, S5_T, LANES),
                        lambda b, i: (b * (L // tm) + i, 0, 0, 0))


def _premix(x, mods, mod_row, norm_g, w_in, rope_tabs, tm, cache_out):
    B, L, _ = x.shape
    rope = rope_tabs is not None
    row_spec = lambda w: pl.BlockSpec((1, tm, w), lambda b, i: (b, i, 0))
    in_specs = [row_spec(D_MODEL),
                pl.BlockSpec((1, N_MOD, D_MODEL), lambda b, i: (mod_row(b), 0, 0)),
                _const_spec((1, D_MODEL)),
                _const_spec((D_MODEL, IN_WIDTH))]
    args = [x, mods, norm_g.reshape(1, D_MODEL), w_in]
    if rope:
        in_specs += [pl.BlockSpec((tm, LANES), lambda b, i: (i, 0))] * 3
        args += list(rope_tabs)
    out_specs = [row_spec(DA_WIDTH)] * 3 + [_chunk_spec(L, tm)]
    out_shape = [jax.ShapeDtypeStruct((B, L, DA_WIDTH), BF16)] * 3 + [
        jax.ShapeDtypeStruct((B * L // S5_T, S5_OCT, S5_T, LANES), F32)]
    if cache_out:
        out_specs += [row_spec(DA_WIDTH)] * 2
        out_shape += [jax.ShapeDtypeStruct((B, L, DA_WIDTH), F32)] * 2
    return pl.pallas_call(
        functools.partial(_premix_kernel, rope=rope, cache_out=cache_out),
        grid=(B, L // tm),
        in_specs=in_specs, out_specs=out_specs, out_shape=out_shape,
        compiler_params=_params(2),
        name="premix_rope" if rope else "premix",
    )(*args)


def _rope_tables(L):
    rows = L // GRID_W
    row = jnp.repeat(jnp.arange(rows), GRID_W).astype(F32)
    col = jnp.tile(jnp.arange(GRID_W), rows).astype(F32)
    half = DA_QK // 2
    inv = 1.0 / (ROPE_BASE ** (jnp.arange(0, half, 2, dtype=F32) / half))
    ang_r, ang_c = row[:, None] * inv, col[:, None] * inv
    zero = jnp.zeros_like(ang_r)
    cos64 = jnp.concatenate([jnp.cos(ang_r)] * 2 + [jnp.cos(ang_c)] * 2, axis=1)
    sa64 = jnp.concatenate([-jnp.sin(ang_r), zero, -jnp.sin(ang_c), zero], axis=1)
    sb64 = jnp.concatenate([zero, jnp.sin(ang_r), zero, jnp.sin(ang_c)], axis=1)
    return tuple(jnp.tile(t, (1, LANES // DA_QK)) for t in (cos64, sa64, sb64))


def _attn_kernel(*refs, lam_init, has_cache):
    lqk_ref, g_ref, q_ref, k_ref, v_ref = refs[:5]
    if has_cache:
        ck_ref, cv_ref, o_ref = refs[5:8]
    else:
        o_ref = refs[5]
    lqk = lqk_ref[...]
    lam = (jnp.exp(jnp.sum(lqk[0:1] * lqk[1:2], axis=-1, keepdims=True))
           - jnp.exp(jnp.sum(lqk[2:3] * lqk[3:4], axis=-1, keepdims=True)) + lam_init)
    q = q_ref[0]
    lane = lax.broadcasted_iota(jnp.int32, q.shape, 1)
    zero = jnp.zeros_like(q)
    segs = [(k_ref[0], v_ref[0])]
    if has_cache:
        segs.append((ck_ref[0].astype(BF16), cv_ref[0].astype(BF16)))
    nt = (((1,), (1,)), ((), ()))
    outs = []
    for qm in (jnp.where(lane < DA_QK, q, zero), jnp.where(lane >= DA_QK, q, zero)):
        ss = [lax.dot_general(qm, k, nt, preferred_element_type=F32) for k, _ in segs]
        m = functools.reduce(jnp.maximum, [jnp.max(s, axis=-1, keepdims=True) for s in ss])
        es = [jnp.exp(s - m) for s in ss]
        l = functools.reduce(jnp.add, [jnp.sum(e, axis=-1, keepdims=True) for e in es])
        o = functools.reduce(jnp.add, [_dot(e.astype(BF16), v) for e, (_, v) in zip(es, segs)])
        outs.append(o / l)
    o = outs[0] - lam * outs[1]
    o_ref[0] = (_rms(o, g_ref[...]) * (1.0 - lam_init)).astype(BF16)


def _attention(q, k, v, cache, lambda_qk, subln_g, lam_init, tq):
    B, L, _ = q.shape
    has_cache = cache is not None
    q_spec = pl.BlockSpec((1, tq, LANES), lambda b, h, i: (b, i, h))
    kv_spec = lambda n: pl.BlockSpec((1, n, LANES), lambda b, h, i: (b, 0, h))
    in_specs = [_const_spec((4, DA_QK)), _const_spec((1, DA_V)), q_spec, kv_spec(L), kv_spec(L)]
    args = [lambda_qk, subln_g.reshape(1, DA_V), q, k, v]
    if has_cache:
        P = cache[0].shape[1]
        in_specs += [kv_spec(P), kv_spec(P)]
        args += list(cache)
    return pl.pallas_call(
        functools.partial(_attn_kernel, lam_init=lam_init, has_cache=has_cache),
        grid=(B, DA_HEADS, L // tq),
        in_specs=in_specs, out_specs=q_spec,
        out_shape=jax.ShapeDtypeStruct((B, L, DA_WIDTH), BF16),
        compiler_params=_params(3),
        name="diff_attn_cache" if has_cache else "diff_attn",
    )(*args)


def _s5_weights(lam_re, lam_im, log_dt, b_re, b_im, c_re, c_im):
    T, G, P = S5_T, S5_GROUPS, S5_STATE
    hp = lax.Precision.HIGHEST
    dt = jnp.exp(log_dt)[..., None]
    lr, li = lam_re * dt, lam_im * dt
    kk = jnp.arange(T + 1, dtype=F32)[:, None, None, None]
    pw_re = jnp.exp(kk * lr) * jnp.cos(kk * li)
    pw_im = jnp.exp(kk * lr) * jnp.sin(kk * li)
    a_re, a_im = pw_re[1], pw_im[1]
    den = lam_re * lam_re + lam_im * lam_im
    f_re = ((a_re - 1.0) * lam_re + a_im * lam_im) / den
    f_im = (a_im * lam_re - (a_re - 1.0) * lam_im) / den
    bb_re = f_re[..., None] * b_re - f_im[..., None] * b_im
    bb_im = f_re[..., None] * b_im + f_im[..., None] * b_re

    def in_to_state(d, pows):
        pr, pi = pw_re[pows, d], pw_im[pows, d]
        re = pr[..., None] * bb_re[d][None] - pi[..., None] * bb_im[d][None]
        im = pr[..., None] * bb_im[d][None] + pi[..., None] * bb_re[d][None]
        f = jnp.stack([re, im], axis=0)
        f = f.reshape(2, T, S5_OCT, S5_GPO, P, S5_CH).transpose(2, 0, 1, 3, 5, 4)
        f = f.reshape(S5_OCT, 2, T, LANES, P)
        return jnp.concatenate([f, f], axis=-1)

    def state_to_out(d, pows):
        pr, pi = pw_re[pows, d], pw_im[pows, d]
        cr, ci = c_re[d], c_im[d]
        re = cr[None] * pr[:, :, None, :] - ci[None] * pi[:, :, None, :]
        im = cr[None] * pi[:, :, None, :] + ci[None] * pr[:, :, None, :]
        e = jnp.stack([re, -im], axis=0)
        e = e.reshape(2, T, S5_OCT, S5_GPO, S5_CH, P).transpose(2, 0, 1, 5, 3, 4)
        e = e.reshape(S5_OCT, 2, T, P, LANES)
        return jnp.concatenate([e, e], axis=-2)

    lags = jnp.arange(T)
    fc = jnp.stack([in_to_state(0, T - 1 - lags), in_to_state(1, lags)], axis=1)
    ec = jnp.stack([state_to_out(0, lags + 1), state_to_out(1, T - lags)], axis=1)

    def lag_kernels(d):
        pr, pi = pw_re[:T, d], pw_im[:T, d]
        cr, ci = c_re[d], c_im[d]
        car = cr[None] * pr[:, :, None, :] - ci[None] * pi[:, :, None, :]
        cai = cr[None] * pi[:, :, None, :] + ci[None] * pr[:, :, None, :]
        return (jnp.einsum('tgip,gpj->tgji', car, bb_re[d], precision=hp)
                - jnp.einsum('tgip,gpj->tgji', cai, bb_im[d], precision=hp))

    kf, kb = lag_kernels(0), lag_kernels(1)
    klag = jnp.concatenate([kb[:0:-1], (kf[0] + kb[0])[None], kf[1:]], axis=0)
    klag = klag.reshape(2 * T - 1, S5_OCT, S5_GPO, S5_CH, S5_CH)
    eye = jnp.eye(S5_GPO, dtype=F32)
    kt = jnp.einsum('loghi,gk->olghki', klag, eye).reshape(S5_OCT, 2 * T - 1, LANES, LANES)

    at = jnp.stack([pw_re[T], pw_im[T]], axis=1)
    at = at.reshape(2, 2, S5_OCT, S5_GPO * P).transpose(0, 2, 1, 3)
    return kt, fc, ec, at


def _s5_kernel(*refs, n_seq, n_chunk):
    T = S5_T
    (x_ref, kt_ref, fc_ref, ec_ref, at_ref, h0_ref, d_ref, out_ref, fin_ref,
     sf_ref, sb_ref, wt_ref, wf_ref, we_ref) = refs
    n_tile = S5_SW // LANES
    nh = n_tile // 2

    @pl.when(pl.program_id(1) == 0)
    def _():
        _s5_expand(kt_ref, fc_ref, ec_ref, wt_ref, wf_ref, we_ref)

    x = x_ref[...]
    xb = x.astype(BF16)

    def to_tiles(ref, val):
        for j in range(n_tile):
            ref[j] = val[:, j * LANES:(j + 1) * LANES]

    def from_tiles(ref):
        return jnp.concatenate([ref[j].astype(BF16) for j in range(n_tile)], axis=1)

    to_tiles(sf_ref, _dot(xb, wf_ref[0]))
    to_tiles(sb_ref, _dot(xb, wf_ref[1]))

    def coeff(d):
        return [(at_ref[d, 0, 0:1, j * LANES:(j + 1) * LANES],
                 at_ref[d, 0, 1:2, j * LANES:(j + 1) * LANES]) for j in range(nh)]

    def advance(ref, rows, a, state):
        out = []
        for j in range(nh):
            (ar, ai), (hr, hi) = a[j], state[j]
            sr, si = ref[j, rows, :], ref[nh + j, rows, :]
            ref[j, rows, :] = hr
            ref[nh + j, rows, :] = hi
            out.append((ar * hr - ai * hi + sr, ar * hi + ai * hr + si))
        return out

    af, ab = coeff(0), coeff(1)

    def step(i, carry):
        hf, hb = carry
        hf = advance(sf_ref, pl.ds(i, n_seq, stride=n_chunk), af, hf)
        hb = advance(sb_ref, pl.ds(n_chunk - 1 - i, n_seq, stride=n_chunk), ab, hb)
        return hf, hb

    def initial(d):
        return [(h0_ref[d, 0, 0, :, j * LANES:(j + 1) * LANES],
                 h0_ref[d, 0, 0, :, (nh + j) * LANES:(nh + j + 1) * LANES]) for j in range(nh)]

    finals = lax.fori_loop(0, n_chunk, step, (initial(0), initial(1)))
    for d in range(2):
        for j in range(nh):
            fin_ref[d, 0, 0, :, j * LANES:(j + 1) * LANES] = finals[d][j][0]
            fin_ref[d, 0, 0, :, (nh + j) * LANES:(nh + j + 1) * LANES] = finals[d][j][1]

    y = (_dot(xb, wt_ref[...]) + _dot(from_tiles(sf_ref), we_ref[0])
         + _dot(from_tiles(sb_ref), we_ref[1]))
    d = jnp.concatenate([d_ref[0]] * T, axis=1)
    out_ref[...] = jax.nn.gelu(y + d * x)


def _s5_expand(kt_ref, fc_ref, ec_ref, wt_ref, wf_ref, we_ref):
    T = S5_T
    nh = S5_SW // LANES // 2
    for s in range(T):
        for t in range(T):
            wt_ref[s * LANES:(s + 1) * LANES, t * LANES:(t + 1) * LANES] = (
                kt_ref[0, t - s + T - 1].astype(BF16))
    row = lax.broadcasted_iota(jnp.int32, (LANES, LANES), 0)
    lane = lax.broadcasted_iota(jnp.int32, (LANES, LANES), 1)
    gpt = LANES // S5_STATE
    own_f = [(row // S5_CH) == (gpt * j + lane // S5_STATE) for j in range(nh)]
    own_e = [(lane // S5_CH) == (gpt * j + row // S5_STATE) for j in range(nh)]
    for d in range(2):
        for ri in range(2):
            for s in range(T):
                f, e = fc_ref[0, d, ri, s], ec_ref[0, d, ri, s]
                rows = slice(s * LANES, (s + 1) * LANES)
                for j in range(nh):
                    cols = slice((ri * nh + j) * LANES, (ri * nh + j + 1) * LANES)
                    wf_ref[d, rows, cols] = jnp.where(own_f[j], f, 0.0).astype(BF16)
                    we_ref[d, cols, rows] = jnp.where(own_e[j], e, 0.0).astype(BF16)


def _s5_mixer(u, weights, h0, d_skip, n_seq, n_split):
    T = S5_T
    M_all = u.shape[0]
    M = M_all // n_split
    B = n_seq // n_split
    n_chunk = M // B
    kt, fc, ec, at = weights
    TL = T * LANES
    x_spec = pl.BlockSpec((M, TL), lambda o, j: (j, o))
    tile_spec = pl.BlockSpec((1, 2, 2, T, LANES, LANES), lambda o, j: (o, 0, 0, 0, 0, 0))
    st_spec = pl.BlockSpec((2, 1, 1, B, S5_SW), lambda o, j: (0, o, j, 0, 0))
    g, fin = pl.pallas_call(
        functools.partial(_s5_kernel, n_seq=B, n_chunk=n_chunk),
        grid=(S5_OCT, n_split),
        in_specs=[x_spec,
                  pl.BlockSpec((1, 2 * T - 1, LANES, LANES), lambda o, j: (o, 0, 0, 0)),
                  tile_spec, tile_spec,
                  pl.BlockSpec((2, 1, 2, S5_SW // 2), lambda o, j: (0, o, 0, 0)),
                  st_spec,
                  pl.BlockSpec((1, 1, LANES), lambda o, j: (o, 0, 0))],
        out_specs=[x_spec, st_spec],
        out_shape=[jax.ShapeDtypeStruct((M_all, S5_OCT * TL), F32),
                   jax.ShapeDtypeStruct((2, S5_OCT, n_split, B, S5_SW), F32)],
        scratch_shapes=[pltpu.VMEM((S5_SW // LANES, M, LANES), F32)] * 2 + [
            pltpu.VMEM((TL, TL), BF16), pltpu.VMEM((2, TL, S5_SW), BF16),
            pltpu.VMEM((2, S5_SW, TL), BF16)],
        compiler_params=_params(2),
        name="s5_mixer",
    )(u.reshape(M_all, S5_OCT * TL), kt, fc, ec, at, h0, d_skip.reshape(S5_OCT, 1, LANES))
    return g.reshape(M_all, S5_OCT, T, LANES), fin


def _mlp(h, w1_ref, w2_ref):
    hb = h.astype(BF16)
    acc = None
    for j in range(FF_HIDDEN // FF_CHUNK):
        a = _dot(hb, w1_ref[:, j * FF_CHUNK:(j + 1) * FF_CHUNK])
        a = jnp.square(jnp.maximum(a, 0.0)).astype(BF16)
        p = _dot(a, w2_ref[j * FF_CHUNK:(j + 1) * FF_CHUNK, :])
        acc = p if acc is None else acc + p
    return acc


def _post0_kernel(x_ref, attn_ref, g_ref, mod_ref, ng_ref, wglu_ref, wout_ref, w1_ref, w2_ref, o_ref):
    tm = x_ref.shape[1]
    g = jnp.concatenate([g_ref[:, o].reshape(tm, LANES) for o in range(S5_OCT)], axis=1)
    s5 = g * jax.nn.sigmoid(_dot(g.astype(BF16), wglu_ref[...]))
    oc = (_dot(attn_ref[0], wout_ref[0:DA_WIDTH, :])
          + _dot(s5.astype(BF16), wout_ref[DA_WIDTH:D_MODEL, :]))
    x1 = x_ref[0] + mod_ref[0, 2:3, :] * oc
    h = _rms(x1, ng_ref[...]) * (1.0 + mod_ref[0, 4:5, :]) + mod_ref[0, 3:4, :]
    o_ref[0] = x1 + mod_ref[0, 5:6, :] * _mlp(h, w1_ref, w2_ref)


def _post0(x, attn, g, mods, mod_row, norm_g, w_glu, w_out, w1, w2, tm):
    B, L, _ = x.shape
    row_spec = lambda w: pl.BlockSpec((1, tm, w), lambda b, i: (b, i, 0))
    return pl.pallas_call(
        _post0_kernel,
        grid=(B, L // tm),
        in_specs=[row_spec(D_MODEL), row_spec(DA_WIDTH), _chunk_spec(L, tm),
                  pl.BlockSpec((1, N_MOD, D_MODEL), lambda b, i: (mod_row(b), 0, 0)),
                  _const_spec((1, D_MODEL)),
                  _const_spec((S5_WIDTH, S5_WIDTH)),
                  _const_spec((D_MODEL, D_MODEL)),
                  _const_spec((D_MODEL, FF_HIDDEN)),
                  _const_spec((FF_HIDDEN, D_MODEL))],
        out_specs=row_spec(D_MODEL),
        out_shape=jax.ShapeDtypeStruct((B, L, D_MODEL), F32),
        compiler_params=_params(2),
        name="mix_out_mlp",
    )(x, attn, g, mods, norm_g.reshape(1, D_MODEL), w_glu, w_out, w1, w2)


def _layer1_kernel(x_ref, xp_ref, xn_ref, mod_ref, ng1_ref, ng2_ref, pw_ref, ps_ref,
                   w1_ref, w2_ref, fg_ref, o_ref, *, tm, seq_len):
    i = pl.program_id(1)
    H = POOL_HALO
    n = tm + 2 * H
    x = x_ref[0]
    xe = jnp.concatenate([xp_ref[0], x, xn_ref[0]], axis=0)
    he = _rms(xe, ng1_ref[...]) * (1.0 + mod_ref[0, 1:2, :]) + mod_ref[0, 0:1, :]
    pos_e = i * tm - H + lax.broadcasted_iota(jnp.int32, (n, 1), 0)
    he = jnp.where((pos_e >= 0) & (pos_e < seq_len), he, 0.0)
    pos = i * tm + lax.broadcasted_iota(jnp.int32, (tm, 1), 0)
    zs = []
    for gi, w in enumerate(POOL_WINDOWS):
        hg = he[:, gi * POOL_CH:(gi + 1) * POOL_CH]
        acc = hg + pltpu.roll(hg, 1, axis=0)
        r = 1
        while 2 * r < w:
            acc = pltpu.roll(acc, r, axis=0) + pltpu.roll(acc, n - r, axis=0)
            r *= 2
        lo = jnp.clip(pos - w // 2, 0, seq_len)
        hi = jnp.clip(pos + w // 2, 0, seq_len)
        mean = acc[H:H + tm] / (hi - lo).astype(F32)
        z = mean - hg[H:H + tm]
        zs.append(_dot(z.astype(BF16), pw_ref[gi]))
    oc = jnp.concatenate(zs, axis=1) * ps_ref[...]
    x1 = x + mod_ref[0, 2:3, :] * oc
    h = _rms(x1, ng2_ref[...]) * (1.0 + mod_ref[0, 4:5, :]) + mod_ref[0, 3:4, :]
    x2 = x1 + mod_ref[0, 5:6, :] * _mlp(h, w1_ref, w2_ref)
    o_ref[0] = _rms(x2, fg_ref[...])


def _layer1(x, mods, mod_row, norm_g, pool_w, pool_scale, w1, w2, final_g, tm):
    B, L, _ = x.shape
    H = POOL_HALO
    per = tm // H
    last = L // H - 1
    row_spec = pl.BlockSpec((1, tm, D_MODEL), lambda b, i: (b, i, 0))
    return pl.pallas_call(
        functools.partial(_layer1_kernel, tm=tm, seq_len=L),
        grid=(B, L // tm),
        in_specs=[row_spec,
                  pl.BlockSpec((1, H, D_MODEL), lambda b, i: (b, jnp.maximum(i * per - 1, 0), 0)),
                  pl.BlockSpec((1, H, D_MODEL), lambda b, i: (b, jnp.minimum((i + 1) * per, last), 0)),
                  pl.BlockSpec((1, N_MOD, D_MODEL), lambda b, i: (mod_row(b), 0, 0)),
                  _const_spec((1, D_MODEL)), _const_spec((1, D_MODEL)),
                  _const_spec((len(POOL_WINDOWS), POOL_CH, POOL_CH)),
                  _const_spec((1, D_MODEL)),
                  _const_spec((D_MODEL, FF_HIDDEN)),
                  _const_spec((FF_HIDDEN, D_MODEL)),
                  _const_spec((1, D_MODEL))],
        out_specs=row_spec,
        out_shape=jax.ShapeDtypeStruct((B, L, D_MODEL), F32),
        compiler_params=_params(2),
        name="pool_mlp_norm",
    )(x, x, x, mods, norm_g[0].reshape(1, D_MODEL), norm_g[1].reshape(1, D_MODEL),
      pool_w, pool_scale.reshape(1, D_MODEL), w1, w2, final_g.reshape(1, D_MODEL))


def kernel(x_prompt, x_sample, cache_k, cache_v, state_s5, c, c_ctx, mod_w, mod_b, norm_g, mix_w_in, mix_w_out, diff_lambda_qk, diff_subln_g, s5_lambda_re, s5_lambda_im, s5_log_dt, s5_b_re, s5_b_im, s5_c_re, s5_c_im, s5_d, s5_w_glu, pool_w, pool_scale, ff_w1, ff_w2, final_norm_g):
    B_ctx, L_ctx, _ = x_prompt.shape
    B_lat, L_lat, _ = x_sample.shape
    ctx_row = B_lat
    n_rows = 8
    cc = jnp.concatenate([c, c_ctx[None], jnp.zeros((n_rows - B_lat - 1, D_MODEL), F32)], axis=0)
    mods = _modulation(cc, mod_w, mod_b).reshape(DEPTH, n_rows, N_MOD, D_MODEL)
    lat_row = lambda b: b
    ctx_mod_row = lambda b: ctx_row

    j = 0
    lam_init = 0.8 - 0.6 * math.exp(-0.3 * 0)
    w_in = mix_w_in[j].astype(BF16)
    w_out = mix_w_out[j].astype(BF16)
    w_glu = s5_w_glu[j].astype(BF16)
    w1, w2 = ff_w1[0].astype(BF16), ff_w2[0].astype(BF16)
    s5w = _s5_weights(s5_lambda_re[j], s5_lambda_im[j], s5_log_dt[j], s5_b_re[j], s5_b_im[j],
                      s5_c_re[j], s5_c_im[j])

    qc, kc, vc, uc, kc32, vc32 = _premix(x_prompt, mods[0], ctx_mod_row, norm_g[0, 0], w_in,
                                         None, 256, True)
    ql, kl, vl, ul = _premix(x_sample, mods[0], lat_row, norm_g[0, 0], w_in,
                             _rope_tables(L_lat), 512, False)

    ac = _attention(qc, kc, vc, None, diff_lambda_qk[j], diff_subln_g[j], lam_init, 256)
    P = cache_k.shape[2]
    cache = (cache_k[:, j].reshape(B_lat, P, DA_WIDTH), cache_v[:, j].reshape(B_lat, P, DA_WIDTH))
    al = _attention(ql, kl, vl, cache, diff_lambda_qk[j], diff_subln_g[j], lam_init, 256)

    lat_split = 2
    h0_ctx = jnp.zeros((2, S5_OCT, 1, B_ctx, S5_SW), F32)
    h0_lat = state_s5[:, j].reshape(B_lat, 2, S5_OCT, S5_GPO, S5_STATE, 2)
    h0_lat = h0_lat.transpose(1, 2, 0, 5, 3, 4).reshape(
        2, S5_OCT, lat_split, B_lat // lat_split, S5_SW)
    gc, fin_c = _s5_mixer(uc, s5w, h0_ctx, s5_d[j], B_ctx, 1)
    gl, _ = _s5_mixer(ul, s5w, h0_lat, s5_d[j], B_lat, lat_split)

    ctx = _post0(x_prompt, ac, gc, mods[0], ctx_mod_row, norm_g[0, 1], w_glu, w_out, w1, w2, 256)
    lat = _post0(x_sample, al, gl, mods[0], lat_row, norm_g[0, 1], w_glu, w_out, w1, w2, 512)

    w1, w2 = ff_w1[1].astype(BF16), ff_w2[1].astype(BF16)
    pw = pool_w[0].astype(BF16)
    y_prompt = _layer1(ctx, mods[1], ctx_mod_row, norm_g[1], pw, pool_scale[0], w1, w2,
                       final_norm_g, 256)
    y_sample = _layer1(lat, mods[1], lat_row, norm_g[1], pw, pool_scale[0], w1, w2,
                       final_norm_g, 512)

    new_cache_k = kc32.reshape(B_ctx, 1, L_ctx, DA_HEADS, 2 * DA_QK)
    new_cache_v = vc32.reshape(B_ctx, 1, L_ctx, DA_HEADS, DA_V)
    st = fin_c.reshape(2, S5_OCT, B_ctx, 2, S5_GPO, S5_STATE)
    new_state = st.transpose(2, 0, 1, 4, 5, 3).reshape(B_ctx, 1, 2, S5_GROUPS, S5_STATE, 2)
    return (y_prompt, y_sample, new_cache_k, new_cache_v, new_state)
```

```python
import functools
import math

import jax
import jax.numpy as jnp
from jax import lax
from jax.experimental import pallas as pl
from jax.experimental.pallas import tpu as pltpu

D_MODEL = 1024
DEPTH = 2
GRID_W = 64
DA_HEADS = 4
DA_QK = 64
DA_V = 2 * DA_QK
DA_WIDTH = DA_HEADS * DA_V
S5_WIDTH = D_MODEL - DA_WIDTH
S5_CH = 16
S5_GROUPS = S5_WIDTH // S5_CH
S5_STATE = 64
IN_WIDTH = 3 * DA_WIDTH + S5_WIDTH
POOL_WINDOWS = (2, 4, 8, 16)
POOL_CH = D_MODEL // len(POOL_WINDOWS)
FF_HIDDEN = 4 * D_MODEL
N_MOD = 6
ROPE_BASE = 10000.0
EPS = 1e-6

LANES = 128
S5_T = 8
S5_OCT = S5_WIDTH // LANES
S5_GPO = LANES // S5_CH
S5_SW = 2 * S5_GPO * S5_STATE
Q_SCALE = DA_QK ** -0.5 * math.log2(math.e)
POOL_HALO = 16
FF_CHUNK = 1024
VMEM_LIMIT = 56 * 1024 * 1024

F32 = jnp.float32
BF16 = jnp.bfloat16


def _const_spec(shape):
    nd = len(shape)
    return pl.BlockSpec(shape, lambda *_: (0,) * nd, pipeline_mode=pl.Buffered(1))


def _layer_spec(shape, layer):
    return pl.BlockSpec((None,) + shape, lambda *_: (layer, 0, 0), pipeline_mode=pl.Buffered(1))


def _params(n_axes):
    return pltpu.CompilerParams(dimension_semantics=("arbitrary",) * n_axes,
                                vmem_limit_bytes=VMEM_LIMIT)


def _rms(x, g):
    return x * lax.rsqrt(jnp.mean(x * x, axis=-1, keepdims=True) + EPS) * g


def _dot(a, b):
    return jnp.dot(a, b, preferred_element_type=F32)


def _mod_kernel(c_ref, w_ref, b_ref, o_ref):
    c = c_ref[...]
    s = (c * jax.nn.sigmoid(c)).astype(BF16)
    o_ref[0] = _dot(s, w_ref[0].astype(BF16)) + b_ref[0]


def _modulation(cc, mod_w, mod_b):
    rows = cc.shape[0]
    tn = 1536
    n = N_MOD * D_MODEL
    return pl.pallas_call(
        _mod_kernel,
        grid=(DEPTH, n // tn),
        in_specs=[pl.BlockSpec((rows, D_MODEL), lambda l, j: (0, 0)),
                  pl.BlockSpec((1, D_MODEL, tn), lambda l, j: (l, 0, j)),
                  pl.BlockSpec((1, 1, tn), lambda l, j: (l, 0, j))],
        out_specs=pl.BlockSpec((1, rows, tn), lambda l, j: (l, 0, j)),
        out_shape=jax.ShapeDtypeStruct((DEPTH, rows, n), F32),
        compiler_params=_params(2),
        name="modulation",
    )(cc, mod_w, mod_b.reshape(DEPTH, 1, n))


def _premix_kernel(*refs, rope, cache_out):
    x_ref, mod_ref, g_ref, w_ref = refs[:4]
    refs = refs[4:]
    if rope:
        cos_ref, sa_ref, sb_ref = refs[:3]
        refs = refs[3:]
    q_ref, k_ref, v_ref, u_ref = refs[:4]
    x = x_ref[0]
    h = _rms(x, g_ref[...]) * (1.0 + mod_ref[0, 1:2, :]) + mod_ref[0, 0:1, :]
    z = _dot(h.astype(BF16), w_ref[...])
    n_slab = 2 * DA_HEADS

    def rot(s):
        if not rope:
            return s
        return (s * cos_ref[...] + pltpu.roll(s, LANES - 16, axis=1) * sa_ref[...]
                + pltpu.roll(s, 16, axis=1) * sb_ref[...])

    for j in range(n_slab):
        s = rot(z[:, j * LANES:(j + 1) * LANES])
        if j < DA_HEADS:
            q_ref[0, :, j * LANES:(j + 1) * LANES] = (s * Q_SCALE).astype(BF16)
        else:
            jj = j - DA_HEADS
            k_ref[0, :, jj * LANES:(jj + 1) * LANES] = s.astype(BF16)
    v = z[:, 2 * DA_WIDTH:3 * DA_WIDTH]
    v_ref[0] = v.astype(BF16)
    ubuf_ref = refs[-1]
    n_chunk = z.shape[0] // S5_T
    for o in range(S5_OCT):
        ubuf_ref[o] = z[:, 3 * DA_WIDTH + o * LANES:3 * DA_WIDTH + (o + 1) * LANES]
    for o in range(S5_OCT):
        for s in range(S5_T):
            c0 = (o * S5_T + s) * LANES
            u_ref[:, c0:c0 + LANES] = ubuf_ref[o, pl.ds(s, n_chunk, stride=S5_T), :]
    if cache_out:
        kf_ref, vf_ref = refs[4:6]
        kf_ref[0] = z[:, DA_WIDTH:2 * DA_WIDTH]
        vf_ref[0] = v


def _chunk_spec(L, tm):
    return pl.BlockSpec((tm // S5_T, S5_WIDTH * S5_T), lambda b, i: (b * (L // tm) + i, 0))


def _premix(x, mods, mod_row, norm_g, w_in, rope_tabs, tm, cache_out):
    B, L, _ = x.shape
    rope = rope_tabs is not None
    row_spec = lambda w: pl.BlockSpec((1, tm, w), lambda b, i: (b, i, 0))
    in_specs = [row_spec(D_MODEL),
                pl.BlockSpec((1, N_MOD, D_MODEL), lambda b, i: (mod_row(b), 0, 0)),
                _const_spec((1, D_MODEL)),
                _const_spec((D_MODEL, IN_WIDTH))]
    args = [x, mods, norm_g.reshape(1, D_MODEL), w_in]
    if rope:
        in_specs += [pl.BlockSpec((tm, LANES), lambda b, i: (i, 0))] * 3
        args += list(rope_tabs)
    out_specs = [row_spec(DA_WIDTH)] * 3 + [_chunk_spec(L, tm)]
    out_shape = [jax.ShapeDtypeStruct((B, L, DA_WIDTH), BF16)] * 3 + [
        jax.ShapeDtypeStruct((B * L // S5_T, S5_WIDTH * S5_T), F32)]
    if cache_out:
        out_specs += [row_spec(DA_WIDTH)] * 2
        out_shape += [jax.ShapeDtypeStruct((B, L, DA_WIDTH), F32)] * 2
    return pl.pallas_call(
        functools.partial(_premix_kernel, rope=rope, cache_out=cache_out),
        grid=(B, L // tm),
        in_specs=in_specs, out_specs=out_specs, out_shape=out_shape,
        scratch_shapes=[pltpu.VMEM((S5_OCT, tm, LANES), F32)],
        compiler_params=_params(2),
        name="premix_rope" if rope else "premix",
    )(*args)


def _rope_tables(L):
    rows = L // GRID_W
    row = jnp.repeat(jnp.arange(rows), GRID_W).astype(F32)
    col = jnp.tile(jnp.arange(GRID_W), rows).astype(F32)
    half = DA_QK // 2
    inv = 1.0 / (ROPE_BASE ** (jnp.arange(0, half, 2, dtype=F32) / half))
    ang_r, ang_c = row[:, None] * inv, col[:, None] * inv
    zero = jnp.zeros_like(ang_r)
    cos64 = jnp.concatenate([jnp.cos(ang_r)] * 2 + [jnp.cos(ang_c)] * 2, axis=1)
    sa64 = jnp.concatenate([-jnp.sin(ang_r), zero, -jnp.sin(ang_c), zero], axis=1)
    sb64 = jnp.concatenate([zero, jnp.sin(ang_r), zero, jnp.sin(ang_c)], axis=1)
    return tuple(jnp.tile(t, (1, LANES // DA_QK)) for t in (cos64, sa64, sb64))


def _attn_kernel(*refs, lam_init, has_cache):
    lqk_ref, g_ref, q_ref, k_ref, v_ref = refs[:5]
    if has_cache:
        ck_ref, cv_ref, o_ref = refs[5:8]
    else:
        o_ref = refs[5]
    lqk = lqk_ref[...]
    lam = (jnp.exp(jnp.sum(lqk[0:1] * lqk[1:2], axis=-1, keepdims=True))
           - jnp.exp(jnp.sum(lqk[2:3] * lqk[3:4], axis=-1, keepdims=True)) + lam_init)
    q = q_ref[0]
    lane = lax.broadcasted_iota(jnp.int32, q.shape, 1)
    zero = jnp.zeros_like(q)
    segs = [(k_ref[0], v_ref[0])]
    if has_cache:
        segs.append((ck_ref[0].astype(BF16), cv_ref[0].astype(BF16)))
    nt = (((1,), (1,)), ((), ()))
    outs = []
    for qm in (jnp.where(lane < DA_QK, q, zero), jnp.where(lane >= DA_QK, q, zero)):
        ss = [lax.dot_general(qm, k, nt, preferred_element_type=F32) for k, _ in segs]
        m = functools.reduce(jnp.maximum, [jnp.max(s, axis=-1, keepdims=True) for s in ss])
        es = [jnp.exp2(s - m) for s in ss]
        l = functools.reduce(jnp.add, [jnp.sum(e, axis=-1, keepdims=True) for e in es])
        o = functools.reduce(jnp.add, [_dot(e.astype(BF16), v) for e, (_, v) in zip(es, segs)])
        outs.append(o / l)
    o = outs[0] - lam * outs[1]
    o_ref[0] = (_rms(o, g_ref[...]) * (1.0 - lam_init)).astype(BF16)


def _attention(q, k, v, cache, lambda_qk, subln_g, lam_init, tq):
    B, L, _ = q.shape
    has_cache = cache is not None
    q_spec = pl.BlockSpec((1, tq, LANES), lambda b, h, i: (b, i, h))
    kv_spec = lambda n: pl.BlockSpec((1, n, LANES), lambda b, h, i: (b, 0, h))
    in_specs = [_const_spec((4, DA_QK)), _const_spec((1, DA_V)), q_spec, kv_spec(L), kv_spec(L)]
    args = [lambda_qk, subln_g.reshape(1, DA_V), q, k, v]
    if has_cache:
        P = cache[0].shape[1]
        in_specs += [kv_spec(P), kv_spec(P)]
        args += list(cache)
    return pl.pallas_call(
        functools.partial(_attn_kernel, lam_init=lam_init, has_cache=has_cache),
        grid=(B, DA_HEADS, L // tq),
        in_specs=in_specs, out_specs=q_spec,
        out_shape=jax.ShapeDtypeStruct((B, L, DA_WIDTH), BF16),
        compiler_params=_params(3),
        name="diff_attn_cache" if has_cache else "diff_attn",
    )(*args)


def _s5_weights(lam_re, lam_im, log_dt, b_re, b_im, c_re, c_im):
    T, G, P, H = S5_T, S5_GROUPS, S5_STATE, S5_CH
    dt = jnp.exp(log_dt)[..., None]
    lr, li = lam_re * dt, lam_im * dt
    kk = jnp.arange(T + 1, dtype=F32)[None, :, None, None]
    mag = jnp.exp(kk * lr[:, None])
    pw_re = mag * jnp.cos(kk * li[:, None])
    pw_im = mag * jnp.sin(kk * li[:, None])
    a_re, a_im = pw_re[:, 1], pw_im[:, 1]
    den = lam_re * lam_re + lam_im * lam_im
    f_re = ((a_re - 1.0) * lam_re + a_im * lam_im) / den
    f_im = (a_im * lam_re - (a_re - 1.0) * lam_im) / den
    bt_re, bt_im = jnp.swapaxes(b_re, -1, -2), jnp.swapaxes(b_im, -1, -2)
    bb_re = f_re[:, :, None, :] * bt_re - f_im[:, :, None, :] * bt_im
    bb_im = f_re[:, :, None, :] * bt_im + f_im[:, :, None, :] * bt_re

    def per_dir(x, fwd, bwd):
        return jnp.stack([fwd(x[0]), bwd(x[1])], axis=0)

    flip = lambda x: x[::-1]
    pr = per_dir(pw_re, lambda x: flip(x[:T]), lambda x: x[:T])[:, :, :, None, :]
    pi = per_dir(pw_im, lambda x: flip(x[:T]), lambda x: x[:T])[:, :, :, None, :]
    br, bi = bb_re[:, None], bb_im[:, None]
    fc = jnp.stack([pr * br - pi * bi, pr * bi + pi * br], axis=1)
    fc = fc.reshape(2, 2, T, S5_OCT, LANES, P)
    fc = jnp.concatenate([fc, fc], axis=-1)

    def lanes_gh(x):
        x = x.reshape(x.shape[:-2] + (S5_OCT, S5_GPO, P))
        return jnp.repeat(jnp.swapaxes(x, -1, -2), H, axis=-1)

    def c_lanes(c):
        c = c.reshape(2, S5_OCT, S5_GPO, H, P).transpose(0, 1, 4, 2, 3)
        return c.reshape(2, 1, S5_OCT, P, LANES)

    per = lanes_gh(per_dir(pw_re, lambda x: x[1:], lambda x: flip(x[1:])))
    pei = lanes_gh(per_dir(pw_im, lambda x: x[1:], lambda x: flip(x[1:])))
    cr, ci = c_lanes(c_re), c_lanes(c_im)
    ec = jnp.stack([cr * per - ci * pei, -(cr * pei + ci * per)], axis=1)
    ec = jnp.concatenate([ec, ec], axis=-2)

    kr, ki = pw_re[:, :T, :, None, :], pw_im[:, :T, :, None, :]
    car = c_re[:, None] * kr - c_im[:, None] * ki
    cai = c_re[:, None] * ki + c_im[:, None] * kr
    kd = jnp.sum(car[:, :, :, None] * bb_re[:, None, :, :, None]
                 - cai[:, :, :, None] * bb_im[:, None, :, :, None], axis=-1)
    kf, kb = kd[0], kd[1]
    klag = jnp.concatenate([kb[:0:-1], (kf[0] + kb[0])[None], kf[1:]], axis=0)
    kt = jnp.tile(klag.reshape(2 * T - 1, S5_OCT, LANES, H), (1, 1, 1, S5_GPO))

    at = jnp.stack([pw_re[:, T], pw_im[:, T]], axis=1)
    at = at.reshape(2, 2, S5_OCT, S5_GPO * P).transpose(0, 2, 1, 3)
    return kt, fc, ec, at


def _s5_kernel(*refs, n_seq, n_chunk):
    T = S5_T
    (x_ref, kt_ref, fc_ref, ec_ref, at_ref, h0_ref, d_ref, out_ref, fin_ref,
     sf_ref, sb_ref, wt_ref, wf_ref, we_ref) = refs
    n_tile = S5_SW // LANES
    nh = n_tile // 2

    @pl.when(pl.program_id(1) == 0)
    def _():
        _s5_expand(kt_ref, fc_ref, ec_ref, wt_ref, wf_ref, we_ref)

    x = x_ref[...]
    xb = x.astype(BF16)

    def to_tiles(ref, val):
        for j in range(n_tile):
            ref[j] = val[:, j * LANES:(j + 1) * LANES]

    def from_tiles(ref):
        return jnp.concatenate([ref[j].astype(BF16) for j in range(n_tile)], axis=1)

    to_tiles(sf_ref, _dot(xb, wf_ref[0]))
    to_tiles(sb_ref, _dot(xb, wf_ref[1]))

    def coeff(d):
        return [(at_ref[d, 0, 0:1, j * LANES:(j + 1) * LANES],
                 at_ref[d, 0, 1:2, j * LANES:(j + 1) * LANES]) for j in range(nh)]

    def advance(ref, rows, a, state):
        out = []
        for j in range(nh):
            (ar, ai), (hr, hi) = a[j], state[j]
            sr, si = ref[j, rows, :], ref[nh + j, rows, :]
            ref[j, rows, :] = hr
            ref[nh + j, rows, :] = hi
            out.append((ar * hr - ai * hi + sr, ar * hi + ai * hr + si))
        return out

    af, ab = coeff(0), coeff(1)

    def step(i, carry):
        hf, hb = carry
        hf = advance(sf_ref, pl.ds(i, n_seq, stride=n_chunk), af, hf)
        hb = advance(sb_ref, pl.ds(n_chunk - 1 - i, n_seq, stride=n_chunk), ab, hb)
        return hf, hb

    def initial(d):
        return [(h0_ref[d, 0, 0, :, j * LANES:(j + 1) * LANES],
                 h0_ref[d, 0, 0, :, (nh + j) * LANES:(nh + j + 1) * LANES]) for j in range(nh)]

    finals = lax.fori_loop(0, n_chunk, step, (initial(0), initial(1)))
    for d in range(2):
        for j in range(nh):
            fin_ref[d, 0, 0, :, j * LANES:(j + 1) * LANES] = finals[d][j][0]
            fin_ref[d, 0, 0, :, (nh + j) * LANES:(nh + j + 1) * LANES] = finals[d][j][1]

    y = (_dot(xb, wt_ref[...]) + _dot(from_tiles(sf_ref), we_ref[0])
         + _dot(from_tiles(sb_ref), we_ref[1]))
    d = jnp.concatenate([d_ref[0]] * T, axis=1)
    out_ref[...] = jax.nn.gelu(y + d * x)


def _s5_expand(kt_ref, fc_ref, ec_ref, wt_ref, wf_ref, we_ref):
    T = S5_T
    nh = S5_SW // LANES // 2
    row = lax.broadcasted_iota(jnp.int32, (LANES, LANES), 0)
    lane = lax.broadcasted_iota(jnp.int32, (LANES, LANES), 1)
    own_k = (row // S5_CH) == (lane // S5_CH)
    lag_tiles = [jnp.where(own_k, kt_ref[l, 0], 0.0).astype(BF16) for l in range(2 * T - 1)]
    for s in range(T):
        for t in range(T):
            wt_ref[s * LANES:(s + 1) * LANES, t * LANES:(t + 1) * LANES] = lag_tiles[t - s + T - 1]
    gpt = LANES // S5_STATE
    own_f = [(row // S5_CH) == (gpt * j + lane // S5_STATE) for j in range(nh)]
    own_e = [(lane // S5_CH) == (gpt * j + row // S5_STATE) for j in range(nh)]
    for d in range(2):
        for ri in range(2):
            for s in range(T):
                f, e = fc_ref[d, ri, s, 0], ec_ref[d, ri, s, 0]
                rows = slice(s * LANES, (s + 1) * LANES)
                for j in range(nh):
                    cols = slice((ri * nh + j) * LANES, (ri * nh + j + 1) * LANES)
                    wf_ref[d, rows, cols] = jnp.where(own_f[j], f, 0.0).astype(BF16)
                    we_ref[d, cols, rows] = jnp.where(own_e[j], e, 0.0).astype(BF16)


def _s5_mixer(u, weights, h0, d_skip, n_seq, n_split):
    T = S5_T
    M_all = u.shape[0]
    M = M_all // n_split
    B = n_seq // n_split
    n_chunk = M // B
    kt, fc, ec, at = weights
    TL = T * LANES
    x_spec = pl.BlockSpec((M, TL), lambda o, j: (j, o))
    tile_spec = pl.BlockSpec((2, 2, T, 1, LANES, LANES), lambda o, j: (0, 0, 0, o, 0, 0))
    st_spec = pl.BlockSpec((2, 1, 1, B, S5_SW), lambda o, j: (0, o, j, 0, 0))
    g, fin = pl.pallas_call(
        functools.partial(_s5_kernel, n_seq=B, n_chunk=n_chunk),
        grid=(S5_OCT, n_split),
        in_specs=[x_spec,
                  pl.BlockSpec((2 * T - 1, 1, LANES, LANES), lambda o, j: (0, o, 0, 0)),
                  tile_spec, tile_spec,
                  pl.BlockSpec((2, 1, 2, S5_SW // 2), lambda o, j: (0, o, 0, 0)),
                  st_spec,
                  pl.BlockSpec((1, 1, LANES), lambda o, j: (o, 0, 0))],
        out_specs=[x_spec, st_spec],
        out_shape=[jax.ShapeDtypeStruct((M_all, S5_OCT * TL), F32),
                   jax.ShapeDtypeStruct((2, S5_OCT, n_split, B, S5_SW), F32)],
        scratch_shapes=[pltpu.VMEM((S5_SW // LANES, M, LANES), F32)] * 2 + [
            pltpu.VMEM((TL, TL), BF16), pltpu.VMEM((2, TL, S5_SW), BF16),
            pltpu.VMEM((2, S5_SW, TL), BF16)],
        compiler_params=_params(2),
        name="s5_mixer",
    )(u, kt, fc, ec, at, h0, d_skip.reshape(S5_OCT, 1, LANES))
    return g, fin


def _mlp(h, w1_ref, w2_ref):
    hb = h.astype(BF16)
    acc = None
    for j in range(FF_HIDDEN // FF_CHUNK):
        a = _dot(hb, w1_ref[:, j * FF_CHUNK:(j + 1) * FF_CHUNK])
        a = jnp.square(jnp.maximum(a, 0.0)).astype(BF16)
        p = _dot(a, w2_ref[j * FF_CHUNK:(j + 1) * FF_CHUNK, :])
        acc = p if acc is None else acc + p
    return acc


def _post0_kernel(x_ref, attn_ref, g_ref, mod_ref, ng_ref, wglu_ref, wout_ref, w1_ref, w2_ref, o_ref,
                  gbuf_ref):
    tm = x_ref.shape[1]
    for o in range(S5_OCT):
        for s in range(S5_T):
            c0 = (o * S5_T + s) * LANES
            gbuf_ref[o, pl.ds(s, tm // S5_T, stride=S5_T), :] = g_ref[:, c0:c0 + LANES]
    g = jnp.concatenate([gbuf_ref[o] for o in range(S5_OCT)], axis=1)
    s5 = g * jax.nn.sigmoid(_dot(g.astype(BF16), wglu_ref[...]))
    oc = (_dot(attn_ref[0], wout_ref[0:DA_WIDTH, :])
          + _dot(s5.astype(BF16), wout_ref[DA_WIDTH:D_MODEL, :]))
    x1 = x_ref[0] + mod_ref[0, 2:3, :] * oc
    h = _rms(x1, ng_ref[...]) * (1.0 + mod_ref[0, 4:5, :]) + mod_ref[0, 3:4, :]
    o_ref[0] = x1 + mod_ref[0, 5:6, :] * _mlp(h, w1_ref, w2_ref)


def _post0(x, attn, g, mods, mod_row, norm_g, w_glu, w_out, ff, layer, tm):
    B, L, _ = x.shape
    w1, w2 = ff
    row_spec = lambda w: pl.BlockSpec((1, tm, w), lambda b, i: (b, i, 0))
    return pl.pallas_call(
        _post0_kernel,
        grid=(B, L // tm),
        in_specs=[row_spec(D_MODEL), row_spec(DA_WIDTH), _chunk_spec(L, tm),
                  pl.BlockSpec((1, N_MOD, D_MODEL), lambda b, i: (mod_row(b), 0, 0)),
                  _const_spec((1, D_MODEL)),
                  _const_spec((S5_WIDTH, S5_WIDTH)),
                  _const_spec((D_MODEL, D_MODEL)),
                  _layer_spec((D_MODEL, FF_HIDDEN), layer),
                  _layer_spec((FF_HIDDEN, D_MODEL), layer)],
        out_specs=row_spec(D_MODEL),
        out_shape=jax.ShapeDtypeStruct((B, L, D_MODEL), F32),
        scratch_shapes=[pltpu.VMEM((S5_OCT, tm, LANES), F32)],
        compiler_params=_params(2),
        name="mix_out_mlp",
    )(x, attn, g, mods, norm_g.reshape(1, D_MODEL), w_glu, w_out, w1, w2)


def _layer1_kernel(x_ref, xp_ref, xn_ref, mod_ref, ng1_ref, ng2_ref, pw_ref, ps_ref,
                   w1_ref, w2_ref, fg_ref, o_ref, *, tm, seq_len):
    i = pl.program_id(1)
    H = POOL_HALO
    n = tm + 2 * H
    x = x_ref[0]
    xe = jnp.concatenate([xp_ref[0], x, xn_ref[0]], axis=0)
    he = _rms(xe, ng1_ref[...]) * (1.0 + mod_ref[0, 1:2, :]) + mod_ref[0, 0:1, :]
    pos_e = i * tm - H + lax.broadcasted_iota(jnp.int32, (n, 1), 0)
    he = jnp.where((pos_e >= 0) & (pos_e < seq_len), he, 0.0)
    pos = i * tm + lax.broadcasted_iota(jnp.int32, (tm, 1), 0)
    zs = []
    for gi, w in enumerate(POOL_WINDOWS):
        hg = he[:, gi * POOL_CH:(gi + 1) * POOL_CH]
        acc = hg + pltpu.roll(hg, 1, axis=0)
        r = 1
        while 2 * r < w:
            acc = pltpu.roll(acc, r, axis=0) + pltpu.roll(acc, n - r, axis=0)
            r *= 2
        lo = jnp.clip(pos - w // 2, 0, seq_len)
        hi = jnp.clip(pos + w // 2, 0, seq_len)
        mean = acc[H:H + tm] / (hi - lo).astype(F32)
        z = mean - hg[H:H + tm]
        zs.append(_dot(z.astype(BF16), pw_ref[gi]))
    oc = jnp.concatenate(zs, axis=1) * ps_ref[...]
    x1 = x + mod_ref[0, 2:3, :] * oc
    h = _rms(x1, ng2_ref[...]) * (1.0 + mod_ref[0, 4:5, :]) + mod_ref[0, 3:4, :]
    x2 = x1 + mod_ref[0, 5:6, :] * _mlp(h, w1_ref, w2_ref)
    o_ref[0] = _rms(x2, fg_ref[...])


def _layer1(x, mods, mod_row, norm_g, pool_w, pool_scale, ff, layer, final_g, tm):
    B, L, _ = x.shape
    w1, w2 = ff
    H = POOL_HALO
    per = tm // H
    last = L // H - 1
    row_spec = pl.BlockSpec((1, tm, D_MODEL), lambda b, i: (b, i, 0))
    return pl.pallas_call(
        functools.partial(_layer1_kernel, tm=tm, seq_len=L),
        grid=(B, L // tm),
        in_specs=[row_spec,
                  pl.BlockSpec((1, H, D_MODEL), lambda b, i: (b, jnp.maximum(i * per - 1, 0), 0)),
                  pl.BlockSpec((1, H, D_MODEL), lambda b, i: (b, jnp.minimum((i + 1) * per, last), 0)),
                  pl.BlockSpec((1, N_MOD, D_MODEL), lambda b, i: (mod_row(b), 0, 0)),
                  _const_spec((1, D_MODEL)), _const_spec((1, D_MODEL)),
                  _const_spec((len(POOL_WINDOWS), POOL_CH, POOL_CH)),
                  _const_spec((1, D_MODEL)),
                  _layer_spec((D_MODEL, FF_HIDDEN), layer),
                  _layer_spec((FF_HIDDEN, D_MODEL), layer),
                  _const_spec((1, D_MODEL))],
        out_specs=row_spec,
        out_shape=jax.ShapeDtypeStruct((B, L, D_MODEL), F32),
        compiler_params=_params(2),
        name="pool_mlp_norm",
    )(x, x, x, mods, norm_g[0].reshape(1, D_MODEL), norm_g[1].reshape(1, D_MODEL),
      pool_w, pool_scale.reshape(1, D_MODEL), w1, w2, final_g.reshape(1, D_MODEL))


def kernel(x_prompt, x_sample, cache_k, cache_v, state_s5, c, c_ctx, mod_w, mod_b, norm_g, mix_w_in, mix_w_out, diff_lambda_qk, diff_subln_g, s5_lambda_re, s5_lambda_im, s5_log_dt, s5_b_re, s5_b_im, s5_c_re, s5_c_im, s5_d, s5_w_glu, pool_w, pool_scale, ff_w1, ff_w2, final_norm_g):
    B_ctx, L_ctx, _ = x_prompt.shape
    B_lat, L_lat, _ = x_sample.shape
    ctx_row = B_lat
    n_rows = 8
    cc = jnp.concatenate([c, c_ctx[None], jnp.zeros((n_rows - B_lat - 1, D_MODEL), F32)], axis=0)
    mods = _modulation(cc, mod_w, mod_b).reshape(DEPTH, n_rows, N_MOD, D_MODEL)
    lat_row = lambda b: b
    ctx_mod_row = lambda b: ctx_row

    j = 0
    lam_init = 0.8 - 0.6 * math.exp(-0.3 * 0)
    w_in = mix_w_in[j].astype(BF16)
    w_out = mix_w_out[j].astype(BF16)
    w_glu = s5_w_glu[j].astype(BF16)
    ff = (ff_w1.astype(BF16), ff_w2.astype(BF16))
    s5w = _s5_weights(s5_lambda_re[j], s5_lambda_im[j], s5_log_dt[j], s5_b_re[j], s5_b_im[j],
                      s5_c_re[j], s5_c_im[j])

    qc, kc, vc, uc, kc32, vc32 = _premix(x_prompt, mods[0], ctx_mod_row, norm_g[0, 0], w_in,
                                         None, 256, True)
    ql, kl, vl, ul = _premix(x_sample, mods[0], lat_row, norm_g[0, 0], w_in,
                             _rope_tables(L_lat), 512, False)

    ac = _attention(qc, kc, vc, None, diff_lambda_qk[j], diff_subln_g[j], lam_init, 256)
    P = cache_k.shape[2]
    cache = (cache_k[:, j].reshape(B_lat, P, DA_WIDTH), cache_v[:, j].reshape(B_lat, P, DA_WIDTH))
    al = _attention(ql, kl, vl, cache, diff_lambda_qk[j], diff_subln_g[j], lam_init, 256)

    lat_split = 2
    h0_ctx = jnp.zeros((2, S5_OCT, 1, B_ctx, S5_SW), F32)
    h0_lat = state_s5[:, j].reshape(B_lat, 2, S5_OCT, S5_GPO, S5_STATE, 2)
    h0_lat = h0_lat.transpose(1, 2, 0, 5, 3, 4).reshape(
        2, S5_OCT, lat_split, B_lat // lat_split, S5_SW)
    gc, fin_c = _s5_mixer(uc, s5w, h0_ctx, s5_d[j], B_ctx, 1)
    gl, _ = _s5_mixer(ul, s5w, h0_lat, s5_d[j], B_lat, lat_split)

    ctx = _post0(x_prompt, ac, gc, mods[0], ctx_mod_row, norm_g[0, 1], w_glu, w_out, ff, 0, 256)
    lat = _post0(x_sample, al, gl, mods[0], lat_row, norm_g[0, 1], w_glu, w_out, ff, 0, 512)

    pw = pool_w[0].astype(BF16)
    y_prompt = _layer1(ctx, mods[1], ctx_mod_row, norm_g[1], pw, pool_scale[0], ff, 1,
                       final_norm_g, 256)
    y_sample = _layer1(lat, mods[1], lat_row, norm_g[1], pw, pool_scale[0], ff, 1,
                       final_norm_g, 512)

    new_cache_k = kc32.reshape(B_ctx, 1, L_ctx, DA_HEADS, 2 * DA_QK)
    new_cache_v = vc32.reshape(B_ctx, 1, L_ctx, DA_HEADS, DA_V)
    st = fin_c.reshape(2, S5_OCT, B_ctx, 2, S5_GPO, S5_STATE)
    new_state = st.transpose(2, 0, 1, 4, 5, 3).reshape(B_ctx, 1, 2, S5_GROUPS, S5_STATE, 2)
    return (y_prompt, y_sample, new_cache_k, new_cache_v, new_state)
```

```python
import functools
import math

import jax
import jax.numpy as jnp
from jax import lax
from jax.experimental import pallas as pl
from jax.experimental.pallas import tpu as pltpu

D_MODEL = 1024
DEPTH = 2
GRID_W = 64
DA_HEADS = 4
DA_QK = 64
DA_V = 2 * DA_QK
DA_WIDTH = DA_HEADS * DA_V
S5_WIDTH = D_MODEL - DA_WIDTH
S5_CH = 16
S5_GROUPS = S5_WIDTH // S5_CH
S5_STATE = 64
IN_WIDTH = 3 * DA_WIDTH + S5_WIDTH
POOL_WINDOWS = (2, 4, 8, 16)
POOL_CH = D_MODEL // len(POOL_WINDOWS)
FF_HIDDEN = 4 * D_MODEL
N_MOD = 6
ROPE_BASE = 10000.0
EPS = 1e-6

LANES = 128
S5_T = 8
S5_OCT = S5_WIDTH // LANES
S5_GPO = LANES // S5_CH
S5_SW = 2 * S5_GPO * S5_STATE
Q_SCALE = DA_QK ** -0.5 * math.log2(math.e)
ATTN_SUB = 256
POOL_HALO = 16
FF_CHUNK = 1024
VMEM_LIMIT = 56 * 1024 * 1024

F32 = jnp.float32
BF16 = jnp.bfloat16


def _const_spec(shape):
    nd = len(shape)
    return pl.BlockSpec(shape, lambda *_: (0,) * nd, pipeline_mode=pl.Buffered(1))


def _layer_spec(shape, layer):
    return pl.BlockSpec((None,) + shape, lambda *_: (layer, 0, 0), pipeline_mode=pl.Buffered(1))


def _params(n_axes):
    return pltpu.CompilerParams(dimension_semantics=("arbitrary",) * n_axes,
                                vmem_limit_bytes=VMEM_LIMIT)


def _rms(x, g):
    return x * lax.rsqrt(jnp.mean(x * x, axis=-1, keepdims=True) + EPS) * g


def _dot(a, b):
    return jnp.dot(a, b, preferred_element_type=F32)


def _mod_kernel(c_ref, w_ref, b_ref, o_ref):
    c = c_ref[...]
    s = (c * jax.nn.sigmoid(c)).astype(BF16)
    o_ref[0] = _dot(s, w_ref[0].astype(BF16)) + b_ref[0]


def _modulation(cc, mod_w, mod_b):
    rows = cc.shape[0]
    tn = 1536
    n = N_MOD * D_MODEL
    return pl.pallas_call(
        _mod_kernel,
        grid=(DEPTH, n // tn),
        in_specs=[pl.BlockSpec((rows, D_MODEL), lambda l, j: (0, 0)),
                  pl.BlockSpec((1, D_MODEL, tn), lambda l, j: (l, 0, j)),
                  pl.BlockSpec((1, 1, tn), lambda l, j: (l, 0, j))],
        out_specs=pl.BlockSpec((1, rows, tn), lambda l, j: (l, 0, j)),
        out_shape=jax.ShapeDtypeStruct((DEPTH, rows, n), F32),
        compiler_params=_params(2),
        name="modulation",
    )(cc, mod_w, mod_b.reshape(DEPTH, 1, n))


def _premix_kernel(*refs, rope, cache_out):
    x_ref, mod_ref, g_ref, w_ref = refs[:4]
    refs = refs[4:]
    if rope:
        cos_ref, sa_ref, sb_ref = refs[:3]
        refs = refs[3:]
    q_ref, k_ref, v_ref, u_ref = refs[:4]
    x = x_ref[0]
    h = _rms(x, g_ref[...]) * (1.0 + mod_ref[0, 1:2, :]) + mod_ref[0, 0:1, :]
    z = _dot(h.astype(BF16), w_ref[...])
    n_slab = 2 * DA_HEADS

    def rot(s):
        if not rope:
            return s
        return (s * cos_ref[...] + pltpu.roll(s, LANES - 16, axis=1) * sa_ref[...]
                + pltpu.roll(s, 16, axis=1) * sb_ref[...])

    for j in range(n_slab):
        s = rot(z[:, j * LANES:(j + 1) * LANES])
        if j < DA_HEADS:
            q_ref[0, :, j * LANES:(j + 1) * LANES] = (s * Q_SCALE).astype(BF16)
        else:
            jj = j - DA_HEADS
            k_ref[0, :, jj * LANES:(jj + 1) * LANES] = s.astype(BF16)
    v = z[:, 2 * DA_WIDTH:3 * DA_WIDTH]
    v_ref[0] = v.astype(BF16)
    ubuf_ref = refs[-1]
    n_chunk = z.shape[0] // S5_T
    for o in range(S5_OCT):
        ubuf_ref[o] = z[:, 3 * DA_WIDTH + o * LANES:3 * DA_WIDTH + (o + 1) * LANES]
    for o in range(S5_OCT):
        for s in range(S5_T):
            c0 = (o * S5_T + s) * LANES
            u_ref[:, c0:c0 + LANES] = ubuf_ref[o, pl.ds(s, n_chunk, stride=S5_T), :]
    if cache_out:
        kf_ref, vf_ref = refs[4:6]
        kf_ref[0] = z[:, DA_WIDTH:2 * DA_WIDTH]
        vf_ref[0] = v


def _chunk_spec(L, tm):
    return pl.BlockSpec((tm // S5_T, S5_WIDTH * S5_T), lambda b, i: (b * (L // tm) + i, 0))


def _premix(x, mods, mod_row, norm_g, w_in, rope_tabs, tm, cache_out):
    B, L, _ = x.shape
    rope = rope_tabs is not None
    row_spec = lambda w: pl.BlockSpec((1, tm, w), lambda b, i: (b, i, 0))
    in_specs = [row_spec(D_MODEL),
                pl.BlockSpec((1, N_MOD, D_MODEL), lambda b, i: (mod_row(b), 0, 0)),
                _const_spec((1, D_MODEL)),
                _const_spec((D_MODEL, IN_WIDTH))]
    args = [x, mods, norm_g.reshape(1, D_MODEL), w_in]
    if rope:
        in_specs += [pl.BlockSpec((tm, LANES), lambda b, i: (i, 0))] * 3
        args += list(rope_tabs)
    out_specs = [row_spec(DA_WIDTH)] * 3 + [_chunk_spec(L, tm)]
    out_shape = [jax.ShapeDtypeStruct((B, L, DA_WIDTH), BF16)] * 3 + [
        jax.ShapeDtypeStruct((B * L // S5_T, S5_WIDTH * S5_T), F32)]
    if cache_out:
        out_specs += [row_spec(DA_WIDTH)] * 2
        out_shape += [jax.ShapeDtypeStruct((B, L, DA_WIDTH), F32)] * 2
    return pl.pallas_call(
        functools.partial(_premix_kernel, rope=rope, cache_out=cache_out),
        grid=(B, L // tm),
        in_specs=in_specs, out_specs=out_specs, out_shape=out_shape,
        scratch_shapes=[pltpu.VMEM((S5_OCT, tm, LANES), F32)],
        compiler_params=_params(2),
        name="premix_rope" if rope else "premix",
    )(*args)


def _rope_tables(L):
    rows = L // GRID_W
    row = jnp.repeat(jnp.arange(rows), GRID_W).astype(F32)
    col = jnp.tile(jnp.arange(GRID_W), rows).astype(F32)
    half = DA_QK // 2
    inv = 1.0 / (ROPE_BASE ** (jnp.arange(0, half, 2, dtype=F32) / half))
    ang_r, ang_c = row[:, None] * inv, col[:, None] * inv
    zero = jnp.zeros_like(ang_r)
    cos64 = jnp.concatenate([jnp.cos(ang_r)] * 2 + [jnp.cos(ang_c)] * 2, axis=1)
    sa64 = jnp.concatenate([-jnp.sin(ang_r), zero, -jnp.sin(ang_c), zero], axis=1)
    sb64 = jnp.concatenate([zero, jnp.sin(ang_r), zero, jnp.sin(ang_c)], axis=1)
    return tuple(jnp.tile(t, (1, LANES // DA_QK)) for t in (cos64, sa64, sb64))


def _attn_kernel(*refs, lam_init, has_cache):
    lqk_ref, g_ref, q_ref, k_ref, v_ref = refs[:5]
    if has_cache:
        ck_ref, cv_ref, o_ref = refs[5:8]
    else:
        o_ref = refs[5]
    lqk = lqk_ref[...]
    lam = (jnp.exp(jnp.sum(lqk[0:1] * lqk[1:2], axis=-1, keepdims=True))
           - jnp.exp(jnp.sum(lqk[2:3] * lqk[3:4], axis=-1, keepdims=True)) + lam_init)
    tq = q_ref.shape[1]
    lane = lax.broadcasted_iota(jnp.int32, (ATTN_SUB, LANES), 1)
    zero = jnp.zeros((ATTN_SUB, LANES), BF16)
    nt = (((1,), (1,)), ((), ()))
    for hh in range(q_ref.shape[2] // LANES):
        cols = slice(hh * LANES, (hh + 1) * LANES)
        segs = [(k_ref[0, :, cols], v_ref[0, :, cols])]
        if has_cache:
            segs.append((ck_ref[0, :, cols].astype(BF16), cv_ref[0, :, cols].astype(BF16)))
        for r in range(tq // ATTN_SUB):
            rows = slice(r * ATTN_SUB, (r + 1) * ATTN_SUB)
            q = q_ref[0, rows, cols]
            outs = []
            for qm in (jnp.where(lane < DA_QK, q, zero), jnp.where(lane >= DA_QK, q, zero)):
                ss = [lax.dot_general(qm, k, nt, preferred_element_type=F32) for k, _ in segs]
                m = functools.reduce(jnp.maximum,
                                     [jnp.max(s, axis=-1, keepdims=True) for s in ss])
                es = [jnp.exp2(s - m) for s in ss]
                l = functools.reduce(jnp.add, [jnp.sum(e, axis=-1, keepdims=True) for e in es])
                o = functools.reduce(jnp.add,
                                     [_dot(e.astype(BF16), v) for e, (_, v) in zip(es, segs)])
                outs.append(o / l)
            o = outs[0] - lam * outs[1]
            o_ref[0, rows, cols] = (_rms(o, g_ref[...]) * (1.0 - lam_init)).astype(BF16)


def _attention(q, k, v, cache, lambda_qk, subln_g, lam_init, tq, heads):
    B, L, _ = q.shape
    has_cache = cache is not None
    q_spec = pl.BlockSpec((1, tq, heads * LANES), lambda b, h, i: (b, i, h))
    kv_spec = lambda n: pl.BlockSpec((1, n, heads * LANES), lambda b, h, i: (b, 0, h))
    in_specs = [_const_spec((4, DA_QK)), _const_spec((1, DA_V)), q_spec, kv_spec(L), kv_spec(L)]
    args = [lambda_qk, subln_g.reshape(1, DA_V), q, k, v]
    if has_cache:
        P = cache[0].shape[1]
        in_specs += [kv_spec(P), kv_spec(P)]
        args += list(cache)
    return pl.pallas_call(
        functools.partial(_attn_kernel, lam_init=lam_init, has_cache=has_cache),
        grid=(B, DA_HEADS // heads, L // tq),
        in_specs=in_specs, out_specs=q_spec,
        out_shape=jax.ShapeDtypeStruct((B, L, DA_WIDTH), BF16),
        compiler_params=_params(3),
        name="diff_attn_cache" if has_cache else "diff_attn",
    )(*args)


def _s5_weights(lam_re, lam_im, log_dt, b_re, b_im, c_re, c_im):
    T, G, P, H = S5_T, S5_GROUPS, S5_STATE, S5_CH
    dt = jnp.exp(log_dt)[..., None]
    lr, li = lam_re * dt, lam_im * dt
    kk = jnp.arange(T + 1, dtype=F32)[None, :, None, None]
    mag = jnp.exp(kk * lr[:, None])
    pw_re = mag * jnp.cos(kk * li[:, None])
    pw_im = mag * jnp.sin(kk * li[:, None])
    a_re, a_im = pw_re[:, 1], pw_im[:, 1]
    den = lam_re * lam_re + lam_im * lam_im
    f_re = ((a_re - 1.0) * lam_re + a_im * lam_im) / den
    f_im = (a_im * lam_re - (a_re - 1.0) * lam_im) / den
    bt_re, bt_im = jnp.swapaxes(b_re, -1, -2), jnp.swapaxes(b_im, -1, -2)
    bb_re = f_re[:, :, None, :] * bt_re - f_im[:, :, None, :] * bt_im
    bb_im = f_re[:, :, None, :] * bt_im + f_im[:, :, None, :] * bt_re

    def per_dir(x, fwd, bwd):
        return jnp.stack([fwd(x[0]), bwd(x[1])], axis=0)

    flip = lambda x: x[::-1]
    pr = per_dir(pw_re, lambda x: flip(x[:T]), lambda x: x[:T])[:, :, :, None, :]
    pi = per_dir(pw_im, lambda x: flip(x[:T]), lambda x: x[:T])[:, :, :, None, :]
    br, bi = bb_re[:, None], bb_im[:, None]
    fc = jnp.stack([pr * br - pi * bi, pr * bi + pi * br], axis=1)
    fc = fc.reshape(2, 2, T, S5_OCT, LANES, P)
    fc = jnp.concatenate([fc, fc], axis=-1)

    def lanes_gh(x):
        x = x.reshape(x.shape[:-2] + (S5_OCT, S5_GPO, P))
        return jnp.repeat(jnp.swapaxes(x, -1, -2), H, axis=-1)

    def c_lanes(c):
        c = c.reshape(2, S5_OCT, S5_GPO, H, P).transpose(0, 1, 4, 2, 3)
        return c.reshape(2, 1, S5_OCT, P, LANES)

    per = lanes_gh(per_dir(pw_re, lambda x: x[1:], lambda x: flip(x[1:])))
    pei = lanes_gh(per_dir(pw_im, lambda x: x[1:], lambda x: flip(x[1:])))
    cr, ci = c_lanes(c_re), c_lanes(c_im)
    ec = jnp.stack([cr * per - ci * pei, -(cr * pei + ci * per)], axis=1)
    ec = jnp.concatenate([ec, ec], axis=-2)

    kr, ki = pw_re[:, :T, :, None, :], pw_im[:, :T, :, None, :]
    car = c_re[:, None] * kr - c_im[:, None] * ki
    cai = c_re[:, None] * ki + c_im[:, None] * kr
    kd = jnp.sum(car[:, :, :, None] * bb_re[:, None, :, :, None]
                 - cai[:, :, :, None] * bb_im[:, None, :, :, None], axis=-1)
    kf, kb = kd[0], kd[1]
    klag = jnp.concatenate([kb[:0:-1], (kf[0] + kb[0])[None], kf[1:]], axis=0)
    kt = jnp.tile(klag.reshape(2 * T - 1, S5_OCT, LANES, H), (1, 1, 1, S5_GPO))

    at = jnp.stack([pw_re[:, T], pw_im[:, T]], axis=1)
    at = at.reshape(2, 2, S5_OCT, S5_GPO * P).transpose(0, 2, 1, 3)
    return kt, fc, ec, at


def _s5_kernel(*refs, n_seq, n_chunk):
    T = S5_T
    (x_ref, kt_ref, fc_ref, ec_ref, at_ref, h0_ref, d_ref, out_ref, fin_ref,
     sf_ref, sb_ref, wt_ref, wf_ref, we_ref) = refs
    n_tile = S5_SW // LANES
    nh = n_tile // 2

    @pl.when(pl.program_id(1) == 0)
    def _():
        _s5_expand(kt_ref, fc_ref, ec_ref, wt_ref, wf_ref, we_ref)

    x = x_ref[...]
    xb = x.astype(BF16)

    def to_tiles(ref, val):
        for j in range(n_tile):
            ref[j] = val[:, j * LANES:(j + 1) * LANES]

    def from_tiles(ref):
        return jnp.concatenate([ref[j].astype(BF16) for j in range(n_tile)], axis=1)

    to_tiles(sf_ref, _dot(xb, wf_ref[0]))
    to_tiles(sb_ref, _dot(xb, wf_ref[1]))

    def coeff(d):
        return [(at_ref[d, 0, 0:1, j * LANES:(j + 1) * LANES],
                 at_ref[d, 0, 1:2, j * LANES:(j + 1) * LANES]) for j in range(nh)]

    def advance(ref, rows, a, state):
        out = []
        for j in range(nh):
            (ar, ai), (hr, hi) = a[j], state[j]
            sr, si = ref[j, rows, :], ref[nh + j, rows, :]
            ref[j, rows, :] = hr
            ref[nh + j, rows, :] = hi
            out.append((ar * hr - ai * hi + sr, ar * hi + ai * hr + si))
        return out

    af, ab = coeff(0), coeff(1)

    def step(i, carry):
        hf, hb = carry
        hf = advance(sf_ref, pl.ds(i, n_seq, stride=n_chunk), af, hf)
        hb = advance(sb_ref, pl.ds(n_chunk - 1 - i, n_seq, stride=n_chunk), ab, hb)
        return hf, hb

    def initial(d):
        return [(h0_ref[d, 0, 0, :, j * LANES:(j + 1) * LANES],
                 h0_ref[d, 0, 0, :, (nh + j) * LANES:(nh + j + 1) * LANES]) for j in range(nh)]

    finals = lax.fori_loop(0, n_chunk, step, (initial(0), initial(1)))
    for d in range(2):
        for j in range(nh):
            fin_ref[d, 0, 0, :, j * LANES:(j + 1) * LANES] = finals[d][j][0]
            fin_ref[d, 0, 0, :, (nh + j) * LANES:(nh + j + 1) * LANES] = finals[d][j][1]

    y = (_dot(xb, wt_ref[...]) + _dot(from_tiles(sf_ref), we_ref[0])
         + _dot(from_tiles(sb_ref), we_ref[1]))
    d = jnp.concatenate([d_ref[0]] * T, axis=1)
    out_ref[...] = jax.nn.gelu(y + d * x)


def _s5_expand(kt_ref, fc_ref, ec_ref, wt_ref, wf_ref, we_ref):
    T = S5_T
    nh = S5_SW // LANES // 2
    row = lax.broadcasted_iota(jnp.int32, (LANES, LANES), 0)
    lane = lax.broadcasted_iota(jnp.int32, (LANES, LANES), 1)
    own_k = (row // S5_CH) == (lane // S5_CH)
    lag_tiles = [jnp.where(own_k, kt_ref[l, 0], 0.0).astype(BF16) for l in range(2 * T - 1)]
    for s in range(T):
        for t in range(T):
            wt_ref[s * LANES:(s + 1) * LANES, t * LANES:(t + 1) * LANES] = lag_tiles[t - s + T - 1]
    gpt = LANES // S5_STATE
    own_f = [(row // S5_CH) == (gpt * j + lane // S5_STATE) for j in range(nh)]
    own_e = [(lane // S5_CH) == (gpt * j + row // S5_STATE) for j in range(nh)]
    for d in range(2):
        for ri in range(2):
            for s in range(T):
                f, e = fc_ref[d, ri, s, 0], ec_ref[d, ri, s, 0]
                rows = slice(s * LANES, (s + 1) * LANES)
                for j in range(nh):
                    cols = slice((ri * nh + j) * LANES, (ri * nh + j + 1) * LANES)
                    wf_ref[d, rows, cols] = jnp.where(own_f[j], f, 0.0).astype(BF16)
                    we_ref[d, cols, rows] = jnp.where(own_e[j], e, 0.0).astype(BF16)


def _s5_mixer(u, weights, h0, d_skip, n_seq, n_split):
    T = S5_T
    M_all = u.shape[0]
    M = M_all // n_split
    B = n_seq // n_split
    n_chunk = M // B
    kt, fc, ec, at = weights
    TL = T * LANES
    x_spec = pl.BlockSpec((M, TL), lambda o, j: (j, o))
    tile_spec = pl.BlockSpec((2, 2, T, 1, LANES, LANES), lambda o, j: (0, 0, 0, o, 0, 0))
    st_spec = pl.BlockSpec((2, 1, 1, B, S5_SW), lambda o, j: (0, o, j, 0, 0))
    g, fin = pl.pallas_call(
        functools.partial(_s5_kernel, n_seq=B, n_chunk=n_chunk),
        grid=(S5_OCT, n_split),
        in_specs=[x_spec,
                  pl.BlockSpec((2 * T - 1, 1, LANES, LANES), lambda o, j: (0, o, 0, 0)),
                  tile_spec, tile_spec,
                  pl.BlockSpec((2, 1, 2, S5_SW // 2), lambda o, j: (0, o, 0, 0)),
                  st_spec,
                  pl.BlockSpec((1, 1, LANES), lambda o, j: (o, 0, 0))],
        out_specs=[x_spec, st_spec],
        out_shape=[jax.ShapeDtypeStruct((M_all, S5_OCT * TL), F32),
                   jax.ShapeDtypeStruct((2, S5_OCT, n_split, B, S5_SW), F32)],
        scratch_shapes=[pltpu.VMEM((S5_SW // LANES, M, LANES), F32)] * 2 + [
            pltpu.VMEM((TL, TL), BF16), pltpu.VMEM((2, TL, S5_SW), BF16),
            pltpu.VMEM((2, S5_SW, TL), BF16)],
        compiler_params=_params(2),
        name="s5_mixer",
    )(u, kt, fc, ec, at, h0, d_skip.reshape(S5_OCT, 1, LANES))
    return g, fin


def _mlp(h, w1_ref, w2_ref):
    hb = h.astype(BF16)
    acc = None
    for j in range(FF_HIDDEN // FF_CHUNK):
        a = _dot(hb, w1_ref[:, j * FF_CHUNK:(j + 1) * FF_CHUNK])
        a = jnp.square(jnp.maximum(a, 0.0)).astype(BF16)
        p = _dot(a, w2_ref[j * FF_CHUNK:(j + 1) * FF_CHUNK, :])
        acc = p if acc is None else acc + p
    return acc


def _post0_kernel(x_ref, attn_ref, g_ref, mod_ref, ng_ref, wglu_ref, wout_ref, w1_ref, w2_ref, o_ref,
                  gbuf_ref):
    tm = x_ref.shape[1]
    for o in range(S5_OCT):
        for s in range(S5_T):
            c0 = (o * S5_T + s) * LANES
            gbuf_ref[o, pl.ds(s, tm // S5_T, stride=S5_T), :] = g_ref[:, c0:c0 + LANES]
    g = jnp.concatenate([gbuf_ref[o] for o in range(S5_OCT)], axis=1)
    s5 = g * jax.nn.sigmoid(_dot(g.astype(BF16), wglu_ref[...]))
    oc = (_dot(attn_ref[0], wout_ref[0:DA_WIDTH, :])
          + _dot(s5.astype(BF16), wout_ref[DA_WIDTH:D_MODEL, :]))
    x1 = x_ref[0] + mod_ref[0, 2:3, :] * oc
    h = _rms(x1, ng_ref[...]) * (1.0 + mod_ref[0, 4:5, :]) + mod_ref[0, 3:4, :]
    o_ref[0] = x1 + mod_ref[0, 5:6, :] * _mlp(h, w1_ref, w2_ref)


def _post0(x, attn, g, mods, mod_row, norm_g, w_glu, w_out, ff, layer, tm):
    B, L, _ = x.shape
    w1, w2 = ff
    row_spec = lambda w: pl.BlockSpec((1, tm, w), lambda b, i: (b, i, 0))
    return pl.pallas_call(
        _post0_kernel,
        grid=(B, L // tm),
        in_specs=[row_spec(D_MODEL), row_spec(DA_WIDTH), _chunk_spec(L, tm),
                  pl.BlockSpec((1, N_MOD, D_MODEL), lambda b, i: (mod_row(b), 0, 0)),
                  _const_spec((1, D_MODEL)),
                  _const_spec((S5_WIDTH, S5_WIDTH)),
                  _const_spec((D_MODEL, D_MODEL)),
                  _layer_spec((D_MODEL, FF_HIDDEN), layer),
                  _layer_spec((FF_HIDDEN, D_MODEL), layer)],
        out_specs=row_spec(D_MODEL),
        out_shape=jax.ShapeDtypeStruct((B, L, D_MODEL), F32),
        scratch_shapes=[pltpu.VMEM((S5_OCT, tm, LANES), F32)],
        compiler_params=_params(2),
        name="mix_out_mlp",
    )(x, attn, g, mods, norm_g.reshape(1, D_MODEL), w_glu, w_out, w1, w2)


def _layer1_kernel(x_ref, xp_ref, xn_ref, mod_ref, ng1_ref, ng2_ref, pw_ref, ps_ref,
                   w1_ref, w2_ref, fg_ref, o_ref, *, tm, seq_len):
    i = pl.program_id(1)
    H = POOL_HALO
    n = tm + 2 * H
    x = x_ref[0]
    xe = jnp.concatenate([xp_ref[0], x, xn_ref[0]], axis=0)
    he = _rms(xe, ng1_ref[...]) * (1.0 + mod_ref[0, 1:2, :]) + mod_ref[0, 0:1, :]
    pos_e = i * tm - H + lax.broadcasted_iota(jnp.int32, (n, 1), 0)
    he = jnp.where((pos_e >= 0) & (pos_e < seq_len), he, 0.0)
    pos = i * tm + lax.broadcasted_iota(jnp.int32, (tm, 1), 0)
    zs = []
    for gi, w in enumerate(POOL_WINDOWS):
        hg = he[:, gi * POOL_CH:(gi + 1) * POOL_CH]
        acc = hg + pltpu.roll(hg, 1, axis=0)
        r = 1
        while 2 * r < w:
            acc = pltpu.roll(acc, r, axis=0) + pltpu.roll(acc, n - r, axis=0)
            r *= 2
        lo = jnp.clip(pos - w // 2, 0, seq_len)
        hi = jnp.clip(pos + w // 2, 0, seq_len)
        mean = acc[H:H + tm] / (hi - lo).astype(F32)
        z = mean - hg[H:H + tm]
        zs.append(_dot(z.astype(BF16), pw_ref[gi]))
    oc = jnp.concatenate(zs, axis=1) * ps_ref[...]
    x1 = x + mod_ref[0, 2:3, :] * oc
    h = _rms(x1, ng2_ref[...]) * (1.0 + mod_ref[0, 4:5, :]) + mod_ref[0, 3:4, :]
    x2 = x1 + mod_ref[0, 5:6, :] * _mlp(h, w1_ref, w2_ref)
    o_ref[0] = _rms(x2, fg_ref[...])


def _layer1(x, mods, mod_row, norm_g, pool_w, pool_scale, ff, layer, final_g, tm):
    B, L, _ = x.shape
    w1, w2 = ff
    H = POOL_HALO
    per = tm // H
    last = L // H - 1
    row_spec = pl.BlockSpec((1, tm, D_MODEL), lambda b, i: (b, i, 0))
    return pl.pallas_call(
        functools.partial(_layer1_kernel, tm=tm, seq_len=L),
        grid=(B, L // tm),
        in_specs=[row_spec,
                  pl.BlockSpec((1, H, D_MODEL), lambda b, i: (b, jnp.maximum(i * per - 1, 0), 0)),
                  pl.BlockSpec((1, H, D_MODEL), lambda b, i: (b, jnp.minimum((i + 1) * per, last), 0)),
                  pl.BlockSpec((1, N_MOD, D_MODEL), lambda b, i: (mod_row(b), 0, 0)),
                  _const_spec((1, D_MODEL)), _const_spec((1, D_MODEL)),
                  _const_spec((len(POOL_WINDOWS), POOL_CH, POOL_CH)),
                  _const_spec((1, D_MODEL)),
                  _layer_spec((D_MODEL, FF_HIDDEN), layer),
                  _layer_spec((FF_HIDDEN, D_MODEL), layer),
                  _const_spec((1, D_MODEL))],
        out_specs=row_spec,
        out_shape=jax.ShapeDtypeStruct((B, L, D_MODEL), F32),
        compiler_params=_params(2),
        name="pool_mlp_norm",
    )(x, x, x, mods, norm_g[0].reshape(1, D_MODEL), norm_g[1].reshape(1, D_MODEL),
      pool_w, pool_scale.reshape(1, D_MODEL), w1, w2, final_g.reshape(1, D_MODEL))


def kernel(x_prompt, x_sample, cache_k, cache_v, state_s5, c, c_ctx, mod_w, mod_b, norm_g, mix_w_in, mix_w_out, diff_lambda_qk, diff_subln_g, s5_lambda_re, s5_lambda_im, s5_log_dt, s5_b_re, s5_b_im, s5_c_re, s5_c_im, s5_d, s5_w_glu, pool_w, pool_scale, ff_w1, ff_w2, final_norm_g):
    B_ctx, L_ctx, _ = x_prompt.shape
    B_lat, L_lat, _ = x_sample.shape
    ctx_row = B_lat
    n_rows = 8
    cc = jnp.concatenate([c, c_ctx[None], jnp.zeros((n_rows - B_lat - 1, D_MODEL), F32)], axis=0)
    mods = _modulation(cc, mod_w, mod_b).reshape(DEPTH, n_rows, N_MOD, D_MODEL)
    lat_row = lambda b: b
    ctx_mod_row = lambda b: ctx_row

    j = 0
    lam_init = 0.8 - 0.6 * math.exp(-0.3 * 0)
    w_in = mix_w_in[j].astype(BF16)
    w_out = mix_w_out[j].astype(BF16)
    w_glu = s5_w_glu[j].astype(BF16)
    ff = (ff_w1.astype(BF16), ff_w2.astype(BF16))
    s5w = _s5_weights(s5_lambda_re[j], s5_lambda_im[j], s5_log_dt[j], s5_b_re[j], s5_b_im[j],
                      s5_c_re[j], s5_c_im[j])

    qc, kc, vc, uc, kc32, vc32 = _premix(x_prompt, mods[0], ctx_mod_row, norm_g[0, 0], w_in,
                                         None, 256, True)
    ql, kl, vl, ul = _premix(x_sample, mods[0], lat_row, norm_g[0, 0], w_in,
                             _rope_tables(L_lat), 512, False)

    ac = _attention(qc, kc, vc, None, diff_lambda_qk[j], diff_subln_g[j], lam_init, 256, DA_HEADS)
    P = cache_k.shape[2]
    cache = (cache_k[:, j].reshape(B_lat, P, DA_WIDTH), cache_v[:, j].reshape(B_lat, P, DA_WIDTH))
    al = _attention(ql, kl, vl, cache, diff_lambda_qk[j], diff_subln_g[j], lam_init, 1024, 1)

    lat_split = 2
    h0_ctx = jnp.zeros((2, S5_OCT, 1, B_ctx, S5_SW), F32)
    h0_lat = state_s5[:, j].reshape(B_lat, 2, S5_OCT, S5_GPO, S5_STATE, 2)
    h0_lat = h0_lat.transpose(1, 2, 0, 5, 3, 4).reshape(
        2, S5_OCT, lat_split, B_lat // lat_split, S5_SW)
    gc, fin_c = _s5_mixer(uc, s5w, h0_ctx, s5_d[j], B_ctx, 1)
    gl, _ = _s5_mixer(ul, s5w, h0_lat, s5_d[j], B_lat, lat_split)

    ctx = _post0(x_prompt, ac, gc, mods[0], ctx_mod_row, norm_g[0, 1], w_glu, w_out, ff, 0, 256)
    lat = _post0(x_sample, al, gl, mods[0], lat_row, norm_g[0, 1], w_glu, w_out, ff, 0, 512)

    pw = pool_w[0].astype(BF16)
    y_prompt = _layer1(ctx, mods[1], ctx_mod_row, norm_g[1], pw, pool_scale[0], ff, 1,
                       final_norm_g, 256)
    y_sample = _layer1(lat, mods[1], lat_row, norm_g[1], pw, pool_scale[0], ff, 1,
                       final_norm_g, 512)

    new_cache_k = kc32.reshape(B_ctx, 1, L_ctx, DA_HEADS, 2 * DA_QK)
    new_cache_v = vc32.reshape(B_ctx, 1, L_ctx, DA_HEADS, DA_V)
    st = fin_c.reshape(2, S5_OCT, B_ctx, 2, S5_GPO, S5_STATE)
    new_state = st.transpose(2, 0, 1, 4, 5, 3).reshape(B_ctx, 1, 2, S5_GROUPS, S5_STATE, 2)
    return (y_prompt, y_sample, new_cache_k, new_cache_v, new_state)
```

```python
import functools
import math

import jax
import jax.numpy as jnp
from jax import lax
from jax.experimental import pallas as pl
from jax.experimental.pallas import tpu as pltpu

D_MODEL = 1024
DEPTH = 2
GRID_W = 64
DA_HEADS = 4
DA_QK = 64
DA_V = 2 * DA_QK
DA_WIDTH = DA_HEADS * DA_V
S5_WIDTH = D_MODEL - DA_WIDTH
S5_CH = 16
S5_GROUPS = S5_WIDTH // S5_CH
S5_STATE = 64
IN_WIDTH = 3 * DA_WIDTH + S5_WIDTH
POOL_WINDOWS = (2, 4, 8, 16)
POOL_CH = D_MODEL // len(POOL_WINDOWS)
FF_HIDDEN = 4 * D_MODEL
N_MOD = 6
ROPE_BASE = 10000.0
EPS = 1e-6

LANES = 128
S5_T = 8
S5_OCT = S5_WIDTH // LANES
S5_GPO = LANES // S5_CH
S5_ROWS = 512
S5_SW = 2 * S5_GPO * S5_STATE
Q_SCALE = DA_QK ** -0.5 * math.log2(math.e)
CTX_TILING = (2, 256, 256)
LAT_TILING = (1, 1024, 512)
ATTN_SUB = 256
POOL_HALO = 16
FF_CHUNK = 1024
VMEM_LIMIT = 56 * 1024 * 1024

F32 = jnp.float32
BF16 = jnp.bfloat16


def _const_spec(shape):
    nd = len(shape)
    return pl.BlockSpec(shape, lambda *_: (0,) * nd, pipeline_mode=pl.Buffered(1))


def _layer_spec(shape, layer):
    return pl.BlockSpec((None,) + shape, lambda *_: (layer, 0, 0), pipeline_mode=pl.Buffered(1))


def _params(n_axes):
    return pltpu.CompilerParams(dimension_semantics=("arbitrary",) * n_axes,
                                vmem_limit_bytes=VMEM_LIMIT)


def _rms(x, g):
    return x * lax.rsqrt(jnp.mean(x * x, axis=-1, keepdims=True) + EPS) * g


def _dot(a, b):
    return jnp.dot(a, b, preferred_element_type=F32)


def _mod_kernel(c_ref, w_ref, b_ref, o_ref):
    c = c_ref[...]
    s = (c * jax.nn.sigmoid(c)).astype(BF16)
    o_ref[0] = _dot(s, w_ref[0].astype(BF16)) + b_ref[0]


def _modulation(cc, mod_w, mod_b):
    rows = cc.shape[0]
    tn = 1536
    n = N_MOD * D_MODEL
    return pl.pallas_call(
        _mod_kernel,
        grid=(DEPTH, n // tn),
        in_specs=[pl.BlockSpec((rows, D_MODEL), lambda l, j: (0, 0)),
                  pl.BlockSpec((1, D_MODEL, tn), lambda l, j: (l, 0, j)),
                  pl.BlockSpec((1, 1, tn), lambda l, j: (l, 0, j))],
        out_specs=pl.BlockSpec((1, rows, tn), lambda l, j: (l, 0, j)),
        out_shape=jax.ShapeDtypeStruct((DEPTH, rows, n), F32),
        compiler_params=_params(2),
        name="modulation",
    )(cc, mod_w, mod_b.reshape(DEPTH, 1, n))


def _premix_kernel(*refs, rope, cache_out):
    x_ref, mod_ref, g_ref, w_ref = refs[:4]
    refs = refs[4:]
    if rope:
        cos_ref, sa_ref, sb_ref = refs[:3]
        refs = refs[3:]
    q_ref, k_ref, v_ref, u_ref = refs[:4]
    x = x_ref[0]
    h = _rms(x, g_ref[...]) * (1.0 + mod_ref[0, 1:2, :]) + mod_ref[0, 0:1, :]
    z = _dot(h.astype(BF16), w_ref[...])
    n_slab = 2 * DA_HEADS

    def rot(s):
        if not rope:
            return s
        return (s * cos_ref[...] + pltpu.roll(s, LANES - 16, axis=1) * sa_ref[...]
                + pltpu.roll(s, 16, axis=1) * sb_ref[...])

    for j in range(n_slab):
        s = rot(z[:, j * LANES:(j + 1) * LANES])
        if j < DA_HEADS:
            q_ref[0, :, j * LANES:(j + 1) * LANES] = (s * Q_SCALE).astype(BF16)
        else:
            jj = j - DA_HEADS
            k_ref[0, :, jj * LANES:(jj + 1) * LANES] = s.astype(BF16)
    v = z[:, 2 * DA_WIDTH:3 * DA_WIDTH]
    v_ref[0] = v.astype(BF16)
    ubuf_ref = refs[-1]
    n_chunk = z.shape[0] // S5_T
    for o in range(S5_OCT):
        ubuf_ref[o] = z[:, 3 * DA_WIDTH + o * LANES:3 * DA_WIDTH + (o + 1) * LANES]
    for o in range(S5_OCT):
        for s in range(S5_T):
            c0 = (o * S5_T + s) * LANES
            u_ref[:, c0:c0 + LANES] = ubuf_ref[o, pl.ds(s, n_chunk, stride=S5_T), :]
    if cache_out:
        kf_ref, vf_ref = refs[4:6]
        kf_ref[0] = z[:, DA_WIDTH:2 * DA_WIDTH]
        vf_ref[0] = v


def _chunk_spec(L, bb, tm):
    assert bb == 1 or tm == L
    return pl.BlockSpec((bb * tm // S5_T, S5_WIDTH * S5_T), lambda b, i: (b * (L // tm) + i, 0))


def _premix(x, mods, mod_row, norm_g, w_in, rope_tabs, tm, cache_out):
    B, L, _ = x.shape
    rope = rope_tabs is not None
    row_spec = lambda w: pl.BlockSpec((1, tm, w), lambda b, i: (b, i, 0))
    in_specs = [row_spec(D_MODEL),
                pl.BlockSpec((1, N_MOD, D_MODEL), lambda b, i: (mod_row(b), 0, 0)),
                _const_spec((1, D_MODEL)),
                _const_spec((D_MODEL, IN_WIDTH))]
    args = [x, mods, norm_g.reshape(1, D_MODEL), w_in]
    if rope:
        in_specs += [pl.BlockSpec((tm, LANES), lambda b, i: (i, 0))] * 3
        args += list(rope_tabs)
    out_specs = [row_spec(DA_WIDTH)] * 3 + [_chunk_spec(L, 1, tm)]
    out_shape = [jax.ShapeDtypeStruct((B, L, DA_WIDTH), BF16)] * 3 + [
        jax.ShapeDtypeStruct((B * L // S5_T, S5_WIDTH * S5_T), F32)]
    if cache_out:
        out_specs += [row_spec(DA_WIDTH)] * 2
        out_shape += [jax.ShapeDtypeStruct((B, L, DA_WIDTH), F32)] * 2
    return pl.pallas_call(
        functools.partial(_premix_kernel, rope=rope, cache_out=cache_out),
        grid=(B, L // tm),
        in_specs=in_specs, out_specs=out_specs, out_shape=out_shape,
        scratch_shapes=[pltpu.VMEM((S5_OCT, tm, LANES), F32)],
        compiler_params=_params(2),
        name="premix_rope" if rope else "premix",
    )(*args)


def _rope_tables(L):
    rows = L // GRID_W
    row = jnp.repeat(jnp.arange(rows), GRID_W).astype(F32)
    col = jnp.tile(jnp.arange(GRID_W), rows).astype(F32)
    half = DA_QK // 2
    inv = 1.0 / (ROPE_BASE ** (jnp.arange(0, half, 2, dtype=F32) / half))
    ang_r, ang_c = row[:, None] * inv, col[:, None] * inv
    zero = jnp.zeros_like(ang_r)
    cos64 = jnp.concatenate([jnp.cos(ang_r)] * 2 + [jnp.cos(ang_c)] * 2, axis=1)
    sa64 = jnp.concatenate([-jnp.sin(ang_r), zero, -jnp.sin(ang_c), zero], axis=1)
    sb64 = jnp.concatenate([zero, jnp.sin(ang_r), zero, jnp.sin(ang_c)], axis=1)
    return tuple(jnp.tile(t, (1, LANES // DA_QK)) for t in (cos64, sa64, sb64))


def _attn_kernel(*refs, lam_init, has_cache):
    lqk_ref, g_ref, q_ref, k_ref, v_ref = refs[:5]
    if has_cache:
        ck_ref, cv_ref, o_ref = refs[5:8]
    else:
        o_ref = refs[5]
    lqk = lqk_ref[...]
    lam = (jnp.exp(jnp.sum(lqk[0:1] * lqk[1:2], axis=-1, keepdims=True))
           - jnp.exp(jnp.sum(lqk[2:3] * lqk[3:4], axis=-1, keepdims=True)) + lam_init)
    tq = q_ref.shape[1]
    lane = lax.broadcasted_iota(jnp.int32, (ATTN_SUB, LANES), 1)
    zero = jnp.zeros((ATTN_SUB, LANES), BF16)
    nt = (((1,), (1,)), ((), ()))
    for hh in range(q_ref.shape[2] // LANES):
        cols = slice(hh * LANES, (hh + 1) * LANES)
        mxu_sum = has_cache
        ones = lambda v: jnp.concatenate([v, jnp.ones_like(v)], axis=1) if mxu_sum else v
        segs = [(k_ref[0, :, cols], ones(v_ref[0, :, cols]))]
        if has_cache:
            segs.append((ck_ref[0, :, cols].astype(BF16), ones(cv_ref[0, :, cols].astype(BF16))))
        for r in range(tq // ATTN_SUB):
            rows = slice(r * ATTN_SUB, (r + 1) * ATTN_SUB)
            q = q_ref[0, rows, cols]
            outs = []
            for qm in (jnp.where(lane < DA_QK, q, zero), jnp.where(lane >= DA_QK, q, zero)):
                ss = [lax.dot_general(qm, k, nt, preferred_element_type=F32) for k, _ in segs]
                m = functools.reduce(jnp.maximum,
                                     [jnp.max(s, axis=-1, keepdims=True) for s in ss])
                es = [jnp.exp2(s - m) for s in ss]
                o = functools.reduce(jnp.add,
                                     [_dot(e.astype(BF16), v) for e, (_, v) in zip(es, segs)])
                if mxu_sum:
                    l = o[:, DA_V:DA_V + 1]
                else:
                    l = functools.reduce(jnp.add, [jnp.sum(e, axis=-1, keepdims=True) for e in es])
                outs.append(o[:, :DA_V] / l)
            o = outs[0] - lam * outs[1]
            o_ref[0, rows, cols] = (_rms(o, g_ref[...]) * (1.0 - lam_init)).astype(BF16)


def _attention(q, k, v, cache, lambda_qk, subln_g, lam_init, tq, heads):
    B, L, _ = q.shape
    has_cache = cache is not None
    q_spec = pl.BlockSpec((1, tq, heads * LANES), lambda b, h, i: (b, i, h))
    kv_spec = lambda n: pl.BlockSpec((1, n, heads * LANES), lambda b, h, i: (b, 0, h))
    in_specs = [_const_spec((4, DA_QK)), _const_spec((1, DA_V)), q_spec, kv_spec(L), kv_spec(L)]
    args = [lambda_qk, subln_g.reshape(1, DA_V), q, k, v]
    if has_cache:
        P = cache[0].shape[1]
        in_specs += [kv_spec(P), kv_spec(P)]
        args += list(cache)
    return pl.pallas_call(
        functools.partial(_attn_kernel, lam_init=lam_init, has_cache=has_cache),
        grid=(B, DA_HEADS // heads, L // tq),
        in_specs=in_specs, out_specs=q_spec,
        out_shape=jax.ShapeDtypeStruct((B, L, DA_WIDTH), BF16),
        compiler_params=_params(3),
        name="diff_attn_cache" if has_cache else "diff_attn",
    )(*args)


def _s5_weights(lam_re, lam_im, log_dt, b_re, b_im, c_re, c_im):
    T, G, P, H = S5_T, S5_GROUPS, S5_STATE, S5_CH
    dt = jnp.exp(log_dt)[..., None]
    lr, li = lam_re * dt, lam_im * dt
    kk = jnp.arange(T + 1, dtype=F32)[None, :, None, None]
    mag = jnp.exp(kk * lr[:, None])
    pw_re = mag * jnp.cos(kk * li[:, None])
    pw_im = mag * jnp.sin(kk * li[:, None])
    a_re, a_im = pw_re[:, 1], pw_im[:, 1]
    den = lam_re * lam_re + lam_im * lam_im
    f_re = ((a_re - 1.0) * lam_re + a_im * lam_im) / den
    f_im = (a_im * lam_re - (a_re - 1.0) * lam_im) / den
    bt_re, bt_im = jnp.swapaxes(b_re, -1, -2), jnp.swapaxes(b_im, -1, -2)
    bb_re = f_re[:, :, None, :] * bt_re - f_im[:, :, None, :] * bt_im
    bb_im = f_re[:, :, None, :] * bt_im + f_im[:, :, None, :] * bt_re

    def per_dir(x, fwd, bwd):
        return jnp.stack([fwd(x[0]), bwd(x[1])], axis=0)

    flip = lambda x: x[::-1]
    pr = per_dir(pw_re, lambda x: flip(x[:T]), lambda x: x[:T])[:, :, :, None, :]
    pi = per_dir(pw_im, lambda x: flip(x[:T]), lambda x: x[:T])[:, :, :, None, :]
    br, bi = bb_re[:, None], bb_im[:, None]
    fc = jnp.stack([pr * br - pi * bi, pr * bi + pi * br], axis=1)
    fc = fc.reshape(2, 2, T, S5_OCT, LANES, P)
    fc = jnp.concatenate([fc, fc], axis=-1)

    def lanes_gh(x):
        x = x.reshape(x.shape[:-2] + (S5_OCT, S5_GPO, P))
        return jnp.repeat(jnp.swapaxes(x, -1, -2), H, axis=-1)

    def c_lanes(c):
        c = c.reshape(2, S5_OCT, S5_GPO, H, P).transpose(0, 1, 4, 2, 3)
        return c.reshape(2, 1, S5_OCT, P, LANES)

    per = lanes_gh(per_dir(pw_re, lambda x: x[1:], lambda x: flip(x[1:])))
    pei = lanes_gh(per_dir(pw_im, lambda x: x[1:], lambda x: flip(x[1:])))
    cr, ci = c_lanes(c_re), c_lanes(c_im)
    ec = jnp.stack([cr * per - ci * pei, -(cr * pei + ci * per)], axis=1)
    ec = jnp.concatenate([ec, ec], axis=-2)

    kr, ki = pw_re[:, :T, :, None, :], pw_im[:, :T, :, None, :]
    car = c_re[:, None] * kr - c_im[:, None] * ki
    cai = c_re[:, None] * ki + c_im[:, None] * kr
    kd = jnp.sum(car[:, :, :, None] * bb_re[:, None, :, :, None]
                 - cai[:, :, :, None] * bb_im[:, None, :, :, None], axis=-1)
    kf, kb = kd[0], kd[1]
    klag = jnp.concatenate([kb[:0:-1], (kf[0] + kb[0])[None], kf[1:]], axis=0)
    kt = jnp.tile(klag.reshape(2 * T - 1, S5_OCT, LANES, H), (1, 1, 1, S5_GPO))

    at = jnp.stack([pw_re[:, T], pw_im[:, T]], axis=1)
    at = at.reshape(2, 2, S5_OCT, S5_GPO * P).transpose(0, 2, 1, 3)
    return kt.astype(BF16), fc.astype(BF16), ec.astype(BF16), at


def _s5_kernel(*refs, n_seq, n_chunk):
    T = S5_T
    (x_ref, kt_ref, fc_ref, ec_ref, at_ref, h0_ref, d_ref, out_ref, fin_ref,
     sf_ref, sb_ref, wt_ref, wf_ref, we_ref) = refs
    n_tile = S5_SW // LANES
    nh = n_tile // 2

    @pl.when(pl.program_id(1) == 0)
    def _():
        _s5_expand(kt_ref, fc_ref, ec_ref, wt_ref, wf_ref, we_ref)

    M = x_ref.shape[0]
    blocks = [slice(r, r + S5_ROWS) for r in range(0, M, S5_ROWS)]

    def to_tiles(ref, rows, val):
        for j in range(n_tile):
            ref[j, rows, :] = val[:, j * LANES:(j + 1) * LANES]

    def from_tiles(ref, rows):
        return jnp.concatenate([ref[j, rows, :].astype(BF16) for j in range(n_tile)], axis=1)

    for rows in blocks:
        xb = x_ref[rows, :].astype(BF16)
        to_tiles(sf_ref, rows, _dot(xb, wf_ref[0]))
        to_tiles(sb_ref, rows, _dot(xb, wf_ref[1]))

    def coeff(d):
        return [(at_ref[d, 0, 0:1, j * LANES:(j + 1) * LANES],
                 at_ref[d, 0, 1:2, j * LANES:(j + 1) * LANES]) for j in range(nh)]

    def advance(ref, rows, a, state):
        out = []
        for j in range(nh):
            (ar, ai), (hr, hi) = a[j], state[j]
            sr, si = ref[j, rows, :], ref[nh + j, rows, :]
            ref[j, rows, :] = hr
            ref[nh + j, rows, :] = hi
            out.append((ar * hr - ai * hi + sr, ar * hi + ai * hr + si))
        return out

    af, ab = coeff(0), coeff(1)

    def step(i, carry):
        hf, hb = carry
        hf = advance(sf_ref, pl.ds(i, n_seq, stride=n_chunk), af, hf)
        hb = advance(sb_ref, pl.ds(n_chunk - 1 - i, n_seq, stride=n_chunk), ab, hb)
        return hf, hb

    def initial(d):
        return [(h0_ref[d, 0, 0, :, j * LANES:(j + 1) * LANES],
                 h0_ref[d, 0, 0, :, (nh + j) * LANES:(nh + j + 1) * LANES]) for j in range(nh)]

    finals = lax.fori_loop(0, n_chunk, step, (initial(0), initial(1)))
    for d in range(2):
        for j in range(nh):
            fin_ref[d, 0, 0, :, j * LANES:(j + 1) * LANES] = finals[d][j][0]
            fin_ref[d, 0, 0, :, (nh + j) * LANES:(nh + j + 1) * LANES] = finals[d][j][1]

    d = jnp.concatenate([d_ref[0]] * T, axis=1)
    for rows in blocks:
        x = x_ref[rows, :]
        y = (_dot(x.astype(BF16), wt_ref[...]) + _dot(from_tiles(sf_ref, rows), we_ref[0])
             + _dot(from_tiles(sb_ref, rows), we_ref[1]))
        out_ref[rows, :] = jax.nn.gelu(y + d * x)


def _s5_expand(kt_ref, fc_ref, ec_ref, wt_ref, wf_ref, we_ref):
    T = S5_T
    nh = S5_SW // LANES // 2
    row = lax.broadcasted_iota(jnp.int32, (LANES, LANES), 0)
    lane = lax.broadcasted_iota(jnp.int32, (LANES, LANES), 1)
    own_k = (row // S5_CH) == (lane // S5_CH)
    lag_tiles = [jnp.where(own_k, kt_ref[l, 0].astype(F32), 0.0).astype(BF16)
                 for l in range(2 * T - 1)]
    for s in range(T):
        for t in range(T):
            wt_ref[s * LANES:(s + 1) * LANES, t * LANES:(t + 1) * LANES] = lag_tiles[t - s + T - 1]
    gpt = LANES // S5_STATE
    own_f = [(row // S5_CH) == (gpt * j + lane // S5_STATE) for j in range(nh)]
    own_e = [(lane // S5_CH) == (gpt * j + row // S5_STATE) for j in range(nh)]
    for d in range(2):
        for ri in range(2):
            for s in range(T):
                f, e = fc_ref[d, ri, s, 0].astype(F32), ec_ref[d, ri, s, 0].astype(F32)
                rows = slice(s * LANES, (s + 1) * LANES)
                for j in range(nh):
                    cols = slice((ri * nh + j) * LANES, (ri * nh + j + 1) * LANES)
                    wf_ref[d, rows, cols] = jnp.where(own_f[j], f, 0.0).astype(BF16)
                    we_ref[d, cols, rows] = jnp.where(own_e[j], e, 0.0).astype(BF16)


def _s5_mixer(u, weights, h0, d_skip, n_seq, n_split):
    T = S5_T
    M_all = u.shape[0]
    M = M_all // n_split
    B = n_seq // n_split
    n_chunk = M // B
    kt, fc, ec, at = weights
    TL = T * LANES
    x_spec = pl.BlockSpec((M, TL), lambda o, j: (j, o))
    tile_spec = pl.BlockSpec((2, 2, T, 1, LANES, LANES), lambda o, j: (0, 0, 0, o, 0, 0))
    st_spec = pl.BlockSpec((2, 1, 1, B, S5_SW), lambda o, j: (0, o, j, 0, 0))
    g, fin = pl.pallas_call(
        functools.partial(_s5_kernel, n_seq=B, n_chunk=n_chunk),
        grid=(S5_OCT, n_split),
        in_specs=[x_spec,
                  pl.BlockSpec((2 * T - 1, 1, LANES, LANES), lambda o, j: (0, o, 0, 0)),
                  tile_spec, tile_spec,
                  pl.BlockSpec((2, 1, 2, S5_SW // 2), lambda o, j: (0, o, 0, 0)),
                  st_spec,
                  pl.BlockSpec((1, 1, LANES), lambda o, j: (o, 0, 0))],
        out_specs=[x_spec, st_spec],
        out_shape=[jax.ShapeDtypeStruct((M_all, S5_OCT * TL), F32),
                   jax.ShapeDtypeStruct((2, S5_OCT, n_split, B, S5_SW), F32)],
        scratch_shapes=[pltpu.VMEM((S5_SW // LANES, M, LANES), F32)] * 2 + [
            pltpu.VMEM((TL, TL), BF16), pltpu.VMEM((2, TL, S5_SW), BF16),
            pltpu.VMEM((2, S5_SW, TL), BF16)],
        compiler_params=_params(2),
        name="s5_mixer",
    )(u, kt, fc, ec, at, h0, d_skip.reshape(S5_OCT, 1, LANES))
    return g, fin


def _mlp(h, w1_ref, w2_ref):
    hb = h.astype(BF16)
    acc = None
    for j in range(FF_HIDDEN // FF_CHUNK):
        a = _dot(hb, w1_ref[:, j * FF_CHUNK:(j + 1) * FF_CHUNK])
        a = jnp.square(jnp.maximum(a, 0.0)).astype(BF16)
        p = _dot(a, w2_ref[j * FF_CHUNK:(j + 1) * FF_CHUNK, :])
        acc = p if acc is None else acc + p
    return acc


def _post0_kernel(x_ref, attn_ref, g_ref, mod_ref, ng_ref, wglu_ref, wout_ref, w1_ref, w2_ref, o_ref,
                  gbuf_ref, *, sub):
    bb, tm = x_ref.shape[0], x_ref.shape[1]
    n_sub = tm // sub
    cps = sub // S5_T
    for k in range(bb * n_sub):
        bl, rows = k // n_sub, slice((k % n_sub) * sub, (k % n_sub + 1) * sub)
        for o in range(S5_OCT):
            for s in range(S5_T):
                c0 = (o * S5_T + s) * LANES
                gbuf_ref[k, o, pl.ds(s, cps, stride=S5_T), :] = (
                    g_ref[k * cps:(k + 1) * cps, c0:c0 + LANES])
        g = jnp.concatenate([gbuf_ref[k, o] for o in range(S5_OCT)], axis=1)
        s5 = g * jax.nn.sigmoid(_dot(g.astype(BF16), wglu_ref[...]))
        oc = (_dot(attn_ref[bl, rows], wout_ref[0:DA_WIDTH, :])
              + _dot(s5.astype(BF16), wout_ref[DA_WIDTH:D_MODEL, :]))
        x1 = x_ref[bl, rows] + mod_ref[0, 2:3, :] * oc
        h = _rms(x1, ng_ref[...]) * (1.0 + mod_ref[0, 4:5, :]) + mod_ref[0, 3:4, :]
        o_ref[bl, rows] = x1 + mod_ref[0, 5:6, :] * _mlp(h, w1_ref, w2_ref)


def _post0(x, attn, g, mods, mod_row, norm_g, w_glu, w_out, ff, layer, tiling):
    B, L, _ = x.shape
    bb, tm, sub = tiling
    w1, w2 = ff
    row_spec = lambda w: pl.BlockSpec((bb, tm, w), lambda b, i: (b, i, 0))
    return pl.pallas_call(
        functools.partial(_post0_kernel, sub=sub),
        grid=(B // bb, L // tm),
        in_specs=[row_spec(D_MODEL), row_spec(DA_WIDTH), _chunk_spec(L, bb, tm),
                  pl.BlockSpec((1, N_MOD, D_MODEL), lambda b, i: (mod_row(b), 0, 0)),
                  _const_spec((1, D_MODEL)),
                  _const_spec((S5_WIDTH, S5_WIDTH)),
                  _const_spec((D_MODEL, D_MODEL)),
                  _layer_spec((D_MODEL, FF_HIDDEN), layer),
                  _layer_spec((FF_HIDDEN, D_MODEL), layer)],
        out_specs=row_spec(D_MODEL),
        out_shape=jax.ShapeDtypeStruct((B, L, D_MODEL), F32),
        scratch_shapes=[pltpu.VMEM((bb * tm // sub, S5_OCT, sub, LANES), F32)],
        compiler_params=_params(2),
        name="mix_out_mlp",
    )(x, attn, g, mods, norm_g.reshape(1, D_MODEL), w_glu, w_out, w1, w2)


def _layer1_kernel(x_ref, xp_ref, xn_ref, mod_ref, ng1_ref, ng2_ref, pw_ref, ps_ref,
                   w1_ref, w2_ref, fg_ref, o_ref, *, sub, seq_len):
    bb, tm = x_ref.shape[0], x_ref.shape[1]
    n_sub = tm // sub
    H = POOL_HALO
    n = sub + 2 * H
    for k in range(bb * n_sub):
        bl, r0 = k // n_sub, (k % n_sub) * sub
        x = x_ref[bl, r0:r0 + sub]
        top = xp_ref[bl] if r0 == 0 else x_ref[bl, r0 - H:r0]
        bot = xn_ref[bl] if r0 + sub == tm else x_ref[bl, r0 + sub:r0 + sub + H]
        xe = jnp.concatenate([top, x, bot], axis=0)
        he = _rms(xe, ng1_ref[...]) * (1.0 + mod_ref[0, 1:2, :]) + mod_ref[0, 0:1, :]
        first = pl.program_id(1) * tm + r0
        pos_e = first - H + lax.broadcasted_iota(jnp.int32, (n, 1), 0)
        he = jnp.where((pos_e >= 0) & (pos_e < seq_len), he, 0.0)
        pos = first + lax.broadcasted_iota(jnp.int32, (sub, 1), 0)
        zs = []
        for gi, w in enumerate(POOL_WINDOWS):
            hg = he[:, gi * POOL_CH:(gi + 1) * POOL_CH]
            acc = hg + pltpu.roll(hg, 1, axis=0)
            r = 1
            while 2 * r < w:
                acc = pltpu.roll(acc, r, axis=0) + pltpu.roll(acc, n - r, axis=0)
                r *= 2
            lo = jnp.clip(pos - w // 2, 0, seq_len)
            hi = jnp.clip(pos + w // 2, 0, seq_len)
            mean = acc[H:H + sub] / (hi - lo).astype(F32)
            z = mean - hg[H:H + sub]
            zs.append(_dot(z.astype(BF16), pw_ref[gi]))
        oc = jnp.concatenate(zs, axis=1) * ps_ref[...]
        x1 = x + mod_ref[0, 2:3, :] * oc
        h = _rms(x1, ng2_ref[...]) * (1.0 + mod_ref[0, 4:5, :]) + mod_ref[0, 3:4, :]
        x2 = x1 + mod_ref[0, 5:6, :] * _mlp(h, w1_ref, w2_ref)
        o_ref[bl, r0:r0 + sub] = _rms(x2, fg_ref[...])


def _layer1(x, mods, mod_row, norm_g, pool_w, pool_scale, ff, layer, final_g, tiling):
    B, L, _ = x.shape
    bb, tm, sub = tiling
    w1, w2 = ff
    H = POOL_HALO
    per = tm // H
    last = L // H - 1
    row_spec = pl.BlockSpec((bb, tm, D_MODEL), lambda b, i: (b, i, 0))
    return pl.pallas_call(
        functools.partial(_layer1_kernel, sub=sub, seq_len=L),
        grid=(B // bb, L // tm),
        in_specs=[row_spec,
                  pl.BlockSpec((bb, H, D_MODEL), lambda b, i: (b, jnp.maximum(i * per - 1, 0), 0)),
                  pl.BlockSpec((bb, H, D_MODEL), lambda b, i: (b, jnp.minimum((i + 1) * per, last), 0)),
                  pl.BlockSpec((1, N_MOD, D_MODEL), lambda b, i: (mod_row(b), 0, 0)),
                  _const_spec((1, D_MODEL)), _const_spec((1, D_MODEL)),
                  _const_spec((len(POOL_WINDOWS), POOL_CH, POOL_CH)),
                  _const_spec((1, D_MODEL)),
                  _layer_spec((D_MODEL, FF_HIDDEN), layer),
                  _layer_spec((FF_HIDDEN, D_MODEL), layer),
                  _const_spec((1, D_MODEL))],
        out_specs=row_spec,
        out_shape=jax.ShapeDtypeStruct((B, L, D_MODEL), F32),
        compiler_params=_params(2),
        name="pool_mlp_norm",
    )(x, x, x, mods, norm_g[0].reshape(1, D_MODEL), norm_g[1].reshape(1, D_MODEL),
      pool_w, pool_scale.reshape(1, D_MODEL), w1, w2, final_g.reshape(1, D_MODEL))


def kernel(x_prompt, x_sample, cache_k, cache_v, state_s5, c, c_ctx, mod_w, mod_b, norm_g, mix_w_in, mix_w_out, diff_lambda_qk, diff_subln_g, s5_lambda_re, s5_lambda_im, s5_log_dt, s5_b_re, s5_b_im, s5_c_re, s5_c_im, s5_d, s5_w_glu, pool_w, pool_scale, ff_w1, ff_w2, final_norm_g):
    B_ctx, L_ctx, _ = x_prompt.shape
    B_lat, L_lat, _ = x_sample.shape
    ctx_row = B_lat
    n_rows = 8
    cc = jnp.concatenate([c, c_ctx[None], jnp.zeros((n_rows - B_lat - 1, D_MODEL), F32)], axis=0)
    mods = _modulation(cc, mod_w, mod_b).reshape(DEPTH, n_rows, N_MOD, D_MODEL)
    lat_row = lambda b: b
    ctx_mod_row = lambda b: ctx_row

    j = 0
    lam_init = 0.8 - 0.6 * math.exp(-0.3 * 0)
    w_in = mix_w_in[j].astype(BF16)
    w_out = mix_w_out[j].astype(BF16)
    w_glu = s5_w_glu[j].astype(BF16)
    ff = (ff_w1.astype(BF16), ff_w2.astype(BF16))
    s5w = _s5_weights(s5_lambda_re[j], s5_lambda_im[j], s5_log_dt[j], s5_b_re[j], s5_b_im[j],
                      s5_c_re[j], s5_c_im[j])

    qc, kc, vc, uc, kc32, vc32 = _premix(x_prompt, mods[0], ctx_mod_row, norm_g[0, 0], w_in,
                                         None, 256, True)
    ql, kl, vl, ul = _premix(x_sample, mods[0], lat_row, norm_g[0, 0], w_in,
                             _rope_tables(L_lat), 512, False)

    ac = _attention(qc, kc, vc, None, diff_lambda_qk[j], diff_subln_g[j], lam_init, 256, DA_HEADS)
    P = cache_k.shape[2]
    cache = (cache_k[:, j].reshape(B_lat, P, DA_WIDTH), cache_v[:, j].reshape(B_lat, P, DA_WIDTH))
    al = _attention(ql, kl, vl, cache, diff_lambda_qk[j], diff_subln_g[j], lam_init, 1024, 1)

    lat_split = 1
    h0_ctx = jnp.zeros((2, S5_OCT, 1, B_ctx, S5_SW), F32)
    h0_lat = state_s5[:, j].reshape(B_lat, 2, S5_OCT, S5_GPO, S5_STATE, 2)
    h0_lat = h0_lat.transpose(1, 2, 0, 5, 3, 4).reshape(
        2, S5_OCT, lat_split, B_lat // lat_split, S5_SW)
    gc, fin_c = _s5_mixer(uc, s5w, h0_ctx, s5_d[j], B_ctx, 1)
    gl, _ = _s5_mixer(ul, s5w, h0_lat, s5_d[j], B_lat, lat_split)

    ctx = _post0(x_prompt, ac, gc, mods[0], ctx_mod_row, norm_g[0, 1], w_glu, w_out, ff, 0,
                 CTX_TILING)
    lat = _post0(x_sample, al, gl, mods[0], lat_row, norm_g[0, 1], w_glu, w_out, ff, 0,
                 LAT_TILING)

    pw = pool_w[0].astype(BF16)
    y_prompt = _layer1(ctx, mods[1], ctx_mod_row, norm_g[1], pw, pool_scale[0], ff, 1,
                       final_norm_g, CTX_TILING)
    y_sample = _layer1(lat, mods[1], lat_row, norm_g[1], pw, pool_scale[0], ff, 1,
                       final_norm_g, LAT_TILING)

    new_cache_k = kc32.reshape(B_ctx, 1, L_ctx, DA_HEADS, 2 * DA_QK)
    new_cache_v = vc32.reshape(B_ctx, 1, L_ctx, DA_HEADS, DA_V)
    st = fin_c.reshape(2, S5_OCT, B_ctx, 2, S5_GPO, S5_STATE)
    new_state = st.transpose(2, 0, 1, 4, 5, 3).reshape(B_ctx, 1, 2, S5_GROUPS, S5_STATE, 2)
    return (y_prompt, y_sample, new_cache_k, new_cache_v, new_state)
```

```python
import functools
import math

import jax
import jax.numpy as jnp
from jax import lax
from jax.experimental import pallas as pl
from jax.experimental.pallas import tpu as pltpu

D_MODEL = 1024
DEPTH = 2
GRID_W = 64
DA_HEADS = 4
DA_QK = 64
DA_V = 2 * DA_QK
DA_WIDTH = DA_HEADS * DA_V
S5_WIDTH = D_MODEL - DA_WIDTH
S5_CH = 16
S5_GROUPS = S5_WIDTH // S5_CH
S5_STATE = 64
IN_WIDTH = 3 * DA_WIDTH + S5_WIDTH
POOL_WINDOWS = (2, 4, 8, 16)
POOL_CH = D_MODEL // len(POOL_WINDOWS)
FF_HIDDEN = 4 * D_MODEL
N_MOD = 6
ROPE_BASE = 10000.0
EPS = 1e-6

SUBLANES = 8
LANES = 128
S5_T = 8
S5_OCT = S5_WIDTH // LANES
S5_GPO = LANES // S5_CH
S5_ROWS = 512
S5_SW = 2 * S5_GPO * S5_STATE
Q_SCALE = DA_QK ** -0.5 * math.log2(math.e)
CTX_TILING = (2, 256, 256)
LAT_TILING = (1, 1024, 512)
ATTN_SUB = 256
POOL_HALO = 16
FF_CHUNK = 1024
VMEM_LIMIT = 56 * 1024 * 1024

F32 = jnp.float32
BF16 = jnp.bfloat16


def _const_spec(shape):
    nd = len(shape)
    return pl.BlockSpec(shape, lambda *_: (0,) * nd, pipeline_mode=pl.Buffered(1))


def _layer_spec(shape, layer):
    return pl.BlockSpec((None,) + shape, lambda *_: (layer, 0, 0), pipeline_mode=pl.Buffered(1))


def _params(n_axes):
    return pltpu.CompilerParams(dimension_semantics=("arbitrary",) * n_axes,
                                vmem_limit_bytes=VMEM_LIMIT)


def _rms(x, g):
    return x * lax.rsqrt(jnp.mean(x * x, axis=-1, keepdims=True) + EPS) * g


def _dot(a, b):
    return jnp.dot(a, b, preferred_element_type=F32)


def _mod_kernel(c_ref, w_ref, b_ref, o_ref):
    c = c_ref[...]
    s = (c * jax.nn.sigmoid(c)).astype(BF16)
    o_ref[0] = _dot(s, w_ref[0].astype(BF16)) + b_ref[0]


def _modulation(cc, mod_w, mod_b):
    rows = cc.shape[0]
    tn = 1536
    n = N_MOD * D_MODEL
    return pl.pallas_call(
        _mod_kernel,
        grid=(DEPTH, n // tn),
        in_specs=[pl.BlockSpec((rows, D_MODEL), lambda l, j: (0, 0)),
                  pl.BlockSpec((1, D_MODEL, tn), lambda l, j: (l, 0, j)),
                  pl.BlockSpec((1, 1, tn), lambda l, j: (l, 0, j))],
        out_specs=pl.BlockSpec((1, rows, tn), lambda l, j: (l, 0, j)),
        out_shape=jax.ShapeDtypeStruct((DEPTH, rows, n), F32),
        compiler_params=_params(2),
        name="modulation",
    )(cc, mod_w, mod_b.reshape(DEPTH, 1, n))


def _premix_kernel(*refs, rope, cache_out):
    x_ref, mod_ref, g_ref, w_ref = refs[:4]
    refs = refs[4:]
    if rope:
        cos_ref, sa_ref, sb_ref = refs[:3]
        refs = refs[3:]
    q_ref, k_ref, v_ref, u_ref = refs[:4]
    x = x_ref[0]
    h = _rms(x, g_ref[...]) * (1.0 + mod_ref[0, 1:2, :]) + mod_ref[0, 0:1, :]
    z = _dot(h.astype(BF16), w_ref[...])
    n_slab = 2 * DA_HEADS

    def rot(s):
        if not rope:
            return s
        return (s * cos_ref[...] + pltpu.roll(s, LANES - 16, axis=1) * sa_ref[...]
                + pltpu.roll(s, 16, axis=1) * sb_ref[...])

    for j in range(n_slab):
        s = rot(z[:, j * LANES:(j + 1) * LANES])
        if j < DA_HEADS:
            q_ref[0, :, j * LANES:(j + 1) * LANES] = (s * Q_SCALE).astype(BF16)
        else:
            jj = j - DA_HEADS
            k_ref[0, :, jj * LANES:(jj + 1) * LANES] = s.astype(BF16)
    v = z[:, 2 * DA_WIDTH:3 * DA_WIDTH]
    v_ref[0] = v.astype(BF16)
    ubuf_ref = refs[-1]
    n_chunk = z.shape[0] // S5_T
    for o in range(S5_OCT):
        ubuf_ref[o] = z[:, 3 * DA_WIDTH + o * LANES:3 * DA_WIDTH + (o + 1) * LANES]
    for o in range(S5_OCT):
        for s in range(S5_T):
            c0 = (o * S5_T + s) * LANES
            u_ref[:, c0:c0 + LANES] = ubuf_ref[o, pl.ds(s, n_chunk, stride=S5_T), :]
    if cache_out:
        kf_ref, vf_ref = refs[4:6]
        kf_ref[0] = z[:, DA_WIDTH:2 * DA_WIDTH]
        vf_ref[0] = v


def _chunk_spec(L, bb, tm):
    assert bb == 1 or tm == L
    return pl.BlockSpec((bb * tm // S5_T, S5_WIDTH * S5_T), lambda b, i: (b * (L // tm) + i, 0))


def _premix(x, mods, mod_row, norm_g, w_in, rope_tabs, tm, cache_out):
    B, L, _ = x.shape
    rope = rope_tabs is not None
    row_spec = lambda w: pl.BlockSpec((1, tm, w), lambda b, i: (b, i, 0))
    in_specs = [row_spec(D_MODEL),
                pl.BlockSpec((1, N_MOD, D_MODEL), lambda b, i: (mod_row(b), 0, 0)),
                _const_spec((1, D_MODEL)),
                _const_spec((D_MODEL, IN_WIDTH))]
    args = [x, mods, norm_g.reshape(1, D_MODEL), w_in]
    if rope:
        in_specs += [pl.BlockSpec((tm, LANES), lambda b, i: (i, 0))] * 3
        args += list(rope_tabs)
    out_specs = [row_spec(DA_WIDTH)] * 3 + [_chunk_spec(L, 1, tm)]
    out_shape = [jax.ShapeDtypeStruct((B, L, DA_WIDTH), BF16)] * 3 + [
        jax.ShapeDtypeStruct((B * L // S5_T, S5_WIDTH * S5_T), F32)]
    if cache_out:
        out_specs += [row_spec(DA_WIDTH)] * 2
        out_shape += [jax.ShapeDtypeStruct((B, L, DA_WIDTH), F32)] * 2
    return pl.pallas_call(
        functools.partial(_premix_kernel, rope=rope, cache_out=cache_out),
        grid=(B, L // tm),
        in_specs=in_specs, out_specs=out_specs, out_shape=out_shape,
        scratch_shapes=[pltpu.VMEM((S5_OCT, tm, LANES), F32)],
        compiler_params=_params(2),
        name="premix_rope" if rope else "premix",
    )(*args)


def _rope_tables(L):
    rows = L // GRID_W
    row = jnp.repeat(jnp.arange(rows), GRID_W).astype(F32)
    col = jnp.tile(jnp.arange(GRID_W), rows).astype(F32)
    half = DA_QK // 2
    inv = 1.0 / (ROPE_BASE ** (jnp.arange(0, half, 2, dtype=F32) / half))
    ang_r, ang_c = row[:, None] * inv, col[:, None] * inv
    zero = jnp.zeros_like(ang_r)
    cos64 = jnp.concatenate([jnp.cos(ang_r)] * 2 + [jnp.cos(ang_c)] * 2, axis=1)
    sa64 = jnp.concatenate([-jnp.sin(ang_r), zero, -jnp.sin(ang_c), zero], axis=1)
    sb64 = jnp.concatenate([zero, jnp.sin(ang_r), zero, jnp.sin(ang_c)], axis=1)
    return tuple(jnp.tile(t, (1, LANES // DA_QK)) for t in (cos64, sa64, sb64))


def _attn_kernel(*refs, lam_init, has_cache):
    lqk_ref, g_ref, q_ref, k_ref, v_ref = refs[:5]
    if has_cache:
        ck_ref, cv_ref, o_ref = refs[5:8]
    else:
        o_ref = refs[5]
    lqk = lqk_ref[...]
    lam = (jnp.exp(jnp.sum(lqk[0:1] * lqk[1:2], axis=-1, keepdims=True))
           - jnp.exp(jnp.sum(lqk[2:3] * lqk[3:4], axis=-1, keepdims=True)) + lam_init)
    tq = q_ref.shape[1]
    lane = lax.broadcasted_iota(jnp.int32, (ATTN_SUB, LANES), 1)
    zero = jnp.zeros((ATTN_SUB, LANES), BF16)
    nt = (((1,), (1,)), ((), ()))
    for hh in range(q_ref.shape[2] // LANES):
        cols = slice(hh * LANES, (hh + 1) * LANES)
        mxu_sum = has_cache
        ones = lambda v: jnp.concatenate([v, jnp.ones_like(v)], axis=1) if mxu_sum else v
        segs = [(k_ref[0, :, cols], ones(v_ref[0, :, cols]))]
        if has_cache:
            segs.append((ck_ref[0, :, cols].astype(BF16), ones(cv_ref[0, :, cols].astype(BF16))))
        for r in range(tq // ATTN_SUB):
            rows = slice(r * ATTN_SUB, (r + 1) * ATTN_SUB)
            q = q_ref[0, rows, cols]
            outs = []
            for qm in (jnp.where(lane < DA_QK, q, zero), jnp.where(lane >= DA_QK, q, zero)):
                ss = [lax.dot_general(qm, k, nt, preferred_element_type=F32) for k, _ in segs]
                m = functools.reduce(jnp.maximum,
                                     [jnp.max(s, axis=-1, keepdims=True) for s in ss])
                es = [jnp.exp2(s - m) for s in ss]
                o = functools.reduce(jnp.add,
                                     [_dot(e.astype(BF16), v) for e, (_, v) in zip(es, segs)])
                if mxu_sum:
                    l = o[:, DA_V:DA_V + 1]
                else:
                    l = functools.reduce(jnp.add, [jnp.sum(e, axis=-1, keepdims=True) for e in es])
                outs.append(o[:, :DA_V] / l)
            o = outs[0] - lam * outs[1]
            o_ref[0, rows, cols] = (_rms(o, g_ref[...]) * (1.0 - lam_init)).astype(BF16)


def _attention(q, k, v, cache, lambda_qk, subln_g, lam_init, tq, heads):
    B, L, _ = q.shape
    has_cache = cache is not None
    q_spec = pl.BlockSpec((1, tq, heads * LANES), lambda b, h, i: (b, i, h))
    kv_spec = lambda n: pl.BlockSpec((1, n, heads * LANES), lambda b, h, i: (b, 0, h))
    in_specs = [_const_spec((4, DA_QK)), _const_spec((1, DA_V)), q_spec, kv_spec(L), kv_spec(L)]
    args = [lambda_qk, subln_g.reshape(1, DA_V), q, k, v]
    if has_cache:
        P = cache[0].shape[1]
        in_specs += [kv_spec(P), kv_spec(P)]
        args += list(cache)
    return pl.pallas_call(
        functools.partial(_attn_kernel, lam_init=lam_init, has_cache=has_cache),
        grid=(B, DA_HEADS // heads, L // tq),
        in_specs=in_specs, out_specs=q_spec,
        out_shape=jax.ShapeDtypeStruct((B, L, DA_WIDTH), BF16),
        compiler_params=_params(3),
        name="diff_attn_cache" if has_cache else "diff_attn",
    )(*args)


def _s5_weights(lam_re, lam_im, log_dt, b_re, b_im, c_re, c_im):
    T, G, P, H = S5_T, S5_GROUPS, S5_STATE, S5_CH
    dt = jnp.exp(log_dt)[..., None]
    lr, li = lam_re * dt, lam_im * dt
    kk = jnp.arange(T + 1, dtype=F32)[None, :, None, None]
    mag = jnp.exp(kk * lr[:, None])
    pw_re = mag * jnp.cos(kk * li[:, None])
    pw_im = mag * jnp.sin(kk * li[:, None])
    a_re, a_im = pw_re[:, 1], pw_im[:, 1]
    den = lam_re * lam_re + lam_im * lam_im
    f_re = ((a_re - 1.0) * lam_re + a_im * lam_im) / den
    f_im = (a_im * lam_re - (a_re - 1.0) * lam_im) / den
    bt_re, bt_im = jnp.swapaxes(b_re, -1, -2), jnp.swapaxes(b_im, -1, -2)
    bb_re = f_re[:, :, None, :] * bt_re - f_im[:, :, None, :] * bt_im
    bb_im = f_re[:, :, None, :] * bt_im + f_im[:, :, None, :] * bt_re

    def per_dir(x, fwd, bwd):
        return jnp.stack([fwd(x[0]), bwd(x[1])], axis=0)

    flip = lambda x: x[::-1]
    pr = per_dir(pw_re, lambda x: flip(x[:T]), lambda x: x[:T])[:, :, :, None, :]
    pi = per_dir(pw_im, lambda x: flip(x[:T]), lambda x: x[:T])[:, :, :, None, :]
    br, bi = bb_re[:, None], bb_im[:, None]
    fc = jnp.stack([pr * br - pi * bi, pr * bi + pi * br], axis=1)
    fc = fc.reshape(2, 2, T, S5_OCT, LANES, P)
    fc = jnp.concatenate([fc, fc], axis=-1)

    def lanes_gh(x):
        x = x.reshape(x.shape[:-2] + (S5_OCT, S5_GPO, P))
        return jnp.repeat(jnp.swapaxes(x, -1, -2), H, axis=-1)

    def c_lanes(c):
        c = c.reshape(2, S5_OCT, S5_GPO, H, P).transpose(0, 1, 4, 2, 3)
        return c.reshape(2, 1, S5_OCT, P, LANES)

    per = lanes_gh(per_dir(pw_re, lambda x: x[1:], lambda x: flip(x[1:])))
    pei = lanes_gh(per_dir(pw_im, lambda x: x[1:], lambda x: flip(x[1:])))
    cr, ci = c_lanes(c_re), c_lanes(c_im)
    ec = jnp.stack([cr * per - ci * pei, -(cr * pei + ci * per)], axis=1)
    ec = jnp.concatenate([ec, ec], axis=-2)

    kr, ki = pw_re[:, :T, :, None, :], pw_im[:, :T, :, None, :]
    car = c_re[:, None] * kr - c_im[:, None] * ki
    cai = c_re[:, None] * ki + c_im[:, None] * kr
    kd = jnp.sum(car[:, :, :, None] * bb_re[:, None, :, :, None]
                 - cai[:, :, :, None] * bb_im[:, None, :, :, None], axis=-1)
    kf, kb = kd[0], kd[1]
    klag = jnp.concatenate([kb[:0:-1], (kf[0] + kb[0])[None], kf[1:]], axis=0)
    kt = jnp.tile(klag.reshape(2 * T - 1, S5_OCT, LANES, H), (1, 1, 1, S5_GPO))

    at = jnp.stack([pw_re[:, T], pw_im[:, T]], axis=1)
    at = at.reshape(2, 2, S5_OCT, S5_GPO * P).transpose(0, 2, 1, 3)
    return kt.astype(BF16), fc.astype(BF16), ec.astype(BF16), at


def _s5_kernel(*refs, n_seq, n_chunk):
    T = S5_T
    (x_ref, kt_ref, fc_ref, ec_ref, at_ref, h0_ref, d_ref, out_ref, fin_ref,
     sf_ref, sb_ref, wt_ref, wf_ref, we_ref) = refs
    n_tile = S5_SW // LANES
    nh = n_tile // 2

    @pl.when(pl.program_id(1) == 0)
    def _():
        _s5_expand(kt_ref, fc_ref, ec_ref, wt_ref, wf_ref, we_ref)

    M = x_ref.shape[0]
    blocks = [slice(r, r + S5_ROWS) for r in range(0, M, S5_ROWS)]
    seq_pad = sf_ref.shape[1] // n_chunk

    def seq_rows(rows):
        return [(r // n_chunk, slice(r - rows.start, r - rows.start + n_chunk))
                for r in range(rows.start, rows.stop, n_chunk)]

    def to_tiles(ref, rows, val):
        for b, local in seq_rows(rows):
            for j in range(n_tile):
                ref[j, pl.ds(b, n_chunk, stride=seq_pad), :] = val[local, j * LANES:(j + 1) * LANES]

    def from_tiles(ref, rows):
        return jnp.concatenate(
            [jnp.concatenate([ref[j, pl.ds(b, n_chunk, stride=seq_pad), :].astype(BF16)
                              for j in range(n_tile)], axis=1)
             for b, _ in seq_rows(rows)], axis=0)

    for rows in blocks:
        xb = x_ref[rows, :].astype(BF16)
        to_tiles(sf_ref, rows, _dot(xb, wf_ref[0]))
        to_tiles(sb_ref, rows, _dot(xb, wf_ref[1]))

    def coeff(d):
        return [(at_ref[d, 0, 0:1, j * LANES:(j + 1) * LANES],
                 at_ref[d, 0, 1:2, j * LANES:(j + 1) * LANES]) for j in range(nh)]

    def advance(ref, rows, a, state):
        out = []
        for j in range(nh):
            (ar, ai), (hr, hi) = a[j], state[j]
            sr, si = ref[j, rows, :], ref[nh + j, rows, :]
            ref[j, rows, :] = hr
            ref[nh + j, rows, :] = hi
            out.append((ar * hr - ai * hi + sr, ar * hi + ai * hr + si))
        return out

    af, ab = coeff(0), coeff(1)

    def step(i, carry):
        hf, hb = carry
        fwd = pl.multiple_of(i * seq_pad, seq_pad)
        bwd = pl.multiple_of((n_chunk - 1 - i) * seq_pad, seq_pad)
        hf = advance(sf_ref, pl.ds(fwd, n_seq), af, hf)
        hb = advance(sb_ref, pl.ds(bwd, n_seq), ab, hb)
        return hf, hb

    def initial(d):
        return [(h0_ref[d, 0, 0, :, j * LANES:(j + 1) * LANES],
                 h0_ref[d, 0, 0, :, (nh + j) * LANES:(nh + j + 1) * LANES]) for j in range(nh)]

    finals = lax.fori_loop(0, n_chunk, step, (initial(0), initial(1)))
    for d in range(2):
        for j in range(nh):
            fin_ref[d, 0, 0, :, j * LANES:(j + 1) * LANES] = finals[d][j][0]
            fin_ref[d, 0, 0, :, (nh + j) * LANES:(nh + j + 1) * LANES] = finals[d][j][1]

    d = jnp.concatenate([d_ref[0]] * T, axis=1)
    for rows in blocks:
        x = x_ref[rows, :]
        y = (_dot(x.astype(BF16), wt_ref[...]) + _dot(from_tiles(sf_ref, rows), we_ref[0])
             + _dot(from_tiles(sb_ref, rows), we_ref[1]))
        out_ref[rows, :] = jax.nn.gelu(y + d * x)


def _s5_expand(kt_ref, fc_ref, ec_ref, wt_ref, wf_ref, we_ref):
    T = S5_T
    nh = S5_SW // LANES // 2
    row = lax.broadcasted_iota(jnp.int32, (LANES, LANES), 0)
    lane = lax.broadcasted_iota(jnp.int32, (LANES, LANES), 1)
    own_k = (row // S5_CH) == (lane // S5_CH)
    lag_tiles = [jnp.where(own_k, kt_ref[l, 0].astype(F32), 0.0).astype(BF16)
                 for l in range(2 * T - 1)]
    for s in range(T):
        for t in range(T):
            wt_ref[s * LANES:(s + 1) * LANES, t * LANES:(t + 1) * LANES] = lag_tiles[t - s + T - 1]
    gpt = LANES // S5_STATE
    own_f = [(row // S5_CH) == (gpt * j + lane // S5_STATE) for j in range(nh)]
    own_e = [(lane // S5_CH) == (gpt * j + row // S5_STATE) for j in range(nh)]
    for d in range(2):
        for ri in range(2):
            for s in range(T):
                f, e = fc_ref[d, ri, s, 0].astype(F32), ec_ref[d, ri, s, 0].astype(F32)
                rows = slice(s * LANES, (s + 1) * LANES)
                for j in range(nh):
                    cols = slice((ri * nh + j) * LANES, (ri * nh + j + 1) * LANES)
                    wf_ref[d, rows, cols] = jnp.where(own_f[j], f, 0.0).astype(BF16)
                    we_ref[d, cols, rows] = jnp.where(own_e[j], e, 0.0).astype(BF16)


def _s5_mixer(u, weights, h0, d_skip, n_seq, n_split):
    T = S5_T
    M_all = u.shape[0]
    M = M_all // n_split
    B = n_seq // n_split
    n_chunk = M // B
    seq_pad = -(-B // SUBLANES) * SUBLANES
    kt, fc, ec, at = weights
    TL = T * LANES
    x_spec = pl.BlockSpec((M, TL), lambda o, j: (j, o))
    tile_spec = pl.BlockSpec((2, 2, T, 1, LANES, LANES), lambda o, j: (0, 0, 0, o, 0, 0))
    st_spec = pl.BlockSpec((2, 1, 1, B, S5_SW), lambda o, j: (0, o, j, 0, 0))
    g, fin = pl.pallas_call(
        functools.partial(_s5_kernel, n_seq=B, n_chunk=n_chunk),
        grid=(S5_OCT, n_split),
        in_specs=[x_spec,
                  pl.BlockSpec((2 * T - 1, 1, LANES, LANES), lambda o, j: (0, o, 0, 0)),
                  tile_spec, tile_spec,
                  pl.BlockSpec((2, 1, 2, S5_SW // 2), lambda o, j: (0, o, 0, 0)),
                  st_spec,
                  pl.BlockSpec((1, 1, LANES), lambda o, j: (o, 0, 0))],
        out_specs=[x_spec, st_spec],
        out_shape=[jax.ShapeDtypeStruct((M_all, S5_OCT * TL), F32),
                   jax.ShapeDtypeStruct((2, S5_OCT, n_split, B, S5_SW), F32)],
        scratch_shapes=[pltpu.VMEM((S5_SW // LANES, n_chunk * seq_pad, LANES), F32)] * 2 + [
            pltpu.VMEM((TL, TL), BF16), pltpu.VMEM((2, TL, S5_SW), BF16),
            pltpu.VMEM((2, S5_SW, TL), BF16)],
        compiler_params=_params(2),
        name="s5_mixer",
    )(u, kt, fc, ec, at, h0, d_skip.reshape(S5_OCT, 1, LANES))
    return g, fin


def _mlp(h, w1_ref, w2_ref):
    hb = h.astype(BF16)
    acc = None
    for j in range(FF_HIDDEN // FF_CHUNK):
        a = _dot(hb, w1_ref[:, j * FF_CHUNK:(j + 1) * FF_CHUNK])
        a = jnp.square(jnp.maximum(a, 0.0)).astype(BF16)
        p = _dot(a, w2_ref[j * FF_CHUNK:(j + 1) * FF_CHUNK, :])
        acc = p if acc is None else acc + p
    return acc


def _post0_kernel(x_ref, attn_ref, g_ref, mod_ref, ng_ref, wglu_ref, wout_ref, w1_ref, w2_ref, o_ref,
                  gbuf_ref, *, sub):
    bb, tm = x_ref.shape[0], x_ref.shape[1]
    n_sub = tm // sub
    cps = sub // S5_T
    for k in range(bb * n_sub):
        bl, rows = k // n_sub, slice((k % n_sub) * sub, (k % n_sub + 1) * sub)
        for o in range(S5_OCT):
            for s in range(S5_T):
                c0 = (o * S5_T + s) * LANES
                gbuf_ref[k, o, pl.ds(s, cps, stride=S5_T), :] = (
                    g_ref[k * cps:(k + 1) * cps, c0:c0 + LANES])
        g = jnp.concatenate([gbuf_ref[k, o] for o in range(S5_OCT)], axis=1)
        s5 = g * jax.nn.sigmoid(_dot(g.astype(BF16), wglu_ref[...]))
        oc = (_dot(attn_ref[bl, rows], wout_ref[0:DA_WIDTH, :])
              + _dot(s5.astype(BF16), wout_ref[DA_WIDTH:D_MODEL, :]))
        x1 = x_ref[bl, rows] + mod_ref[0, 2:3, :] * oc
        h = _rms(x1, ng_ref[...]) * (1.0 + mod_ref[0, 4:5, :]) + mod_ref[0, 3:4, :]
        o_ref[bl, rows] = x1 + mod_ref[0, 5:6, :] * _mlp(h, w1_ref, w2_ref)


def _post0(x, attn, g, mods, mod_row, norm_g, w_glu, w_out, ff, layer, tiling):
    B, L, _ = x.shape
    bb, tm, sub = tiling
    w1, w2 = ff
    row_spec = lambda w: pl.BlockSpec((bb, tm, w), lambda b, i: (b, i, 0))
    return pl.pallas_call(
        functools.partial(_post0_kernel, sub=sub),
        grid=(B // bb, L // tm),
        in_specs=[row_spec(D_MODEL), row_spec(DA_WIDTH), _chunk_spec(L, bb, tm),
                  pl.BlockSpec((1, N_MOD, D_MODEL), lambda b, i: (mod_row(b), 0, 0)),
                  _const_spec((1, D_MODEL)),
                  _const_spec((S5_WIDTH, S5_WIDTH)),
                  _const_spec((D_MODEL, D_MODEL)),
                  _layer_spec((D_MODEL, FF_HIDDEN), layer),
                  _layer_spec((FF_HIDDEN, D_MODEL), layer)],
        out_specs=row_spec(D_MODEL),
        out_shape=jax.ShapeDtypeStruct((B, L, D_MODEL), F32),
        scratch_shapes=[pltpu.VMEM((bb * tm // sub, S5_OCT, sub, LANES), F32)],
        compiler_params=_params(2),
        name="mix_out_mlp",
    )(x, attn, g, mods, norm_g.reshape(1, D_MODEL), w_glu, w_out, w1, w2)


def _layer1_kernel(x_ref, xp_ref, xn_ref, mod_ref, ng1_ref, ng2_ref, pw_ref, ps_ref,
                   w1_ref, w2_ref, fg_ref, o_ref, *, sub, seq_len):
    bb, tm = x_ref.shape[0], x_ref.shape[1]
    n_sub = tm // sub
    H = POOL_HALO
    n = sub + 2 * H
    for k in range(bb * n_sub):
        bl, r0 = k // n_sub, (k % n_sub) * sub
        x = x_ref[bl, r0:r0 + sub]
        top = xp_ref[bl] if r0 == 0 else x_ref[bl, r0 - H:r0]
        bot = xn_ref[bl] if r0 + sub == tm else x_ref[bl, r0 + sub:r0 + sub + H]
        xe = jnp.concatenate([top, x, bot], axis=0)
        he = _rms(xe, ng1_ref[...]) * (1.0 + mod_ref[0, 1:2, :]) + mod_ref[0, 0:1, :]
        first = pl.program_id(1) * tm + r0
        pos_e = first - H + lax.broadcasted_iota(jnp.int32, (n, 1), 0)
        he = jnp.where((pos_e >= 0) & (pos_e < seq_len), he, 0.0)
        pos = first + lax.broadcasted_iota(jnp.int32, (sub, 1), 0)
        zs = []
        for gi, w in enumerate(POOL_WINDOWS):
            hg = he[:, gi * POOL_CH:(gi + 1) * POOL_CH]
            acc = hg + pltpu.roll(hg, 1, axis=0)
            r = 1
            while 2 * r < w:
                acc = pltpu.roll(acc, r, axis=0) + pltpu.roll(acc, n - r, axis=0)
                r *= 2
            lo = jnp.clip(pos - w // 2, 0, seq_len)
            hi = jnp.clip(pos + w // 2, 0, seq_len)
            mean = acc[H:H + sub] / (hi - lo).astype(F32)
            z = mean - hg[H:H + sub]
            zs.append(_dot(z.astype(BF16), pw_ref[gi]))
        oc = jnp.concatenate(zs, axis=1) * ps_ref[...]
        x1 = x + mod_ref[0, 2:3, :] * oc
        h = _rms(x1, ng2_ref[...]) * (1.0 + mod_ref[0, 4:5, :]) + mod_ref[0, 3:4, :]
        x2 = x1 + mod_ref[0, 5:6, :] * _mlp(h, w1_ref, w2_ref)
        o_ref[bl, r0:r0 + sub] = _rms(x2, fg_ref[...])


def _layer1(x, mods, mod_row, norm_g, pool_w, pool_scale, ff, layer, final_g, tiling):
    B, L, _ = x.shape
    bb, tm, sub = tiling
    w1, w2 = ff
    H = POOL_HALO
    per = tm // H
    last = L // H - 1
    row_spec = pl.BlockSpec((bb, tm, D_MODEL), lambda b, i: (b, i, 0))
    return pl.pallas_call(
        functools.partial(_layer1_kernel, sub=sub, seq_len=L),
        grid=(B // bb, L // tm),
        in_specs=[row_spec,
                  pl.BlockSpec((bb, H, D_MODEL), lambda b, i: (b, jnp.maximum(i * per - 1, 0), 0)),
                  pl.BlockSpec((bb, H, D_MODEL), lambda b, i: (b, jnp.minimum((i + 1) * per, last), 0)),
                  pl.BlockSpec((1, N_MOD, D_MODEL), lambda b, i: (mod_row(b), 0, 0)),
                  _const_spec((1, D_MODEL)), _const_spec((1, D_MODEL)),
                  _const_spec((len(POOL_WINDOWS), POOL_CH, POOL_CH)),
                  _const_spec((1, D_MODEL)),
                  _layer_spec((D_MODEL, FF_HIDDEN), layer),
                  _layer_spec((FF_HIDDEN, D_MODEL), layer),
                  _const_spec((1, D_MODEL))],
        out_specs=row_spec,
        out_shape=jax.ShapeDtypeStruct((B, L, D_MODEL), F32),
        compiler_params=_params(2),
        name="pool_mlp_norm",
    )(x, x, x, mods, norm_g[0].reshape(1, D_MODEL), norm_g[1].reshape(1, D_MODEL),
      pool_w, pool_scale.reshape(1, D_MODEL), w1, w2, final_g.reshape(1, D_MODEL))


def kernel(x_prompt, x_sample, cache_k, cache_v, state_s5, c, c_ctx, mod_w, mod_b, norm_g, mix_w_in, mix_w_out, diff_lambda_qk, diff_subln_g, s5_lambda_re, s5_lambda_im, s5_log_dt, s5_b_re, s5_b_im, s5_c_re, s5_c_im, s5_d, s5_w_glu, pool_w, pool_scale, ff_w1, ff_w2, final_norm_g):
    B_ctx, L_ctx, _ = x_prompt.shape
    B_lat, L_lat, _ = x_sample.shape
    ctx_row = B_lat
    n_rows = 8
    cc = jnp.concatenate([c, c_ctx[None], jnp.zeros((n_rows - B_lat - 1, D_MODEL), F32)], axis=0)
    mods = _modulation(cc, mod_w, mod_b).reshape(DEPTH, n_rows, N_MOD, D_MODEL)
    lat_row = lambda b: b
    ctx_mod_row = lambda b: ctx_row

    j = 0
    lam_init = 0.8 - 0.6 * math.exp(-0.3 * 0)
    w_in = mix_w_in[j].astype(BF16)
    w_out = mix_w_out[j].astype(BF16)
    w_glu = s5_w_glu[j].astype(BF16)
    ff = (ff_w1.astype(BF16), ff_w2.astype(BF16))
    s5w = _s5_weights(s5_lambda_re[j], s5_lambda_im[j], s5_log_dt[j], s5_b_re[j], s5_b_im[j],
                      s5_c_re[j], s5_c_im[j])

    qc, kc, vc, uc, kc32, vc32 = _premix(x_prompt, mods[0], ctx_mod_row, norm_g[0, 0], w_in,
                                         None, 256, True)
    ql, kl, vl, ul = _premix(x_sample, mods[0], lat_row, norm_g[0, 0], w_in,
                             _rope_tables(L_lat), 512, False)

    ac = _attention(qc, kc, vc, None, diff_lambda_qk[j], diff_subln_g[j], lam_init, 256, DA_HEADS)
    P = cache_k.shape[2]
    cache = (cache_k[:, j].reshape(B_lat, P, DA_WIDTH), cache_v[:, j].reshape(B_lat, P, DA_WIDTH))
    al = _attention(ql, kl, vl, cache, diff_lambda_qk[j], diff_subln_g[j], lam_init, 1024, 1)

    lat_split = 1
    h0_ctx = jnp.zeros((2, S5_OCT, 1, B_ctx, S5_SW), F32)
    h0_lat = state_s5[:, j].reshape(B_lat, 2, S5_OCT, S5_GPO, S5_STATE, 2)
    h0_lat = h0_lat.transpose(1, 2, 0, 5, 3, 4).reshape(
        2, S5_OCT, lat_split, B_lat // lat_split, S5_SW)
    gc, fin_c = _s5_mixer(uc, s5w, h0_ctx, s5_d[j], B_ctx, 1)
    gl, _ = _s5_mixer(ul, s5w, h0_lat, s5_d[j], B_lat, lat_split)

    ctx = _post0(x_prompt, ac, gc, mods[0], ctx_mod_row, norm_g[0, 1], w_glu, w_out, ff, 0,
                 CTX_TILING)
    lat = _post0(x_sample, al, gl, mods[0], lat_row, norm_g[0, 1], w_glu, w_out, ff, 0,
                 LAT_TILING)

    pw = pool_w[0].astype(BF16)
    y_prompt = _layer1(ctx, mods[1], ctx_mod_row, norm_g[1], pw, pool_scale[0], ff, 1,
                       final_norm_g, CTX_TILING)
    y_sample = _layer1(lat, mods[1], lat_row, norm_g[1], pw, pool_scale[0], ff, 1,
                       final_norm_g, LAT_TILING)

    new_cache_k = kc32.reshape(B_ctx, 1, L_ctx, DA_HEADS, 2 * DA_QK)
    new_cache_v = vc32.reshape(B_ctx, 1, L_ctx, DA_HEADS, DA_V)
    st = fin_c.reshape(2, S5_OCT, B_ctx, 2, S5_GPO, S5_STATE)
    new_state = st.transpose(2, 0, 1, 4, 5, 3).reshape(B_ctx, 1, 2, S5_GROUPS, S5_STATE, 2)
    return (y_prompt, y_sample, new_cache_k, new_cache_v, new_state)
```

```python
import functools
import math

import jax
import jax.numpy as jnp
from jax import lax
from jax.experimental import pallas as pl
from jax.experimental.pallas import tpu as pltpu

D_MODEL = 1024
DEPTH = 2
GRID_W = 64
DA_HEADS = 4
DA_QK = 64
DA_V = 2 * DA_QK
DA_WIDTH = DA_HEADS * DA_V
S5_WIDTH = D_MODEL - DA_WIDTH
S5_CH = 16
S5_GROUPS = S5_WIDTH // S5_CH
S5_STATE = 64
IN_WIDTH = 3 * DA_WIDTH + S5_WIDTH
POOL_WINDOWS = (2, 4, 8, 16)
POOL_CH = D_MODEL // len(POOL_WINDOWS)
FF_HIDDEN = 4 * D_MODEL
N_MOD = 6
ROPE_BASE = 10000.0
EPS = 1e-6

SUBLANES = 8
LANES = 128
S5_T = 8
S5_OCT = S5_WIDTH // LANES
S5_GPO = LANES // S5_CH
S5_ROWS = 512
S5_SW = 2 * S5_GPO * S5_STATE
Q_SCALE = DA_QK ** -0.5 * math.log2(math.e)
CTX_TILING = (2, 256, 256)
LAT_TILING = (1, 1024, 512)
ATTN_SUB = 256
POOL_HALO = 16
FF_CHUNK = 1024
VMEM_LIMIT = 56 * 1024 * 1024

F32 = jnp.float32
BF16 = jnp.bfloat16


def _const_spec(shape):
    nd = len(shape)
    return pl.BlockSpec(shape, lambda *_: (0,) * nd, pipeline_mode=pl.Buffered(1))


def _layer_spec(shape, layer):
    return pl.BlockSpec((None,) + shape, lambda *_: (layer, 0, 0), pipeline_mode=pl.Buffered(1))


def _params(n_axes):
    return pltpu.CompilerParams(dimension_semantics=("arbitrary",) * n_axes,
                                vmem_limit_bytes=VMEM_LIMIT)


def _rms(x, g):
    return x * lax.rsqrt(jnp.mean(x * x, axis=-1, keepdims=True) + EPS) * g


def _dot(a, b):
    return jnp.dot(a, b, preferred_element_type=F32)


def _mod_kernel(c_ref, w_ref, b_ref, o_ref):
    c = c_ref[...]
    s = (c * jax.nn.sigmoid(c)).astype(BF16)
    o_ref[0] = _dot(s, w_ref[0].astype(BF16)) + b_ref[0]


def _modulation(cc, mod_w, mod_b):
    rows = cc.shape[0]
    tn = 1536
    n = N_MOD * D_MODEL
    return pl.pallas_call(
        _mod_kernel,
        grid=(DEPTH, n // tn),
        in_specs=[pl.BlockSpec((rows, D_MODEL), lambda l, j: (0, 0)),
                  pl.BlockSpec((1, D_MODEL, tn), lambda l, j: (l, 0, j)),
                  pl.BlockSpec((1, 1, tn), lambda l, j: (l, 0, j))],
        out_specs=pl.BlockSpec((1, rows, tn), lambda l, j: (l, 0, j)),
        out_shape=jax.ShapeDtypeStruct((DEPTH, rows, n), F32),
        compiler_params=_params(2),
        name="modulation",
    )(cc, mod_w, mod_b.reshape(DEPTH, 1, n))


def _premix_kernel(*refs, rope, cache_out):
    x_ref, mod_ref, g_ref, w_ref = refs[:4]
    refs = refs[4:]
    if rope:
        cos_ref, sa_ref, sb_ref = refs[:3]
        refs = refs[3:]
    q_ref, k_ref, v_ref, u_ref = refs[:4]
    x = x_ref[0]
    h = _rms(x, g_ref[...]) * (1.0 + mod_ref[0, 1:2, :]) + mod_ref[0, 0:1, :]
    z = _dot(h.astype(BF16), w_ref[...])
    n_slab = 2 * DA_HEADS

    def rot(s):
        if not rope:
            return s
        return (s * cos_ref[...] + pltpu.roll(s, LANES - 16, axis=1) * sa_ref[...]
                + pltpu.roll(s, 16, axis=1) * sb_ref[...])

    for j in range(n_slab):
        s = rot(z[:, j * LANES:(j + 1) * LANES])
        if j < DA_HEADS:
            q_ref[0, :, j * LANES:(j + 1) * LANES] = (s * Q_SCALE).astype(BF16)
        else:
            jj = j - DA_HEADS
            k_ref[0, :, jj * LANES:(jj + 1) * LANES] = s.astype(BF16)
    v = z[:, 2 * DA_WIDTH:3 * DA_WIDTH]
    v_ref[0] = v.astype(BF16)
    ubuf_ref = refs[-1]
    n_chunk = z.shape[0] // S5_T
    for o in range(S5_OCT):
        ubuf_ref[o] = z[:, 3 * DA_WIDTH + o * LANES:3 * DA_WIDTH + (o + 1) * LANES]
    for o in range(S5_OCT):
        for s in range(S5_T):
            c0 = (o * S5_T + s) * LANES
            u_ref[:, c0:c0 + LANES] = ubuf_ref[o, pl.ds(s, n_chunk, stride=S5_T), :]
    if cache_out:
        kf_ref, vf_ref = refs[4:6]
        kf_ref[0] = z[:, DA_WIDTH:2 * DA_WIDTH]
        vf_ref[0] = v


def _chunk_spec(L, bb, tm):
    assert bb == 1 or tm == L
    return pl.BlockSpec((bb * tm // S5_T, S5_WIDTH * S5_T), lambda b, i: (b * (L // tm) + i, 0))


def _premix(x, mods, mod_row, norm_g, w_in, rope_tabs, tm, cache_out):
    B, L, _ = x.shape
    rope = rope_tabs is not None
    row_spec = lambda w: pl.BlockSpec((1, tm, w), lambda b, i: (b, i, 0))
    in_specs = [row_spec(D_MODEL),
                pl.BlockSpec((1, N_MOD, D_MODEL), lambda b, i: (mod_row(b), 0, 0)),
                _const_spec((1, D_MODEL)),
                _const_spec((D_MODEL, IN_WIDTH))]
    args = [x, mods, norm_g.reshape(1, D_MODEL), w_in]
    if rope:
        in_specs += [pl.BlockSpec((tm, LANES), lambda b, i: (i, 0))] * 3
        args += list(rope_tabs)
    out_specs = [row_spec(DA_WIDTH)] * 3 + [_chunk_spec(L, 1, tm)]
    out_shape = [jax.ShapeDtypeStruct((B, L, DA_WIDTH), BF16)] * 3 + [
        jax.ShapeDtypeStruct((B * L // S5_T, S5_WIDTH * S5_T), F32)]
    if cache_out:
        out_specs += [row_spec(DA_WIDTH)] * 2
        out_shape += [jax.ShapeDtypeStruct((B, L, DA_WIDTH), F32)] * 2
    return pl.pallas_call(
        functools.partial(_premix_kernel, rope=rope, cache_out=cache_out),
        grid=(B, L // tm),
        in_specs=in_specs, out_specs=out_specs, out_shape=out_shape,
        scratch_shapes=[pltpu.VMEM((S5_OCT, tm, LANES), F32)],
        compiler_params=_params(2),
        name="premix_rope" if rope else "premix",
    )(*args)


def _rope_tables(L):
    rows = L // GRID_W
    row = jnp.repeat(jnp.arange(rows), GRID_W).astype(F32)
    col = jnp.tile(jnp.arange(GRID_W), rows).astype(F32)
    half = DA_QK // 2
    inv = 1.0 / (ROPE_BASE ** (jnp.arange(0, half, 2, dtype=F32) / half))
    ang_r, ang_c = row[:, None] * inv, col[:, None] * inv
    zero = jnp.zeros_like(ang_r)
    cos64 = jnp.concatenate([jnp.cos(ang_r)] * 2 + [jnp.cos(ang_c)] * 2, axis=1)
    sa64 = jnp.concatenate([-jnp.sin(ang_r), zero, -jnp.sin(ang_c), zero], axis=1)
    sb64 = jnp.concatenate([zero, jnp.sin(ang_r), zero, jnp.sin(ang_c)], axis=1)
    return tuple(jnp.tile(t, (1, LANES // DA_QK)) for t in (cos64, sa64, sb64))


def _attn_kernel(*refs, lam_init, has_cache):
    lqk_ref, g_ref, q_ref, k_ref, v_ref = refs[:5]
    if has_cache:
        ck_ref, cv_ref, o_ref = refs[5:8]
    else:
        o_ref = refs[5]
    lqk = lqk_ref[...]
    lam = (jnp.exp(jnp.sum(lqk[0:1] * lqk[1:2], axis=-1, keepdims=True))
           - jnp.exp(jnp.sum(lqk[2:3] * lqk[3:4], axis=-1, keepdims=True)) + lam_init)
    tq = q_ref.shape[1]
    lane = lax.broadcasted_iota(jnp.int32, (ATTN_SUB, LANES), 1)
    zero = jnp.zeros((ATTN_SUB, LANES), BF16)
    nt = (((1,), (1,)), ((), ()))
    for hh in range(q_ref.shape[2] // LANES):
        cols = slice(hh * LANES, (hh + 1) * LANES)
        mxu_sum = has_cache
        ones = lambda v: jnp.concatenate([v, jnp.ones_like(v)], axis=1) if mxu_sum else v
        segs = [(k_ref[0, :, cols], ones(v_ref[0, :, cols]))]
        if has_cache:
            segs.append((ck_ref[0, :, cols].astype(BF16), ones(cv_ref[0, :, cols].astype(BF16))))
        for r in range(tq // ATTN_SUB):
            rows = slice(r * ATTN_SUB, (r + 1) * ATTN_SUB)
            q = q_ref[0, rows, cols]
            outs = []
            for qm in (jnp.where(lane < DA_QK, q, zero), jnp.where(lane >= DA_QK, q, zero)):
                ss = [lax.dot_general(qm, k, nt, preferred_element_type=F32) for k, _ in segs]
                m = functools.reduce(jnp.maximum,
                                     [jnp.max(s, axis=-1, keepdims=True) for s in ss])
                es = [jnp.exp2(s - m) for s in ss]
                o = functools.reduce(jnp.add,
                                     [_dot(e.astype(BF16), v) for e, (_, v) in zip(es, segs)])
                if mxu_sum:
                    l = o[:, DA_V:DA_V + 1]
                else:
                    l = functools.reduce(jnp.add, [jnp.sum(e, axis=-1, keepdims=True) for e in es])
                outs.append(o[:, :DA_V] / l)
            o = outs[0] - lam * outs[1]
            o_ref[0, rows, cols] = (_rms(o, g_ref[...]) * (1.0 - lam_init)).astype(BF16)


def _attention(q, k, v, cache, lambda_qk, subln_g, lam_init, tq, heads):
    B, L, _ = q.shape
    has_cache = cache is not None
    q_spec = pl.BlockSpec((1, tq, heads * LANES), lambda b, h, i: (b, i, h))
    kv_spec = lambda n: pl.BlockSpec((1, n, heads * LANES), lambda b, h, i: (b, 0, h))
    in_specs = [_const_spec((4, DA_QK)), _const_spec((1, DA_V)), q_spec, kv_spec(L), kv_spec(L)]
    args = [lambda_qk, subln_g.reshape(1, DA_V), q, k, v]
    if has_cache:
        P = cache[0].shape[1]
        in_specs += [kv_spec(P), kv_spec(P)]
        args += list(cache)
    return pl.pallas_call(
        functools.partial(_attn_kernel, lam_init=lam_init, has_cache=has_cache),
        grid=(B, DA_HEADS // heads, L // tq),
        in_specs=in_specs, out_specs=q_spec,
        out_shape=jax.ShapeDtypeStruct((B, L, DA_WIDTH), BF16),
        compiler_params=_params(3),
        name="diff_attn_cache" if has_cache else "diff_attn",
    )(*args)


def _s5_weights(lam_re, lam_im, log_dt, b_re, b_im, c_re, c_im):
    T, G, P, H = S5_T, S5_GROUPS, S5_STATE, S5_CH
    dt = jnp.exp(log_dt)[..., None]
    lr, li = lam_re * dt, lam_im * dt
    kk = jnp.arange(T + 1, dtype=F32)[None, :, None, None]
    mag = jnp.exp(kk * lr[:, None])
    pw_re = mag * jnp.cos(kk * li[:, None])
    pw_im = mag * jnp.sin(kk * li[:, None])
    a_re, a_im = pw_re[:, 1], pw_im[:, 1]
    den = lam_re * lam_re + lam_im * lam_im
    f_re = ((a_re - 1.0) * lam_re + a_im * lam_im) / den
    f_im = (a_im * lam_re - (a_re - 1.0) * lam_im) / den
    bt_re, bt_im = jnp.swapaxes(b_re, -1, -2), jnp.swapaxes(b_im, -1, -2)
    bb_re = f_re[:, :, None, :] * bt_re - f_im[:, :, None, :] * bt_im
    bb_im = f_re[:, :, None, :] * bt_im + f_im[:, :, None, :] * bt_re

    def per_dir(x, fwd, bwd):
        return jnp.stack([fwd(x[0]), bwd(x[1])], axis=0)

    flip = lambda x: x[::-1]
    pr = per_dir(pw_re, lambda x: flip(x[:T]), lambda x: x[:T])[:, :, :, None, :]
    pi = per_dir(pw_im, lambda x: flip(x[:T]), lambda x: x[:T])[:, :, :, None, :]
    br, bi = bb_re[:, None], bb_im[:, None]
    fc = jnp.stack([pr * br - pi * bi, pr * bi + pi * br], axis=1)
    fc = fc.reshape(2, 2, T, S5_OCT, LANES, P)
    fc = jnp.concatenate([fc, fc], axis=-1)

    def lanes_gh(x):
        x = x.reshape(x.shape[:-2] + (S5_OCT, S5_GPO, P))
        return jnp.repeat(jnp.swapaxes(x, -1, -2), H, axis=-1)

    def c_lanes(c):
        c = c.reshape(2, S5_OCT, S5_GPO, H, P).transpose(0, 1, 4, 2, 3)
        return c.reshape(2, 1, S5_OCT, P, LANES)

    per = lanes_gh(per_dir(pw_re, lambda x: x[1:], lambda x: flip(x[1:])))
    pei = lanes_gh(per_dir(pw_im, lambda x: x[1:], lambda x: flip(x[1:])))
    cr, ci = c_lanes(c_re), c_lanes(c_im)
    ec = jnp.stack([cr * per - ci * pei, -(cr * pei + ci * per)], axis=1)
    ec = jnp.concatenate([ec, ec], axis=-2)

    kr, ki = pw_re[:, :T, :, :, None], pw_im[:, :T, :, :, None]
    ct_re, ct_im = jnp.swapaxes(c_re, -1, -2)[:, None], jnp.swapaxes(c_im, -1, -2)[:, None]
    car = jnp.tile(ct_re * kr - ct_im * ki, (1, 1, 1, 1, H))
    cai = jnp.tile(ct_re * ki + ct_im * kr, (1, 1, 1, 1, H))
    bl_re = jnp.repeat(jnp.swapaxes(bb_re, -1, -2), H, axis=-1)[:, None]
    bl_im = jnp.repeat(jnp.swapaxes(bb_im, -1, -2), H, axis=-1)[:, None]
    kd = jnp.sum(car * bl_re - cai * bl_im, axis=3).reshape(2, T, G, H, H)
    kf, kb = kd[0], kd[1]
    klag = jnp.concatenate([kb[:0:-1], (kf[0] + kb[0])[None], kf[1:]], axis=0)
    kt = jnp.tile(klag.reshape(2 * T - 1, S5_OCT, LANES, H), (1, 1, 1, S5_GPO))

    at = jnp.stack([pw_re[:, T], pw_im[:, T]], axis=1)
    at = at.reshape(2, 2, S5_OCT, S5_GPO * P).transpose(0, 2, 1, 3)
    return kt.astype(BF16), fc.astype(BF16), ec.astype(BF16), at


def _s5_kernel(*refs, n_seq, n_chunk):
    T = S5_T
    (x_ref, kt_ref, fc_ref, ec_ref, at_ref, h0_ref, d_ref, out_ref, fin_ref,
     sf_ref, sb_ref, wt_ref, wf_ref, we_ref) = refs
    n_tile = S5_SW // LANES
    nh = n_tile // 2

    @pl.when(pl.program_id(1) == 0)
    def _():
        _s5_expand(kt_ref, fc_ref, ec_ref, wt_ref, wf_ref, we_ref)

    M = x_ref.shape[0]
    blocks = [slice(r, r + S5_ROWS) for r in range(0, M, S5_ROWS)]
    seq_pad = sf_ref.shape[1] // n_chunk

    def seq_rows(rows):
        return [(r // n_chunk, slice(r - rows.start, r - rows.start + n_chunk))
                for r in range(rows.start, rows.stop, n_chunk)]

    def to_tiles(ref, rows, val):
        for b, local in seq_rows(rows):
            for j in range(n_tile):
                ref[j, pl.ds(b, n_chunk, stride=seq_pad), :] = val[local, j * LANES:(j + 1) * LANES]

    def from_tiles(ref, rows):
        return jnp.concatenate(
            [jnp.concatenate([ref[j, pl.ds(b, n_chunk, stride=seq_pad), :].astype(BF16)
                              for j in range(n_tile)], axis=1)
             for b, _ in seq_rows(rows)], axis=0)

    for rows in blocks:
        xb = x_ref[rows, :].astype(BF16)
        to_tiles(sf_ref, rows, _dot(xb, wf_ref[0]))
        to_tiles(sb_ref, rows, _dot(xb, wf_ref[1]))

    def coeff(d):
        return [(at_ref[d, 0, 0:1, j * LANES:(j + 1) * LANES],
                 at_ref[d, 0, 1:2, j * LANES:(j + 1) * LANES]) for j in range(nh)]

    def advance(ref, rows, a, state):
        out = []
        for j in range(nh):
            (ar, ai), (hr, hi) = a[j], state[j]
            sr, si = ref[j, rows, :], ref[nh + j, rows, :]
            ref[j, rows, :] = hr
            ref[nh + j, rows, :] = hi
            out.append((ar * hr - ai * hi + sr, ar * hi + ai * hr + si))
        return out

    af, ab = coeff(0), coeff(1)

    def step(i, carry):
        hf, hb = carry
        fwd = pl.multiple_of(i * seq_pad, seq_pad)
        bwd = pl.multiple_of((n_chunk - 1 - i) * seq_pad, seq_pad)
        hf = advance(sf_ref, pl.ds(fwd, n_seq), af, hf)
        hb = advance(sb_ref, pl.ds(bwd, n_seq), ab, hb)
        return hf, hb

    def initial(d):
        return [(h0_ref[d, 0, 0, :, j * LANES:(j + 1) * LANES],
                 h0_ref[d, 0, 0, :, (nh + j) * LANES:(nh + j + 1) * LANES]) for j in range(nh)]

    finals = lax.fori_loop(0, n_chunk, step, (initial(0), initial(1)))
    for d in range(2):
        for j in range(nh):
            fin_ref[d, 0, 0, :, j * LANES:(j + 1) * LANES] = finals[d][j][0]
            fin_ref[d, 0, 0, :, (nh + j) * LANES:(nh + j + 1) * LANES] = finals[d][j][1]

    d = jnp.concatenate([d_ref[0]] * T, axis=1)
    for rows in blocks:
        x = x_ref[rows, :]
        y = (_dot(x.astype(BF16), wt_ref[...]) + _dot(from_tiles(sf_ref, rows), we_ref[0])
             + _dot(from_tiles(sb_ref, rows), we_ref[1]))
        out_ref[rows, :] = jax.nn.gelu(y + d * x)


def _s5_expand(kt_ref, fc_ref, ec_ref, wt_ref, wf_ref, we_ref):
    T = S5_T
    nh = S5_SW // LANES // 2
    row = lax.broadcasted_iota(jnp.int32, (LANES, LANES), 0)
    lane = lax.broadcasted_iota(jnp.int32, (LANES, LANES), 1)
    own_k = (row // S5_CH) == (lane // S5_CH)
    lag_tiles = [jnp.where(own_k, kt_ref[l, 0].astype(F32), 0.0).astype(BF16)
                 for l in range(2 * T - 1)]
    for s in range(T):
        for t in range(T):
            wt_ref[s * LANES:(s + 1) * LANES, t * LANES:(t + 1) * LANES] = lag_tiles[t - s + T - 1]
    gpt = LANES // S5_STATE
    own_f = [(row // S5_CH) == (gpt * j + lane // S5_STATE) for j in range(nh)]
    own_e = [(lane // S5_CH) == (gpt * j + row // S5_STATE) for j in range(nh)]
    for d in range(2):
        for ri in range(2):
            for s in range(T):
                f, e = fc_ref[d, ri, s, 0].astype(F32), ec_ref[d, ri, s, 0].astype(F32)
                rows = slice(s * LANES, (s + 1) * LANES)
                for j in range(nh):
                    cols = slice((ri * nh + j) * LANES, (ri * nh + j + 1) * LANES)
                    wf_ref[d, rows, cols] = jnp.where(own_f[j], f, 0.0).astype(BF16)
                    we_ref[d, cols, rows] = jnp.where(own_e[j], e, 0.0).astype(BF16)


def _s5_mixer(u, weights, h0, d_skip, n_seq, n_split):
    T = S5_T
    M_all = u.shape[0]
    M = M_all // n_split
    B = n_seq // n_split
    n_chunk = M // B
    seq_pad = -(-B // SUBLANES) * SUBLANES
    kt, fc, ec, at = weights
    TL = T * LANES
    x_spec = pl.BlockSpec((M, TL), lambda o, j: (j, o))
    tile_spec = pl.BlockSpec((2, 2, T, 1, LANES, LANES), lambda o, j: (0, 0, 0, o, 0, 0))
    st_spec = pl.BlockSpec((2, 1, 1, B, S5_SW), lambda o, j: (0, o, j, 0, 0))
    g, fin = pl.pallas_call(
        functools.partial(_s5_kernel, n_seq=B, n_chunk=n_chunk),
        grid=(S5_OCT, n_split),
        in_specs=[x_spec,
                  pl.BlockSpec((2 * T - 1, 1, LANES, LANES), lambda o, j: (0, o, 0, 0)),
                  tile_spec, tile_spec,
                  pl.BlockSpec((2, 1, 2, S5_SW // 2), lambda o, j: (0, o, 0, 0)),
                  st_spec,
                  pl.BlockSpec((1, 1, LANES), lambda o, j: (o, 0, 0))],
        out_specs=[x_spec, st_spec],
        out_shape=[jax.ShapeDtypeStruct((M_all, S5_OCT * TL), F32),
                   jax.ShapeDtypeStruct((2, S5_OCT, n_split, B, S5_SW), F32)],
        scratch_shapes=[pltpu.VMEM((S5_SW // LANES, n_chunk * seq_pad, LANES), F32)] * 2 + [
            pltpu.VMEM((TL, TL), BF16), pltpu.VMEM((2, TL, S5_SW), BF16),
            pltpu.VMEM((2, S5_SW, TL), BF16)],
        compiler_params=_params(2),
        name="s5_mixer",
    )(u, kt, fc, ec, at, h0, d_skip.reshape(S5_OCT, 1, LANES))
    return g, fin


def _mlp(h, w1_ref, w2_ref):
    hb = h.astype(BF16)
    acc = None
    for j in range(FF_HIDDEN // FF_CHUNK):
        a = _dot(hb, w1_ref[:, j * FF_CHUNK:(j + 1) * FF_CHUNK])
        a = jnp.square(jnp.maximum(a, 0.0)).astype(BF16)
        p = _dot(a, w2_ref[j * FF_CHUNK:(j + 1) * FF_CHUNK, :])
        acc = p if acc is None else acc + p
    return acc


def _post0_kernel(x_ref, attn_ref, g_ref, mod_ref, ng_ref, wglu_ref, wout_ref, w1_ref, w2_ref, o_ref,
                  gbuf_ref, *, sub):
    bb, tm = x_ref.shape[0], x_ref.shape[1]
    n_sub = tm // sub
    cps = sub // S5_T
    for k in range(bb * n_sub):
        bl, rows = k // n_sub, slice((k % n_sub) * sub, (k % n_sub + 1) * sub)
        for o in range(S5_OCT):
            for s in range(S5_T):
                c0 = (o * S5_T + s) * LANES
                gbuf_ref[k, o, pl.ds(s, cps, stride=S5_T), :] = (
                    g_ref[k * cps:(k + 1) * cps, c0:c0 + LANES])
        g = jnp.concatenate([gbuf_ref[k, o] for o in range(S5_OCT)], axis=1)
        s5 = g * jax.nn.sigmoid(_dot(g.astype(BF16), wglu_ref[...]))
        oc = (_dot(attn_ref[bl, rows], wout_ref[0:DA_WIDTH, :])
              + _dot(s5.astype(BF16), wout_ref[DA_WIDTH:D_MODEL, :]))
        x1 = x_ref[bl, rows] + mod_ref[0, 2:3, :] * oc
        h = _rms(x1, ng_ref[...]) * (1.0 + mod_ref[0, 4:5, :]) + mod_ref[0, 3:4, :]
        o_ref[bl, rows] = x1 + mod_ref[0, 5:6, :] * _mlp(h, w1_ref, w2_ref)


def _post0(x, attn, g, mods, mod_row, norm_g, w_glu, w_out, ff, layer, tiling):
    B, L, _ = x.shape
    bb, tm, sub = tiling
    w1, w2 = ff
    row_spec = lambda w: pl.BlockSpec((bb, tm, w), lambda b, i: (b, i, 0))
    return pl.pallas_call(
        functools.partial(_post0_kernel, sub=sub),
        grid=(B // bb, L // tm),
        in_specs=[row_spec(D_MODEL), row_spec(DA_WIDTH), _chunk_spec(L, bb, tm),
                  pl.BlockSpec((1, N_MOD, D_MODEL), lambda b, i: (mod_row(b), 0, 0)),
                  _const_spec((1, D_MODEL)),
                  _const_spec((S5_WIDTH, S5_WIDTH)),
                  _const_spec((D_MODEL, D_MODEL)),
                  _layer_spec((D_MODEL, FF_HIDDEN), layer),
                  _layer_spec((FF_HIDDEN, D_MODEL), layer)],
        out_specs=row_spec(D_MODEL),
        out_shape=jax.ShapeDtypeStruct((B, L, D_MODEL), F32),
        scratch_shapes=[pltpu.VMEM((bb * tm // sub, S5_OCT, sub, LANES), F32)],
        compiler_params=_params(2),
        name="mix_out_mlp",
    )(x, attn, g, mods, norm_g.reshape(1, D_MODEL), w_glu, w_out, w1, w2)


def _layer1_kernel(x_ref, xp_ref, xn_ref, mod_ref, ng1_ref, ng2_ref, pw_ref, ps_ref,
                   w1_ref, w2_ref, fg_ref, o_ref, *, sub, seq_len):
    bb, tm = x_ref.shape[0], x_ref.shape[1]
    n_sub = tm // sub
    H = POOL_HALO
    n = sub + 2 * H
    for k in range(bb * n_sub):
        bl, r0 = k // n_sub, (k % n_sub) * sub
        x = x_ref[bl, r0:r0 + sub]
        top = xp_ref[bl] if r0 == 0 else x_ref[bl, r0 - H:r0]
        bot = xn_ref[bl] if r0 + sub == tm else x_ref[bl, r0 + sub:r0 + sub + H]
        xe = jnp.concatenate([top, x, bot], axis=0)
        he = _rms(xe, ng1_ref[...]) * (1.0 + mod_ref[0, 1:2, :]) + mod_ref[0, 0:1, :]
        first = pl.program_id(1) * tm + r0
        pos_e = first - H + lax.broadcasted_iota(jnp.int32, (n, 1), 0)
        he = jnp.where((pos_e >= 0) & (pos_e < seq_len), he, 0.0)
        pos = first + lax.broadcasted_iota(jnp.int32, (sub, 1), 0)
        zs = []
        for gi, w in enumerate(POOL_WINDOWS):
            hg = he[:, gi * POOL_CH:(gi + 1) * POOL_CH]
            half = w // 2
            fwd, k = hg, 1
            while 2 * k < w:
                fwd = fwd + pltpu.roll(fwd, n - k, axis=0)
                k *= 2
            if half % SUBLANES == 0:
                total = fwd[H - half:H - half + sub] + fwd[H:H + sub]
            else:
                total = (fwd + pltpu.roll(fwd, half, axis=0))[H:H + sub]
            count = jnp.clip(pos + half, 0, seq_len) - jnp.clip(pos - half, 0, seq_len)
            z = total * (1.0 / count.astype(F32)) - hg[H:H + sub]
            zs.append(_dot(z.astype(BF16), pw_ref[gi]))
        oc = jnp.concatenate(zs, axis=1) * ps_ref[...]
        x1 = x + mod_ref[0, 2:3, :] * oc
        h = _rms(x1, ng2_ref[...]) * (1.0 + mod_ref[0, 4:5, :]) + mod_ref[0, 3:4, :]
        x2 = x1 + mod_ref[0, 5:6, :] * _mlp(h, w1_ref, w2_ref)
        o_ref[bl, r0:r0 + sub] = _rms(x2, fg_ref[...])


def _layer1(x, mods, mod_row, norm_g, pool_w, pool_scale, ff, layer, final_g, tiling):
    B, L, _ = x.shape
    bb, tm, sub = tiling
    w1, w2 = ff
    H = POOL_HALO
    per = tm // H
    last = L // H - 1
    row_spec = pl.BlockSpec((bb, tm, D_MODEL), lambda b, i: (b, i, 0))
    return pl.pallas_call(
        functools.partial(_layer1_kernel, sub=sub, seq_len=L),
        grid=(B // bb, L // tm),
        in_specs=[row_spec,
                  pl.BlockSpec((bb, H, D_MODEL), lambda b, i: (b, jnp.maximum(i * per - 1, 0), 0)),
                  pl.BlockSpec((bb, H, D_MODEL), lambda b, i: (b, jnp.minimum((i + 1) * per, last), 0)),
                  pl.BlockSpec((1, N_MOD, D_MODEL), lambda b, i: (mod_row(b), 0, 0)),
                  _const_spec((1, D_MODEL)), _const_spec((1, D_MODEL)),
                  _const_spec((len(POOL_WINDOWS), POOL_CH, POOL_CH)),
                  _const_spec((1, D_MODEL)),
                  _layer_spec((D_MODEL, FF_HIDDEN), layer),
                  _layer_spec((FF_HIDDEN, D_MODEL), layer),
                  _const_spec((1, D_MODEL))],
        out_specs=row_spec,
        out_shape=jax.ShapeDtypeStruct((B, L, D_MODEL), F32),
        compiler_params=_params(2),
        name="pool_mlp_norm",
    )(x, x, x, mods, norm_g[0].reshape(1, D_MODEL), norm_g[1].reshape(1, D_MODEL),
      pool_w, pool_scale.reshape(1, D_MODEL), w1, w2, final_g.reshape(1, D_MODEL))


def kernel(x_prompt, x_sample, cache_k, cache_v, state_s5, c, c_ctx, mod_w, mod_b, norm_g, mix_w_in, mix_w_out, diff_lambda_qk, diff_subln_g, s5_lambda_re, s5_lambda_im, s5_log_dt, s5_b_re, s5_b_im, s5_c_re, s5_c_im, s5_d, s5_w_glu, pool_w, pool_scale, ff_w1, ff_w2, final_norm_g):
    B_ctx, L_ctx, _ = x_prompt.shape
    B_lat, L_lat, _ = x_sample.shape
    ctx_row = B_lat
    n_rows = 8
    cc = jnp.concatenate([c, c_ctx[None], jnp.zeros((n_rows - B_lat - 1, D_MODEL), F32)], axis=0)
    mods = _modulation(cc, mod_w, mod_b).reshape(DEPTH, n_rows, N_MOD, D_MODEL)
    lat_row = lambda b: b
    ctx_mod_row = lambda b: ctx_row

    j = 0
    lam_init = 0.8 - 0.6 * math.exp(-0.3 * 0)
    w_in = mix_w_in[j].astype(BF16)
    w_out = mix_w_out[j].astype(BF16)
    w_glu = s5_w_glu[j].astype(BF16)
    ff = (ff_w1.astype(BF16), ff_w2.astype(BF16))
    s5w = _s5_weights(s5_lambda_re[j], s5_lambda_im[j], s5_log_dt[j], s5_b_re[j], s5_b_im[j],
                      s5_c_re[j], s5_c_im[j])

    qc, kc, vc, uc, kc32, vc32 = _premix(x_prompt, mods[0], ctx_mod_row, norm_g[0, 0], w_in,
                                         None, 256, True)
    ql, kl, vl, ul = _premix(x_sample, mods[0], lat_row, norm_g[0, 0], w_in,
                             _rope_tables(L_lat), 512, False)

    ac = _attention(qc, kc, vc, None, diff_lambda_qk[j], diff_subln_g[j], lam_init, 256, DA_HEADS)
    P = cache_k.shape[2]
    cache = (cache_k[:, j].reshape(B_lat, P, DA_WIDTH), cache_v[:, j].reshape(B_lat, P, DA_WIDTH))
    al = _attention(ql, kl, vl, cache, diff_lambda_qk[j], diff_subln_g[j], lam_init, 1024, 1)

    lat_split = 1
    h0_ctx = jnp.zeros((2, S5_OCT, 1, B_ctx, S5_SW), F32)
    h0_lat = state_s5[:, j].reshape(B_lat, 2, S5_OCT, S5_GPO, S5_STATE, 2)
    h0_lat = h0_lat.transpose(1, 2, 0, 5, 3, 4).reshape(
        2, S5_OCT, lat_split, B_lat // lat_split, S5_SW)
    gc, fin_c = _s5_mixer(uc, s5w, h0_ctx, s5_d[j], B_ctx, 1)
    gl, _ = _s5_mixer(ul, s5w, h0_lat, s5_d[j], B_lat, lat_split)

    ctx = _post0(x_prompt, ac, gc, mods[0], ctx_mod_row, norm_g[0, 1], w_glu, w_out, ff, 0,
                 CTX_TILING)
    lat = _post0(x_sample, al, gl, mods[0], lat_row, norm_g[0, 1], w_glu, w_out, ff, 0,
                 LAT_TILING)

    pw = pool_w[0].astype(BF16)
    y_prompt = _layer1(ctx, mods[1], ctx_mod_row, norm_g[1], pw, pool_scale[0], ff, 1,
                       final_norm_g, CTX_TILING)
    y_sample = _layer1(lat, mods[1], lat_row, norm_g[1], pw, pool_scale[0], ff, 1,
                       final_norm_g, LAT_TILING)

    new_cache_k = kc32.reshape(B_ctx, 1, L_ctx, DA_HEADS, 2 * DA_QK)
    new_cache_v = vc32.reshape(B_ctx, 1, L_ctx, DA_HEADS, DA_V)
    st = fin_c.reshape(2, S5_OCT, B_ctx, 2, S5_GPO, S5_STATE)
    new_state = st.transpose(2, 0, 1, 4, 5, 3).reshape(B_ctx, 1, 2, S5_GROUPS, S5_STATE, 2)
    return (y_prompt, y_sample, new_cache_k, new_cache_v, new_state)
```

```python
import functools
import math

import jax
import jax.numpy as jnp
from jax import lax
from jax.experimental import pallas as pl
from jax.experimental.pallas import tpu as pltpu

D_MODEL = 1024
DEPTH = 2
GRID_W = 64
DA_HEADS = 4
DA_QK = 64
DA_V = 2 * DA_QK
DA_WIDTH = DA_HEADS * DA_V
S5_WIDTH = D_MODEL - DA_WIDTH
S5_CH = 16
S5_GROUPS = S5_WIDTH // S5_CH
S5_STATE = 64
IN_WIDTH = 3 * DA_WIDTH + S5_WIDTH
POOL_WINDOWS = (2, 4, 8, 16)
POOL_CH = D_MODEL // len(POOL_WINDOWS)
FF_HIDDEN = 4 * D_MODEL
N_MOD = 6
ROPE_BASE = 10000.0
EPS = 1e-6

SUBLANES = 8
LANES = 128
S5_T = 8
S5_OCT = S5_WIDTH // LANES
S5_GPO = LANES // S5_CH
S5_ROWS = 512
S5_SW = 2 * S5_GPO * S5_STATE
Q_SCALE = DA_QK ** -0.5 * math.log2(math.e)
CTX_TILING = (2, 256, 256)
LAT_TILING = (1, 1024, 512)
CTX_POOL_TILING = (2, 256)
LAT_POOL_TILING = (1, 512)
ATTN_SUB = 256
POOL_HALO = 16
FF_CHUNK = 1024
VMEM_LIMIT = 56 * 1024 * 1024

F32 = jnp.float32
BF16 = jnp.bfloat16


def _const_spec(shape):
    nd = len(shape)
    return pl.BlockSpec(shape, lambda *_: (0,) * nd, pipeline_mode=pl.Buffered(1))


def _layer_spec(shape, layer):
    return pl.BlockSpec((None,) + shape, lambda *_: (layer, 0, 0), pipeline_mode=pl.Buffered(1))


def _params(n_axes):
    return pltpu.CompilerParams(dimension_semantics=("arbitrary",) * n_axes,
                                vmem_limit_bytes=VMEM_LIMIT)


def _rms(x, g):
    return x * lax.rsqrt(jnp.mean(x * x, axis=-1, keepdims=True) + EPS) * g


def _dot(a, b):
    return jnp.dot(a, b, preferred_element_type=F32)


def _mod_kernel(c_ref, w_ref, b_ref, o_ref):
    c = c_ref[...]
    s = (c * jax.nn.sigmoid(c)).astype(BF16)
    o_ref[0] = _dot(s, w_ref[0].astype(BF16)) + b_ref[0]


def _modulation(cc, mod_w, mod_b):
    rows = cc.shape[0]
    tn = 1536
    n = N_MOD * D_MODEL
    return pl.pallas_call(
        _mod_kernel,
        grid=(DEPTH, n // tn),
        in_specs=[pl.BlockSpec((rows, D_MODEL), lambda l, j: (0, 0)),
                  pl.BlockSpec((1, D_MODEL, tn), lambda l, j: (l, 0, j)),
                  pl.BlockSpec((1, 1, tn), lambda l, j: (l, 0, j))],
        out_specs=pl.BlockSpec((1, rows, tn), lambda l, j: (l, 0, j)),
        out_shape=jax.ShapeDtypeStruct((DEPTH, rows, n), F32),
        compiler_params=_params(2),
        name="modulation",
    )(cc, mod_w, mod_b.reshape(DEPTH, 1, n))


def _premix_kernel(*refs, rope, cache_out):
    x_ref, mod_ref, g_ref, w_ref = refs[:4]
    refs = refs[4:]
    if rope:
        cos_ref, sa_ref, sb_ref = refs[:3]
        refs = refs[3:]
    q_ref, k_ref, v_ref, u_ref = refs[:4]
    x = x_ref[0]
    h = _rms(x, g_ref[...]) * (1.0 + mod_ref[0, 1:2, :]) + mod_ref[0, 0:1, :]
    z = _dot(h.astype(BF16), w_ref[...])
    n_slab = 2 * DA_HEADS

    def rot(s):
        if not rope:
            return s
        return (s * cos_ref[...] + pltpu.roll(s, LANES - 16, axis=1) * sa_ref[...]
                + pltpu.roll(s, 16, axis=1) * sb_ref[...])

    for j in range(n_slab):
        s = rot(z[:, j * LANES:(j + 1) * LANES])
        if j < DA_HEADS:
            q_ref[0, :, j * LANES:(j + 1) * LANES] = (s * Q_SCALE).astype(BF16)
        else:
            jj = j - DA_HEADS
            k_ref[0, :, jj * LANES:(jj + 1) * LANES] = s.astype(BF16)
    v = z[:, 2 * DA_WIDTH:3 * DA_WIDTH]
    v_ref[0] = v.astype(BF16)
    ubuf_ref = refs[-1]
    n_chunk = z.shape[0] // S5_T
    for o in range(S5_OCT):
        ubuf_ref[o] = z[:, 3 * DA_WIDTH + o * LANES:3 * DA_WIDTH + (o + 1) * LANES]
    for o in range(S5_OCT):
        for s in range(S5_T):
            c0 = (o * S5_T + s) * LANES
            u_ref[:, c0:c0 + LANES] = ubuf_ref[o, pl.ds(s, n_chunk, stride=S5_T), :]
    if cache_out:
        kf_ref, vf_ref = refs[4:6]
        kf_ref[0] = z[:, DA_WIDTH:2 * DA_WIDTH]
        vf_ref[0] = v


def _chunk_spec(L, bb, tm):
    assert bb == 1 or tm == L
    return pl.BlockSpec((bb * tm // S5_T, S5_WIDTH * S5_T), lambda b, i: (b * (L // tm) + i, 0))


def _premix(x, mods, mod_row, norm_g, w_in, rope_tabs, tm, cache_out):
    B, L, _ = x.shape
    rope = rope_tabs is not None
    row_spec = lambda w: pl.BlockSpec((1, tm, w), lambda b, i: (b, i, 0))
    in_specs = [row_spec(D_MODEL),
                pl.BlockSpec((1, N_MOD, D_MODEL), lambda b, i: (mod_row(b), 0, 0)),
                _const_spec((1, D_MODEL)),
                _const_spec((D_MODEL, IN_WIDTH))]
    args = [x, mods, norm_g.reshape(1, D_MODEL), w_in]
    if rope:
        in_specs += [pl.BlockSpec((tm, LANES), lambda b, i: (i, 0))] * 3
        args += list(rope_tabs)
    out_specs = [row_spec(DA_WIDTH)] * 3 + [_chunk_spec(L, 1, tm)]
    out_shape = [jax.ShapeDtypeStruct((B, L, DA_WIDTH), BF16)] * 3 + [
        jax.ShapeDtypeStruct((B * L // S5_T, S5_WIDTH * S5_T), F32)]
    if cache_out:
        out_specs += [row_spec(DA_WIDTH)] * 2
        out_shape += [jax.ShapeDtypeStruct((B, L, DA_WIDTH), F32)] * 2
    return pl.pallas_call(
        functools.partial(_premix_kernel, rope=rope, cache_out=cache_out),
        grid=(B, L // tm),
        in_specs=in_specs, out_specs=out_specs, out_shape=out_shape,
        scratch_shapes=[pltpu.VMEM((S5_OCT, tm, LANES), F32)],
        compiler_params=_params(2),
        name="premix_rope" if rope else "premix",
    )(*args)


def _rope_tables(L):
    rows = L // GRID_W
    row = jnp.repeat(jnp.arange(rows), GRID_W).astype(F32)
    col = jnp.tile(jnp.arange(GRID_W), rows).astype(F32)
    half = DA_QK // 2
    inv = 1.0 / (ROPE_BASE ** (jnp.arange(0, half, 2, dtype=F32) / half))
    ang_r, ang_c = row[:, None] * inv, col[:, None] * inv
    zero = jnp.zeros_like(ang_r)
    cos64 = jnp.concatenate([jnp.cos(ang_r)] * 2 + [jnp.cos(ang_c)] * 2, axis=1)
    sa64 = jnp.concatenate([-jnp.sin(ang_r), zero, -jnp.sin(ang_c), zero], axis=1)
    sb64 = jnp.concatenate([zero, jnp.sin(ang_r), zero, jnp.sin(ang_c)], axis=1)
    return tuple(jnp.tile(t, (1, LANES // DA_QK)) for t in (cos64, sa64, sb64))


def _attn_kernel(*refs, lam_init, has_cache):
    lqk_ref, g_ref, q_ref, k_ref, v_ref = refs[:5]
    if has_cache:
        ck_ref, cv_ref, o_ref = refs[5:8]
    else:
        o_ref = refs[5]
    lqk = lqk_ref[...]
    lam = (jnp.exp(jnp.sum(lqk[0:1] * lqk[1:2], axis=-1, keepdims=True))
           - jnp.exp(jnp.sum(lqk[2:3] * lqk[3:4], axis=-1, keepdims=True)) + lam_init)
    tq = q_ref.shape[1]
    lane = lax.broadcasted_iota(jnp.int32, (ATTN_SUB, LANES), 1)
    zero = jnp.zeros((ATTN_SUB, LANES), BF16)
    nt = (((1,), (1,)), ((), ()))
    for hh in range(q_ref.shape[2] // LANES):
        cols = slice(hh * LANES, (hh + 1) * LANES)
        mxu_sum = has_cache
        ones = lambda v: jnp.concatenate([v, jnp.ones_like(v)], axis=1) if mxu_sum else v
        segs = [(k_ref[0, :, cols], ones(v_ref[0, :, cols]))]
        if has_cache:
            segs.append((ck_ref[0, :, cols].astype(BF16), ones(cv_ref[0, :, cols].astype(BF16))))
        for r in range(tq // ATTN_SUB):
            rows = slice(r * ATTN_SUB, (r + 1) * ATTN_SUB)
            q = q_ref[0, rows, cols]
            outs = []
            for qm in (jnp.where(lane < DA_QK, q, zero), jnp.where(lane >= DA_QK, q, zero)):
                ss = [lax.dot_general(qm, k, nt, preferred_element_type=F32) for k, _ in segs]
                m = functools.reduce(jnp.maximum,
                                     [jnp.max(s, axis=-1, keepdims=True) for s in ss])
                es = [jnp.exp2(s - m) for s in ss]
                o = functools.reduce(jnp.add,
                                     [_dot(e.astype(BF16), v) for e, (_, v) in zip(es, segs)])
                if mxu_sum:
                    l = o[:, DA_V:DA_V + 1]
                else:
                    l = functools.reduce(jnp.add, [jnp.sum(e, axis=-1, keepdims=True) for e in es])
                outs.append(o[:, :DA_V] / l)
            o = outs[0] - lam * outs[1]
            o_ref[0, rows, cols] = (_rms(o, g_ref[...]) * (1.0 - lam_init)).astype(BF16)


def _attention(q, k, v, cache, lambda_qk, subln_g, lam_init, tq, heads):
    B, L, _ = q.shape
    has_cache = cache is not None
    q_spec = pl.BlockSpec((1, tq, heads * LANES), lambda b, h, i: (b, i, h))
    kv_spec = lambda n: pl.BlockSpec((1, n, heads * LANES), lambda b, h, i: (b, 0, h))
    in_specs = [_const_spec((4, DA_QK)), _const_spec((1, DA_V)), q_spec, kv_spec(L), kv_spec(L)]
    args = [lambda_qk, subln_g.reshape(1, DA_V), q, k, v]
    if has_cache:
        P = cache[0].shape[1]
        in_specs += [kv_spec(P), kv_spec(P)]
        args += list(cache)
    return pl.pallas_call(
        functools.partial(_attn_kernel, lam_init=lam_init, has_cache=has_cache),
        grid=(B, DA_HEADS // heads, L // tq),
        in_specs=in_specs, out_specs=q_spec,
        out_shape=jax.ShapeDtypeStruct((B, L, DA_WIDTH), BF16),
        compiler_params=_params(3),
        name="diff_attn_cache" if has_cache else "diff_attn",
    )(*args)


def _s5_weights(lam_re, lam_im, log_dt, b_re, b_im, c_re, c_im):
    T, G, P, H = S5_T, S5_GROUPS, S5_STATE, S5_CH
    dt = jnp.exp(log_dt)[..., None]
    lr, li = lam_re * dt, lam_im * dt
    kk = jnp.arange(T + 1, dtype=F32)[None, :, None, None]
    mag = jnp.exp(kk * lr[:, None])
    pw_re = mag * jnp.cos(kk * li[:, None])
    pw_im = mag * jnp.sin(kk * li[:, None])
    a_re, a_im = pw_re[:, 1], pw_im[:, 1]
    den = lam_re * lam_re + lam_im * lam_im
    f_re = ((a_re - 1.0) * lam_re + a_im * lam_im) / den
    f_im = (a_im * lam_re - (a_re - 1.0) * lam_im) / den
    bt_re, bt_im = jnp.swapaxes(b_re, -1, -2), jnp.swapaxes(b_im, -1, -2)
    bb_re = f_re[:, :, None, :] * bt_re - f_im[:, :, None, :] * bt_im
    bb_im = f_re[:, :, None, :] * bt_im + f_im[:, :, None, :] * bt_re

    def per_dir(x, fwd, bwd):
        return jnp.stack([fwd(x[0]), bwd(x[1])], axis=0)

    flip = lambda x: x[::-1]
    pr = per_dir(pw_re, lambda x: flip(x[:T]), lambda x: x[:T])[:, :, :, None, :]
    pi = per_dir(pw_im, lambda x: flip(x[:T]), lambda x: x[:T])[:, :, :, None, :]
    br, bi = bb_re[:, None], bb_im[:, None]
    fc = jnp.stack([pr * br - pi * bi, pr * bi + pi * br], axis=1)
    fc = fc.reshape(2, 2, T, S5_OCT, LANES, P)
    fc = jnp.concatenate([fc, fc], axis=-1)

    def lanes_gh(x):
        x = x.reshape(x.shape[:-2] + (S5_OCT, S5_GPO, P))
        return jnp.repeat(jnp.swapaxes(x, -1, -2), H, axis=-1)

    def c_lanes(c):
        c = c.reshape(2, S5_OCT, S5_GPO, H, P).transpose(0, 1, 4, 2, 3)
        return c.reshape(2, 1, S5_OCT, P, LANES)

    per = lanes_gh(per_dir(pw_re, lambda x: x[1:], lambda x: flip(x[1:])))
    pei = lanes_gh(per_dir(pw_im, lambda x: x[1:], lambda x: flip(x[1:])))
    cr, ci = c_lanes(c_re), c_lanes(c_im)
    ec = jnp.stack([cr * per - ci * pei, -(cr * pei + ci * per)], axis=1)
    ec = jnp.concatenate([ec, ec], axis=-2)

    kr, ki = pw_re[:, :T, :, :, None], pw_im[:, :T, :, :, None]
    ct_re, ct_im = jnp.swapaxes(c_re, -1, -2)[:, None], jnp.swapaxes(c_im, -1, -2)[:, None]
    car = jnp.tile(ct_re * kr - ct_im * ki, (1, 1, 1, 1, H))
    cai = jnp.tile(ct_re * ki + ct_im * kr, (1, 1, 1, 1, H))
    bl_re = jnp.repeat(jnp.swapaxes(bb_re, -1, -2), H, axis=-1)[:, None]
    bl_im = jnp.repeat(jnp.swapaxes(bb_im, -1, -2), H, axis=-1)[:, None]
    kd = jnp.sum(car * bl_re - cai * bl_im, axis=3).reshape(2, T, G, H, H)
    kf, kb = kd[0], kd[1]
    klag = jnp.concatenate([kb[:0:-1], (kf[0] + kb[0])[None], kf[1:]], axis=0)
    kt = jnp.tile(klag.reshape(2 * T - 1, S5_OCT, LANES, H), (1, 1, 1, S5_GPO))

    at = jnp.stack([pw_re[:, T], pw_im[:, T]], axis=1)
    at = at.reshape(2, 2, S5_OCT, S5_GPO * P).transpose(0, 2, 1, 3)
    return kt.astype(BF16), fc.astype(BF16), ec.astype(BF16), at


def _s5_kernel(*refs, n_seq, n_chunk):
    T = S5_T
    (x_ref, kt_ref, fc_ref, ec_ref, at_ref, h0_ref, d_ref, out_ref, fin_ref,
     sf_ref, sb_ref, wt_ref, wf_ref, we_ref) = refs
    n_tile = S5_SW // LANES
    nh = n_tile // 2

    @pl.when(pl.program_id(1) == 0)
    def _():
        _s5_expand(kt_ref, fc_ref, ec_ref, wt_ref, wf_ref, we_ref)

    M = x_ref.shape[0]
    blocks = [slice(r, r + S5_ROWS) for r in range(0, M, S5_ROWS)]
    seq_pad = sf_ref.shape[1] // n_chunk

    def seq_rows(rows):
        return [(r // n_chunk, slice(r - rows.start, r - rows.start + n_chunk))
                for r in range(rows.start, rows.stop, n_chunk)]

    def to_tiles(ref, rows, val):
        for b, local in seq_rows(rows):
            for j in range(n_tile):
                ref[j, pl.ds(b, n_chunk, stride=seq_pad), :] = val[local, j * LANES:(j + 1) * LANES]

    def from_tiles(ref, rows):
        return jnp.concatenate(
            [jnp.concatenate([ref[j, pl.ds(b, n_chunk, stride=seq_pad), :].astype(BF16)
                              for j in range(n_tile)], axis=1)
             for b, _ in seq_rows(rows)], axis=0)

    for rows in blocks:
        xb = x_ref[rows, :].astype(BF16)
        to_tiles(sf_ref, rows, _dot(xb, wf_ref[0]))
        to_tiles(sb_ref, rows, _dot(xb, wf_ref[1]))

    def coeff(d):
        return [(at_ref[d, 0, 0:1, j * LANES:(j + 1) * LANES],
                 at_ref[d, 0, 1:2, j * LANES:(j + 1) * LANES]) for j in range(nh)]

    def advance(ref, rows, a, state):
        out = []
        for j in range(nh):
            (ar, ai), (hr, hi) = a[j], state[j]
            sr, si = ref[j, rows, :], ref[nh + j, rows, :]
            ref[j, rows, :] = hr
            ref[nh + j, rows, :] = hi
            out.append((ar * hr - ai * hi + sr, ar * hi + ai * hr + si))
        return out

    af, ab = coeff(0), coeff(1)

    def step(i, carry):
        hf, hb = carry
        fwd = pl.multiple_of(i * seq_pad, seq_pad)
        bwd = pl.multiple_of((n_chunk - 1 - i) * seq_pad, seq_pad)
        hf = advance(sf_ref, pl.ds(fwd, n_seq), af, hf)
        hb = advance(sb_ref, pl.ds(bwd, n_seq), ab, hb)
        return hf, hb

    def initial(d):
        return [(h0_ref[d, 0, 0, :, j * LANES:(j + 1) * LANES],
                 h0_ref[d, 0, 0, :, (nh + j) * LANES:(nh + j + 1) * LANES]) for j in range(nh)]

    finals = lax.fori_loop(0, n_chunk, step, (initial(0), initial(1)))
    for d in range(2):
        for j in range(nh):
            fin_ref[d, 0, 0, :, j * LANES:(j + 1) * LANES] = finals[d][j][0]
            fin_ref[d, 0, 0, :, (nh + j) * LANES:(nh + j + 1) * LANES] = finals[d][j][1]

    d = jnp.concatenate([d_ref[0]] * T, axis=1)
    for rows in blocks:
        x = x_ref[rows, :]
        y = (_dot(x.astype(BF16), wt_ref[...]) + _dot(from_tiles(sf_ref, rows), we_ref[0])
             + _dot(from_tiles(sb_ref, rows), we_ref[1]))
        out_ref[rows, :] = jax.nn.gelu(y + d * x)


def _s5_expand(kt_ref, fc_ref, ec_ref, wt_ref, wf_ref, we_ref):
    T = S5_T
    nh = S5_SW // LANES // 2
    row = lax.broadcasted_iota(jnp.int32, (LANES, LANES), 0)
    lane = lax.broadcasted_iota(jnp.int32, (LANES, LANES), 1)
    own_k = (row // S5_CH) == (lane // S5_CH)
    lag_tiles = [jnp.where(own_k, kt_ref[l, 0].astype(F32), 0.0).astype(BF16)
                 for l in range(2 * T - 1)]
    for s in range(T):
        for t in range(T):
            wt_ref[s * LANES:(s + 1) * LANES, t * LANES:(t + 1) * LANES] = lag_tiles[t - s + T - 1]
    gpt = LANES // S5_STATE
    own_f = [(row // S5_CH) == (gpt * j + lane // S5_STATE) for j in range(nh)]
    own_e = [(lane // S5_CH) == (gpt * j + row // S5_STATE) for j in range(nh)]
    for d in range(2):
        for ri in range(2):
            for s in range(T):
                f, e = fc_ref[d, ri, s, 0].astype(F32), ec_ref[d, ri, s, 0].astype(F32)
                rows = slice(s * LANES, (s + 1) * LANES)
                for j in range(nh):
                    cols = slice((ri * nh + j) * LANES, (ri * nh + j + 1) * LANES)
                    wf_ref[d, rows, cols] = jnp.where(own_f[j], f, 0.0).astype(BF16)
                    we_ref[d, cols, rows] = jnp.where(own_e[j], e, 0.0).astype(BF16)


def _s5_mixer(u, weights, h0, d_skip, n_seq, n_split):
    T = S5_T
    M_all = u.shape[0]
    M = M_all // n_split
    B = n_seq // n_split
    n_chunk = M // B
    seq_pad = -(-B // SUBLANES) * SUBLANES
    kt, fc, ec, at = weights
    TL = T * LANES
    x_spec = pl.BlockSpec((M, TL), lambda o, j: (j, o))
    tile_spec = pl.BlockSpec((2, 2, T, 1, LANES, LANES), lambda o, j: (0, 0, 0, o, 0, 0))
    st_spec = pl.BlockSpec((2, 1, 1, B, S5_SW), lambda o, j: (0, o, j, 0, 0))
    g, fin = pl.pallas_call(
        functools.partial(_s5_kernel, n_seq=B, n_chunk=n_chunk),
        grid=(S5_OCT, n_split),
        in_specs=[x_spec,
                  pl.BlockSpec((2 * T - 1, 1, LANES, LANES), lambda o, j: (0, o, 0, 0)),
                  tile_spec, tile_spec,
                  pl.BlockSpec((2, 1, 2, S5_SW // 2), lambda o, j: (0, o, 0, 0)),
                  st_spec,
                  pl.BlockSpec((1, 1, LANES), lambda o, j: (o, 0, 0))],
        out_specs=[x_spec, st_spec],
        out_shape=[jax.ShapeDtypeStruct((M_all, S5_OCT * TL), F32),
                   jax.ShapeDtypeStruct((2, S5_OCT, n_split, B, S5_SW), F32)],
        scratch_shapes=[pltpu.VMEM((S5_SW // LANES, n_chunk * seq_pad, LANES), F32)] * 2 + [
            pltpu.VMEM((TL, TL), BF16), pltpu.VMEM((2, TL, S5_SW), BF16),
            pltpu.VMEM((2, S5_SW, TL), BF16)],
        compiler_params=_params(2),
        name="s5_mixer",
    )(u, kt, fc, ec, at, h0, d_skip.reshape(S5_OCT, 1, LANES))
    return g, fin


def _mlp(h, w1_ref, w2_ref):
    hb = h.astype(BF16)
    acc = None
    for j in range(FF_HIDDEN // FF_CHUNK):
        a = _dot(hb, w1_ref[:, j * FF_CHUNK:(j + 1) * FF_CHUNK])
        a = jnp.square(jnp.maximum(a, 0.0)).astype(BF16)
        p = _dot(a, w2_ref[j * FF_CHUNK:(j + 1) * FF_CHUNK, :])
        acc = p if acc is None else acc + p
    return acc


def _interleave(*staged):
    live = list(staged)
    while live:
        for item in list(live):
            gen, per_round = item
            for _ in range(per_round):
                if next(gen, _DONE) is _DONE:
                    live.remove(item)
                    break


_DONE = object()


def _post0_kernel(x_ref, attn_ref, g_ref, mod_ref, ng_ref, wglu_ref, wout_ref, w1_ref, w2_ref, o_ref,
                  gbuf_ref, *, sub):
    bb, tm = x_ref.shape[0], x_ref.shape[1]
    n_sub = tm // sub
    cps = sub // S5_T
    for k in range(bb * n_sub):
        bl, rows = k // n_sub, slice((k % n_sub) * sub, (k % n_sub + 1) * sub)
        for o in range(S5_OCT):
            for s in range(S5_T):
                c0 = (o * S5_T + s) * LANES
                gbuf_ref[k, o, pl.ds(s, cps, stride=S5_T), :] = (
                    g_ref[k * cps:(k + 1) * cps, c0:c0 + LANES])
        g = jnp.concatenate([gbuf_ref[k, o] for o in range(S5_OCT)], axis=1)
        s5 = g * jax.nn.sigmoid(_dot(g.astype(BF16), wglu_ref[...]))
        oc = (_dot(attn_ref[bl, rows], wout_ref[0:DA_WIDTH, :])
              + _dot(s5.astype(BF16), wout_ref[DA_WIDTH:D_MODEL, :]))
        x1 = x_ref[bl, rows] + mod_ref[0, 2:3, :] * oc
        h = _rms(x1, ng_ref[...]) * (1.0 + mod_ref[0, 4:5, :]) + mod_ref[0, 3:4, :]
        o_ref[bl, rows] = x1 + mod_ref[0, 5:6, :] * _mlp(h, w1_ref, w2_ref)


def _post0(x, attn, g, mods, mod_row, norm_g, w_glu, w_out, ff, layer, tiling):
    B, L, _ = x.shape
    bb, tm, sub = tiling
    w1, w2 = ff
    row_spec = lambda w: pl.BlockSpec((bb, tm, w), lambda b, i: (b, i, 0))
    return pl.pallas_call(
        functools.partial(_post0_kernel, sub=sub),
        grid=(B // bb, L // tm),
        in_specs=[row_spec(D_MODEL), row_spec(DA_WIDTH), _chunk_spec(L, bb, tm),
                  pl.BlockSpec((1, N_MOD, D_MODEL), lambda b, i: (mod_row(b), 0, 0)),
                  _const_spec((1, D_MODEL)),
                  _const_spec((S5_WIDTH, S5_WIDTH)),
                  _const_spec((D_MODEL, D_MODEL)),
                  _layer_spec((D_MODEL, FF_HIDDEN), layer),
                  _layer_spec((FF_HIDDEN, D_MODEL), layer)],
        out_specs=row_spec(D_MODEL),
        out_shape=jax.ShapeDtypeStruct((B, L, D_MODEL), F32),
        scratch_shapes=[pltpu.VMEM((bb * tm // sub, S5_OCT, sub, LANES), F32)],
        compiler_params=_params(2),
        name="mix_out_mlp",
    )(x, attn, g, mods, norm_g.reshape(1, D_MODEL), w_glu, w_out, w1, w2)


def _layer1_kernel(x_ref, xp_ref, xn_ref, mod_ref, modp_ref, ng1_ref, ng2_ref, pw_ref, ps_ref,
                   w1_ref, w2_ref, fg_ref, o_ref, x1a_ref, ha_ref, x1b_ref, hb_ref,
                   *, seq_len, n_steps):
    s = pl.program_id(0)
    bb, tm = x_ref.shape[0], x_ref.shape[1]
    H = POOL_HALO
    n = tm + 2 * H
    tile = jnp.minimum(s, n_steps - 1)
    first = (tile % (seq_len // tm)) * tm

    def pool(x1_ref, h_ref):
        for bl in range(bb):
            x = x_ref[bl]
            xe = jnp.concatenate([xp_ref[bl], x, xn_ref[bl]], axis=0)
            he = _rms(xe, ng1_ref[...]) * (1.0 + mod_ref[0, 1:2, :]) + mod_ref[0, 0:1, :]
            pos_e = first - H + lax.broadcasted_iota(jnp.int32, (n, 1), 0)
            he = jnp.where((pos_e >= 0) & (pos_e < seq_len), he, 0.0)
            pos = first + lax.broadcasted_iota(jnp.int32, (tm, 1), 0)
            yield
            zs = []
            for gi, w in enumerate(POOL_WINDOWS):
                hg = he[:, gi * POOL_CH:(gi + 1) * POOL_CH]
                half = w // 2
                fwd, k = hg, 1
                while 2 * k < w:
                    fwd = fwd + pltpu.roll(fwd, n - k, axis=0)
                    k *= 2
                if half % SUBLANES == 0:
                    total = fwd[H - half:H - half + tm] + fwd[H:H + tm]
                else:
                    total = (fwd + pltpu.roll(fwd, half, axis=0))[H:H + tm]
                count = jnp.clip(pos + half, 0, seq_len) - jnp.clip(pos - half, 0, seq_len)
                z = total * (1.0 / count.astype(F32)) - hg[H:H + tm]
                zs.append(_dot(z.astype(BF16), pw_ref[gi]))
                yield
            oc = jnp.concatenate(zs, axis=1) * ps_ref[...]
            x1 = x + mod_ref[0, 2:3, :] * oc
            h = _rms(x1, ng2_ref[...]) * (1.0 + mod_ref[0, 4:5, :]) + mod_ref[0, 3:4, :]
            x1_ref[bl * tm:(bl + 1) * tm, :] = x1
            h_ref[bl * tm:(bl + 1) * tm, :] = h.astype(BF16)
            yield

    def mlp(x1_ref, h_ref):
        hb = h_ref[...]
        acc = None
        for j in range(FF_HIDDEN // FF_CHUNK):
            a = _dot(hb, w1_ref[:, j * FF_CHUNK:(j + 1) * FF_CHUNK])
            a = jnp.square(jnp.maximum(a, 0.0)).astype(BF16)
            yield
            p = _dot(a, w2_ref[j * FF_CHUNK:(j + 1) * FF_CHUNK, :])
            acc = p if acc is None else acc + p
            yield
        y = _rms(x1_ref[...] + modp_ref[0, 5:6, :] * acc, fg_ref[...])
        for bl in range(bb):
            o_ref[bl] = y[bl * tm:(bl + 1) * tm]

    pool_per_mlp = bb

    @pl.when(s == 0)
    def _():
        _interleave((pool(x1a_ref, ha_ref), 1))

    @pl.when((s > 0) & (s % 2 == 1))
    def _():
        _interleave((mlp(x1a_ref, ha_ref), 1), (pool(x1b_ref, hb_ref), pool_per_mlp))

    @pl.when((s > 0) & (s % 2 == 0))
    def _():
        _interleave((mlp(x1b_ref, hb_ref), 1), (pool(x1a_ref, ha_ref), pool_per_mlp))


def _layer1(x, mods, mod_row, norm_g, pool_w, pool_scale, ff, layer, final_g, tiling):
    B, L, _ = x.shape
    bb, tm = tiling
    w1, w2 = ff
    H = POOL_HALO
    per = tm // H
    last = L // H - 1
    n_i = L // tm
    n_steps = (B // bb) * n_i

    def at(shift, fn):
        def index(s):
            t = jnp.clip(s - shift, 0, n_steps - 1)
            return fn(t // n_i, t % n_i)
        return index

    row_map = lambda b, i: (b, i, 0)
    mod_map = lambda b, i: (mod_row(b), 0, 0)
    rows = bb * tm
    return pl.pallas_call(
        functools.partial(_layer1_kernel, seq_len=L, n_steps=n_steps),
        grid=(n_steps + 1,),
        in_specs=[pl.BlockSpec((bb, tm, D_MODEL), at(0, row_map)),
                  pl.BlockSpec((bb, H, D_MODEL),
                               at(0, lambda b, i: (b, jnp.maximum(i * per - 1, 0), 0))),
                  pl.BlockSpec((bb, H, D_MODEL),
                               at(0, lambda b, i: (b, jnp.minimum((i + 1) * per, last), 0))),
                  pl.BlockSpec((1, N_MOD, D_MODEL), at(0, mod_map)),
                  pl.BlockSpec((1, N_MOD, D_MODEL), at(1, mod_map)),
                  _const_spec((1, D_MODEL)), _const_spec((1, D_MODEL)),
                  _const_spec((len(POOL_WINDOWS), POOL_CH, POOL_CH)),
                  _const_spec((1, D_MODEL)),
                  _layer_spec((D_MODEL, FF_HIDDEN), layer),
                  _layer_spec((FF_HIDDEN, D_MODEL), layer),
                  _const_spec((1, D_MODEL))],
        out_specs=pl.BlockSpec((bb, tm, D_MODEL), at(1, row_map)),
        out_shape=jax.ShapeDtypeStruct((B, L, D_MODEL), F32),
        scratch_shapes=[pltpu.VMEM((rows, D_MODEL), F32), pltpu.VMEM((rows, D_MODEL), BF16)] * 2,
        compiler_params=_params(1),
        name="pool_mlp_norm",
    )(x, x, x, mods, mods, norm_g[0].reshape(1, D_MODEL), norm_g[1].reshape(1, D_MODEL),
      pool_w, pool_scale.reshape(1, D_MODEL), w1, w2, final_g.reshape(1, D_MODEL))


def kernel(x_prompt, x_sample, cache_k, cache_v, state_s5, c, c_ctx, mod_w, mod_b, norm_g, mix_w_in, mix_w_out, diff_lambda_qk, diff_subln_g, s5_lambda_re, s5_lambda_im, s5_log_dt, s5_b_re, s5_b_im, s5_c_re, s5_c_im, s5_d, s5_w_glu, pool_w, pool_scale, ff_w1, ff_w2, final_norm_g):
    B_ctx, L_ctx, _ = x_prompt.shape
    B_lat, L_lat, _ = x_sample.shape
    ctx_row = B_lat
    n_rows = 8
    cc = jnp.concatenate([c, c_ctx[None], jnp.zeros((n_rows - B_lat - 1, D_MODEL), F32)], axis=0)
    mods = _modulation(cc, mod_w, mod_b).reshape(DEPTH, n_rows, N_MOD, D_MODEL)
    lat_row = lambda b: b
    ctx_mod_row = lambda b: ctx_row

    j = 0
    lam_init = 0.8 - 0.6 * math.exp(-0.3 * 0)
    w_in = mix_w_in[j].astype(BF16)
    w_out = mix_w_out[j].astype(BF16)
    w_glu = s5_w_glu[j].astype(BF16)
    ff = (ff_w1.astype(BF16), ff_w2.astype(BF16))
    s5w = _s5_weights(s5_lambda_re[j], s5_lambda_im[j], s5_log_dt[j], s5_b_re[j], s5_b_im[j],
                      s5_c_re[j], s5_c_im[j])

    qc, kc, vc, uc, kc32, vc32 = _premix(x_prompt, mods[0], ctx_mod_row, norm_g[0, 0], w_in,
                                         None, 256, True)
    ql, kl, vl, ul = _premix(x_sample, mods[0], lat_row, norm_g[0, 0], w_in,
                             _rope_tables(L_lat), 512, False)

    ac = _attention(qc, kc, vc, None, diff_lambda_qk[j], diff_subln_g[j], lam_init, 256, DA_HEADS)
    P = cache_k.shape[2]
    cache = (cache_k[:, j].reshape(B_lat, P, DA_WIDTH), cache_v[:, j].reshape(B_lat, P, DA_WIDTH))
    al = _attention(ql, kl, vl, cache, diff_lambda_qk[j], diff_subln_g[j], lam_init, 1024, 1)

    lat_split = 1
    h0_ctx = jnp.zeros((2, S5_OCT, 1, B_ctx, S5_SW), F32)
    h0_lat = state_s5[:, j].reshape(B_lat, 2, S5_OCT, S5_GPO, S5_STATE, 2)
    h0_lat = h0_lat.transpose(1, 2, 0, 5, 3, 4).reshape(
        2, S5_OCT, lat_split, B_lat // lat_split, S5_SW)
    gc, fin_c = _s5_mixer(uc, s5w, h0_ctx, s5_d[j], B_ctx, 1)
    gl, _ = _s5_mixer(ul, s5w, h0_lat, s5_d[j], B_lat, lat_split)

    ctx = _post0(x_prompt, ac, gc, mods[0], ctx_mod_row, norm_g[0, 1], w_glu, w_out, ff, 0,
                 CTX_TILING)
    lat = _post0(x_sample, al, gl, mods[0], lat_row, norm_g[0, 1], w_glu, w_out, ff, 0,
                 LAT_TILING)

    pw = pool_w[0].astype(BF16)
    y_prompt = _layer1(ctx, mods[1], ctx_mod_row, norm_g[1], pw, pool_scale[0], ff, 1,
                       final_norm_g, CTX_POOL_TILING)
    y_sample = _layer1(lat, mods[1], lat_row, norm_g[1], pw, pool_scale[0], ff, 1,
                       final_norm_g, LAT_POOL_TILING)

    new_cache_k = kc32.reshape(B_ctx, 1, L_ctx, DA_HEADS, 2 * DA_QK)
    new_cache_v = vc32.reshape(B_ctx, 1, L_ctx, DA_HEADS, DA_V)
    st = fin_c.reshape(2, S5_OCT, B_ctx, 2, S5_GPO, S5_STATE)
    new_state = st.transpose(2, 0, 1, 4, 5, 3).reshape(B_ctx, 1, 2, S5_GROUPS, S5_STATE, 2)
    return (y_prompt, y_sample, new_cache_k, new_cache_v, new_state)
```

```python
import functools
import math

import jax
import jax.numpy as jnp
from jax import lax
from jax.experimental import pallas as pl
from jax.experimental.pallas import tpu as pltpu

D_MODEL = 1024
DEPTH = 2
GRID_W = 64
DA_HEADS = 4
DA_QK = 64
DA_V = 2 * DA_QK
DA_WIDTH = DA_HEADS * DA_V
S5_WIDTH = D_MODEL - DA_WIDTH
S5_CH = 16
S5_GROUPS = S5_WIDTH // S5_CH
S5_STATE = 64
IN_WIDTH = 3 * DA_WIDTH + S5_WIDTH
POOL_WINDOWS = (2, 4, 8, 16)
POOL_CH = D_MODEL // len(POOL_WINDOWS)
FF_HIDDEN = 4 * D_MODEL
N_MOD = 6
ROPE_BASE = 10000.0
EPS = 1e-6

SUBLANES = 8
LANES = 128
S5_T = 8
S5_OCT = S5_WIDTH // LANES
S5_GPO = LANES // S5_CH
S5_ROWS = 512
S5_SW = 2 * S5_GPO * S5_STATE
Q_SCALE = DA_QK ** -0.5 * math.log2(math.e)
CTX_TILING = (2, 256, 256)
LAT_TILING = (1, 1024, 512)
CTX_POOL_TILING = (2, 256)
LAT_POOL_TILING = (1, 512)
PREMIX_SUB = 512
ATTN_SUB = 256
POOL_HALO = 16
FF_CHUNK = 1024
VMEM_LIMIT = 56 * 1024 * 1024

F32 = jnp.float32
BF16 = jnp.bfloat16


def _const_spec(shape):
    nd = len(shape)
    return pl.BlockSpec(shape, lambda *_: (0,) * nd, pipeline_mode=pl.Buffered(1))


def _layer_spec(shape, layer):
    return pl.BlockSpec((None,) + shape, lambda *_: (layer, 0, 0), pipeline_mode=pl.Buffered(1))


def _params(n_axes):
    return pltpu.CompilerParams(dimension_semantics=("arbitrary",) * n_axes,
                                vmem_limit_bytes=VMEM_LIMIT)


def _rms(x, g):
    return x * lax.rsqrt(jnp.mean(x * x, axis=-1, keepdims=True) + EPS) * g


def _dot(a, b):
    return jnp.dot(a, b, preferred_element_type=F32)


def _mod_kernel(c_ref, w_ref, b_ref, o_ref):
    c = c_ref[...]
    s = (c * jax.nn.sigmoid(c)).astype(BF16)
    o_ref[0] = _dot(s, w_ref[0].astype(BF16)) + b_ref[0]


def _modulation(cc, mod_w, mod_b):
    rows = cc.shape[0]
    tn = 1536
    n = N_MOD * D_MODEL
    return pl.pallas_call(
        _mod_kernel,
        grid=(DEPTH, n // tn),
        in_specs=[pl.BlockSpec((rows, D_MODEL), lambda l, j: (0, 0)),
                  pl.BlockSpec((1, D_MODEL, tn), lambda l, j: (l, 0, j)),
                  pl.BlockSpec((1, 1, tn), lambda l, j: (l, 0, j))],
        out_specs=pl.BlockSpec((1, rows, tn), lambda l, j: (l, 0, j)),
        out_shape=jax.ShapeDtypeStruct((DEPTH, rows, n), F32),
        compiler_params=_params(2),
        name="modulation",
    )(cc, mod_w, mod_b.reshape(DEPTH, 1, n))


def _premix_kernel(*refs, rope, cache_out):
    x_ref, mod_ref, g_ref, w_ref = refs[:4]
    refs = refs[4:]
    if rope:
        cos_ref, sa_ref, sb_ref = refs[:3]
        refs = refs[3:]
    q_ref, k_ref, v_ref, u_ref = refs[:4]
    if cache_out:
        kf_ref, vf_ref = refs[4:6]
    ubuf_ref = refs[-1]
    bb, tm = x_ref.shape[0], x_ref.shape[1]
    sub = PREMIX_SUB
    cps = sub // S5_T

    def store(ref, r0, c0, val):
        piece = min(sub, tm)
        for p in range(sub // piece):
            bl, t0 = divmod(r0 + p * piece, tm)
            ref[bl, t0:t0 + piece, c0:c0 + val.shape[1]] = val[p * piece:(p + 1) * piece]

    def problem(k):
        r0 = k * sub
        piece = min(sub, tm)
        x = jnp.concatenate([x_ref[divmod(r0 + p * piece, tm)[0],
                                   pl.ds(divmod(r0 + p * piece, tm)[1], piece), :]
                             for p in range(sub // piece)], axis=0)
        h = _rms(x, g_ref[...]) * (1.0 + mod_ref[0, 1:2, :]) + mod_ref[0, 0:1, :]
        hb = h.astype(BF16)
        yield
        z = _dot(hb, w_ref[:, 0:2 * DA_WIDTH])
        yield
        for j in range(2 * DA_HEADS):
            s = z[:, j * LANES:(j + 1) * LANES]
            if cache_out and j >= DA_HEADS:
                store(kf_ref, r0, (j - DA_HEADS) * LANES, s)
            if rope:
                t = slice(r0, r0 + sub)
                s = (s * cos_ref[t, :] + pltpu.roll(s, LANES - 16, axis=1) * sa_ref[t, :]
                     + pltpu.roll(s, 16, axis=1) * sb_ref[t, :])
            if j < DA_HEADS:
                store(q_ref, r0, j * LANES, (s * Q_SCALE).astype(BF16))
            else:
                store(k_ref, r0, (j - DA_HEADS) * LANES, s.astype(BF16))
        yield
        z = _dot(hb, w_ref[:, 2 * DA_WIDTH:IN_WIDTH])
        yield
        v = z[:, 0:DA_WIDTH]
        store(v_ref, r0, 0, v.astype(BF16))
        if cache_out:
            store(vf_ref, r0, 0, v)
        for o in range(S5_OCT):
            ubuf_ref[k, o] = z[:, DA_WIDTH + o * LANES:DA_WIDTH + (o + 1) * LANES]
        for o in range(S5_OCT):
            for s in range(S5_T):
                c0 = (o * S5_T + s) * LANES
                u_ref[k * cps:(k + 1) * cps, c0:c0 + LANES] = (
                    ubuf_ref[k, o, pl.ds(s, cps, stride=S5_T), :])

    _interleave(*[(problem(k), 1, k) for k in range(bb * tm // sub)])


def _chunk_spec(L, bb, tm):
    assert bb == 1 or tm == L
    return pl.BlockSpec((bb * tm // S5_T, S5_WIDTH * S5_T), lambda b, i: (b * (L // tm) + i, 0))


def _premix(x, mods, mod_row, norm_g, w_in, rope_tabs, tiling, cache_out):
    B, L, _ = x.shape
    bb, tm = tiling
    rope = rope_tabs is not None
    row_spec = lambda w: pl.BlockSpec((bb, tm, w), lambda b, i: (b, i, 0))
    in_specs = [row_spec(D_MODEL),
                pl.BlockSpec((1, N_MOD, D_MODEL), lambda b, i: (mod_row(b), 0, 0)),
                _const_spec((1, D_MODEL)),
                _const_spec((D_MODEL, IN_WIDTH))]
    args = [x, mods, norm_g.reshape(1, D_MODEL), w_in]
    if rope:
        in_specs += [pl.BlockSpec((tm, LANES), lambda b, i: (i, 0))] * 3
        args += list(rope_tabs)
    out_specs = [row_spec(DA_WIDTH)] * 3 + [_chunk_spec(L, bb, tm)]
    out_shape = [jax.ShapeDtypeStruct((B, L, DA_WIDTH), BF16)] * 3 + [
        jax.ShapeDtypeStruct((B * L // S5_T, S5_WIDTH * S5_T), F32)]
    if cache_out:
        out_specs += [row_spec(DA_WIDTH)] * 2
        out_shape += [jax.ShapeDtypeStruct((B, L, DA_WIDTH), F32)] * 2
    return pl.pallas_call(
        functools.partial(_premix_kernel, rope=rope, cache_out=cache_out),
        grid=(B // bb, L // tm),
        in_specs=in_specs, out_specs=out_specs, out_shape=out_shape,
        scratch_shapes=[pltpu.VMEM((bb * tm // PREMIX_SUB, S5_OCT, PREMIX_SUB, LANES), F32)],
        compiler_params=_params(2),
        name="premix_rope" if rope else "premix",
    )(*args)


def _rope_tables(L):
    rows = L // GRID_W
    row = jnp.repeat(jnp.arange(rows), GRID_W).astype(F32)
    col = jnp.tile(jnp.arange(GRID_W), rows).astype(F32)
    half = DA_QK // 2
    inv = 1.0 / (ROPE_BASE ** (jnp.arange(0, half, 2, dtype=F32) / half))
    ang_r, ang_c = row[:, None] * inv, col[:, None] * inv
    zero = jnp.zeros_like(ang_r)
    cos64 = jnp.concatenate([jnp.cos(ang_r)] * 2 + [jnp.cos(ang_c)] * 2, axis=1)
    sa64 = jnp.concatenate([-jnp.sin(ang_r), zero, -jnp.sin(ang_c), zero], axis=1)
    sb64 = jnp.concatenate([zero, jnp.sin(ang_r), zero, jnp.sin(ang_c)], axis=1)
    return tuple(jnp.tile(t, (1, LANES // DA_QK)) for t in (cos64, sa64, sb64))


def _attn_kernel(*refs, lam_init, has_cache):
    lqk_ref, g_ref, q_ref, k_ref, v_ref = refs[:5]
    if has_cache:
        ck_ref, cv_ref, o_ref = refs[5:8]
    else:
        o_ref = refs[5]
    lqk = lqk_ref[...]
    lam = (jnp.exp(jnp.sum(lqk[0:1] * lqk[1:2], axis=-1, keepdims=True))
           - jnp.exp(jnp.sum(lqk[2:3] * lqk[3:4], axis=-1, keepdims=True)) + lam_init)
    tq = q_ref.shape[1]
    lane = lax.broadcasted_iota(jnp.int32, (ATTN_SUB, LANES), 1)
    zero = jnp.zeros((ATTN_SUB, LANES), BF16)
    nt = (((1,), (1,)), ((), ()))
    mxu_sum = has_cache
    ones = lambda v: jnp.concatenate([v, jnp.ones_like(v)], axis=1) if mxu_sum else v

    def problem(cols, rows, segs):
        q = q_ref[0, rows, cols]
        outs = []
        for qm in (jnp.where(lane < DA_QK, q, zero), jnp.where(lane >= DA_QK, q, zero)):
            ss = [lax.dot_general(qm, k, nt, preferred_element_type=F32) for k, _ in segs]
            yield
            m = functools.reduce(jnp.maximum, [jnp.max(s, axis=-1, keepdims=True) for s in ss])
            es = [jnp.exp2(s - m) for s in ss]
            if not mxu_sum:
                l = functools.reduce(jnp.add, [jnp.sum(e, axis=-1, keepdims=True) for e in es])
            es = [e.astype(BF16) for e in es]
            yield
            o = functools.reduce(jnp.add, [_dot(e, v) for e, (_, v) in zip(es, segs)])
            if mxu_sum:
                l = o[:, DA_V:DA_V + 1]
            outs.append(o[:, :DA_V] / l)
            yield
        o = outs[0] - lam * outs[1]
        o_ref[0, rows, cols] = (_rms(o, g_ref[...]) * (1.0 - lam_init)).astype(BF16)

    problems = []
    for hh in range(q_ref.shape[2] // LANES):
        cols = slice(hh * LANES, (hh + 1) * LANES)
        segs = [(k_ref[0, :, cols], ones(v_ref[0, :, cols]))]
        if has_cache:
            segs.append((ck_ref[0, :, cols].astype(BF16), ones(cv_ref[0, :, cols].astype(BF16))))
        for r in range(tq // ATTN_SUB):
            problems.append(problem(cols, slice(r * ATTN_SUB, (r + 1) * ATTN_SUB), segs))
    _interleave(*[(p, 1, i) for i, p in enumerate(problems)])


def _attention(q, k, v, cache, lambda_qk, subln_g, lam_init, tq, heads):
    B, L, _ = q.shape
    has_cache = cache is not None
    q_spec = pl.BlockSpec((1, tq, heads * LANES), lambda b, h, i: (b, i, h))
    kv_spec = lambda n: pl.BlockSpec((1, n, heads * LANES), lambda b, h, i: (b, 0, h))
    in_specs = [_const_spec((4, DA_QK)), _const_spec((1, DA_V)), q_spec, kv_spec(L), kv_spec(L)]
    args = [lambda_qk, subln_g.reshape(1, DA_V), q, k, v]
    if has_cache:
        P = cache[0].shape[1]
        in_specs += [kv_spec(P), kv_spec(P)]
        args += list(cache)
    return pl.pallas_call(
        functools.partial(_attn_kernel, lam_init=lam_init, has_cache=has_cache),
        grid=(B, DA_HEADS // heads, L // tq),
        in_specs=in_specs, out_specs=q_spec,
        out_shape=jax.ShapeDtypeStruct((B, L, DA_WIDTH), BF16),
        compiler_params=_params(3),
        name="diff_attn_cache" if has_cache else "diff_attn",
    )(*args)


def _s5_weights(lam_re, lam_im, log_dt, b_re, b_im, c_re, c_im):
    T, G, P, H = S5_T, S5_GROUPS, S5_STATE, S5_CH
    dt = jnp.exp(log_dt)[..., None]
    lr, li = lam_re * dt, lam_im * dt
    kk = jnp.arange(T + 1, dtype=F32)[None, :, None, None]
    mag = jnp.exp(kk * lr[:, None])
    pw_re = mag * jnp.cos(kk * li[:, None])
    pw_im = mag * jnp.sin(kk * li[:, None])
    a_re, a_im = pw_re[:, 1], pw_im[:, 1]
    den = lam_re * lam_re + lam_im * lam_im
    f_re = ((a_re - 1.0) * lam_re + a_im * lam_im) / den
    f_im = (a_im * lam_re - (a_re - 1.0) * lam_im) / den
    bt_re, bt_im = jnp.swapaxes(b_re, -1, -2), jnp.swapaxes(b_im, -1, -2)
    bb_re = f_re[:, :, None, :] * bt_re - f_im[:, :, None, :] * bt_im
    bb_im = f_re[:, :, None, :] * bt_im + f_im[:, :, None, :] * bt_re

    def per_dir(x, fwd, bwd):
        return jnp.stack([fwd(x[0]), bwd(x[1])], axis=0)

    flip = lambda x: x[::-1]
    pr = per_dir(pw_re, lambda x: flip(x[:T]), lambda x: x[:T])[:, :, :, None, :]
    pi = per_dir(pw_im, lambda x: flip(x[:T]), lambda x: x[:T])[:, :, :, None, :]
    br, bi = bb_re[:, None], bb_im[:, None]
    fc = jnp.stack([pr * br - pi * bi, pr * bi + pi * br], axis=1)
    fc = fc.reshape(2, 2, T, S5_OCT, LANES, P)
    fc = jnp.concatenate([fc, fc], axis=-1)

    def lanes_gh(x):
        x = x.reshape(x.shape[:-2] + (S5_OCT, S5_GPO, P))
        return jnp.repeat(jnp.swapaxes(x, -1, -2), H, axis=-1)

    def c_lanes(c):
        c = c.reshape(2, S5_OCT, S5_GPO, H, P).transpose(0, 1, 4, 2, 3)
        return c.reshape(2, 1, S5_OCT, P, LANES)

    per = lanes_gh(per_dir(pw_re, lambda x: x[1:], lambda x: flip(x[1:])))
    pei = lanes_gh(per_dir(pw_im, lambda x: x[1:], lambda x: flip(x[1:])))
    cr, ci = c_lanes(c_re), c_lanes(c_im)
    ec = jnp.stack([cr * per - ci * pei, -(cr * pei + ci * per)], axis=1)
    ec = jnp.concatenate([ec, ec], axis=-2)

    kr, ki = pw_re[:, :T, :, :, None], pw_im[:, :T, :, :, None]
    ct_re, ct_im = jnp.swapaxes(c_re, -1, -2)[:, None], jnp.swapaxes(c_im, -1, -2)[:, None]
    car = jnp.tile(ct_re * kr - ct_im * ki, (1, 1, 1, 1, H))
    cai = jnp.tile(ct_re * ki + ct_im * kr, (1, 1, 1, 1, H))
    bl_re = jnp.repeat(jnp.swapaxes(bb_re, -1, -2), H, axis=-1)[:, None]
    bl_im = jnp.repeat(jnp.swapaxes(bb_im, -1, -2), H, axis=-1)[:, None]
    kd = jnp.sum(car * bl_re - cai * bl_im, axis=3).reshape(2, T, G, H, H)
    kf, kb = kd[0], kd[1]
    klag = jnp.concatenate([kb[:0:-1], (kf[0] + kb[0])[None], kf[1:]], axis=0)
    kt = jnp.tile(klag.reshape(2 * T - 1, S5_OCT, LANES, H), (1, 1, 1, S5_GPO))

    at = jnp.stack([pw_re[:, T], pw_im[:, T]], axis=1)
    at = at.reshape(2, 2, S5_OCT, S5_GPO * P).transpose(0, 2, 1, 3)
    return kt.astype(BF16), fc.astype(BF16), ec.astype(BF16), at


def _s5_kernel(*refs, n_seq, n_chunk):
    T = S5_T
    (x_ref, kt_ref, fc_ref, ec_ref, at_ref, h0_ref, d_ref, out_ref, fin_ref,
     sf_ref, sb_ref, wt_ref, wf_ref, we_ref) = refs
    n_tile = S5_SW // LANES
    nh = n_tile // 2

    @pl.when(pl.program_id(1) == 0)
    def _():
        _s5_expand(kt_ref, fc_ref, ec_ref, wt_ref, wf_ref, we_ref)

    M = x_ref.shape[0]
    blocks = [slice(r, r + S5_ROWS) for r in range(0, M, S5_ROWS)]
    seq_pad = sf_ref.shape[1] // n_chunk

    def seq_rows(rows):
        return [(r // n_chunk, slice(r - rows.start, r - rows.start + n_chunk))
                for r in range(rows.start, rows.stop, n_chunk)]

    def to_tiles(ref, rows, val):
        for b, local in seq_rows(rows):
            for j in range(n_tile):
                ref[j, pl.ds(b, n_chunk, stride=seq_pad), :] = val[local, j * LANES:(j + 1) * LANES]

    def from_tiles(ref, rows):
        return jnp.concatenate(
            [jnp.concatenate([ref[j, pl.ds(b, n_chunk, stride=seq_pad), :].astype(BF16)
                              for j in range(n_tile)], axis=1)
             for b, _ in seq_rows(rows)], axis=0)

    for rows in blocks:
        xb = x_ref[rows, :].astype(BF16)
        to_tiles(sf_ref, rows, _dot(xb, wf_ref[0]))
        to_tiles(sb_ref, rows, _dot(xb, wf_ref[1]))

    def coeff(d):
        return [(at_ref[d, 0, 0:1, j * LANES:(j + 1) * LANES],
                 at_ref[d, 0, 1:2, j * LANES:(j + 1) * LANES]) for j in range(nh)]

    def advance(ref, rows, a, state):
        out = []
        for j in range(nh):
            (ar, ai), (hr, hi) = a[j], state[j]
            sr, si = ref[j, rows, :], ref[nh + j, rows, :]
            ref[j, rows, :] = hr
            ref[nh + j, rows, :] = hi
            out.append((ar * hr - ai * hi + sr, ar * hi + ai * hr + si))
        return out

    af, ab = coeff(0), coeff(1)

    def step(i, carry):
        hf, hb = carry
        fwd = pl.multiple_of(i * seq_pad, seq_pad)
        bwd = pl.multiple_of((n_chunk - 1 - i) * seq_pad, seq_pad)
        hf = advance(sf_ref, pl.ds(fwd, n_seq), af, hf)
        hb = advance(sb_ref, pl.ds(bwd, n_seq), ab, hb)
        return hf, hb

    def initial(d):
        return [(h0_ref[d, 0, 0, :, j * LANES:(j + 1) * LANES],
                 h0_ref[d, 0, 0, :, (nh + j) * LANES:(nh + j + 1) * LANES]) for j in range(nh)]

    finals = lax.fori_loop(0, n_chunk, step, (initial(0), initial(1)))
    for d in range(2):
        for j in range(nh):
            fin_ref[d, 0, 0, :, j * LANES:(j + 1) * LANES] = finals[d][j][0]
            fin_ref[d, 0, 0, :, (nh + j) * LANES:(nh + j + 1) * LANES] = finals[d][j][1]

    d = jnp.concatenate([d_ref[0]] * T, axis=1)
    for rows in blocks:
        x = x_ref[rows, :]
        y = (_dot(x.astype(BF16), wt_ref[...]) + _dot(from_tiles(sf_ref, rows), we_ref[0])
             + _dot(from_tiles(sb_ref, rows), we_ref[1]))
        out_ref[rows, :] = jax.nn.gelu(y + d * x)


def _s5_expand(kt_ref, fc_ref, ec_ref, wt_ref, wf_ref, we_ref):
    T = S5_T
    nh = S5_SW // LANES // 2
    row = lax.broadcasted_iota(jnp.int32, (LANES, LANES), 0)
    lane = lax.broadcasted_iota(jnp.int32, (LANES, LANES), 1)
    own_k = (row // S5_CH) == (lane // S5_CH)
    lag_tiles = [jnp.where(own_k, kt_ref[l, 0].astype(F32), 0.0).astype(BF16)
                 for l in range(2 * T - 1)]
    for s in range(T):
        for t in range(T):
            wt_ref[s * LANES:(s + 1) * LANES, t * LANES:(t + 1) * LANES] = lag_tiles[t - s + T - 1]
    gpt = LANES // S5_STATE
    own_f = [(row // S5_CH) == (gpt * j + lane // S5_STATE) for j in range(nh)]
    own_e = [(lane // S5_CH) == (gpt * j + row // S5_STATE) for j in range(nh)]
    for d in range(2):
        for ri in range(2):
            for s in range(T):
                f, e = fc_ref[d, ri, s, 0].astype(F32), ec_ref[d, ri, s, 0].astype(F32)
                rows = slice(s * LANES, (s + 1) * LANES)
                for j in range(nh):
                    cols = slice((ri * nh + j) * LANES, (ri * nh + j + 1) * LANES)
                    wf_ref[d, rows, cols] = jnp.where(own_f[j], f, 0.0).astype(BF16)
                    we_ref[d, cols, rows] = jnp.where(own_e[j], e, 0.0).astype(BF16)


def _s5_mixer(u, weights, h0, d_skip, n_seq, n_split):
    T = S5_T
    M_all = u.shape[0]
    M = M_all // n_split
    B = n_seq // n_split
    n_chunk = M // B
    seq_pad = -(-B // SUBLANES) * SUBLANES
    kt, fc, ec, at = weights
    TL = T * LANES
    x_spec = pl.BlockSpec((M, TL), lambda o, j: (j, o))
    tile_spec = pl.BlockSpec((2, 2, T, 1, LANES, LANES), lambda o, j: (0, 0, 0, o, 0, 0))
    st_spec = pl.BlockSpec((2, 1, 1, B, S5_SW), lambda o, j: (0, o, j, 0, 0))
    g, fin = pl.pallas_call(
        functools.partial(_s5_kernel, n_seq=B, n_chunk=n_chunk),
        grid=(S5_OCT, n_split),
        in_specs=[x_spec,
                  pl.BlockSpec((2 * T - 1, 1, LANES, LANES), lambda o, j: (0, o, 0, 0)),
                  tile_spec, tile_spec,
                  pl.BlockSpec((2, 1, 2, S5_SW // 2), lambda o, j: (0, o, 0, 0)),
                  st_spec,
                  pl.BlockSpec((1, 1, LANES), lambda o, j: (o, 0, 0))],
        out_specs=[x_spec, st_spec],
        out_shape=[jax.ShapeDtypeStruct((M_all, S5_OCT * TL), F32),
                   jax.ShapeDtypeStruct((2, S5_OCT, n_split, B, S5_SW), F32)],
        scratch_shapes=[pltpu.VMEM((S5_SW // LANES, n_chunk * seq_pad, LANES), F32)] * 2 + [
            pltpu.VMEM((TL, TL), BF16), pltpu.VMEM((2, TL, S5_SW), BF16),
            pltpu.VMEM((2, S5_SW, TL), BF16)],
        compiler_params=_params(2),
        name="s5_mixer",
    )(u, kt, fc, ec, at, h0, d_skip.reshape(S5_OCT, 1, LANES))
    return g, fin


def _mlp(h, w1_ref, w2_ref):
    hb = h.astype(BF16)
    acc = None
    for j in range(FF_HIDDEN // FF_CHUNK):
        a = _dot(hb, w1_ref[:, j * FF_CHUNK:(j + 1) * FF_CHUNK])
        a = jnp.square(jnp.maximum(a, 0.0)).astype(BF16)
        p = _dot(a, w2_ref[j * FF_CHUNK:(j + 1) * FF_CHUNK, :])
        acc = p if acc is None else acc + p
    return acc


def _interleave(*staged):
    live = list(staged)
    rnd = 0
    while live:
        for item in list(live):
            gen, per_round, first = item
            if rnd < first:
                continue
            for _ in range(per_round):
                if next(gen, _DONE) is _DONE:
                    live.remove(item)
                    break
        rnd += 1


_DONE = object()


def _post0_kernel(x_ref, attn_ref, g_ref, mod_ref, ng_ref, wglu_ref, wout_ref, w1_ref, w2_ref, o_ref,
                  gbuf_ref, *, sub):
    bb, tm = x_ref.shape[0], x_ref.shape[1]
    n_sub = tm // sub
    cps = sub // S5_T
    for k in range(bb * n_sub):
        bl, rows = k // n_sub, slice((k % n_sub) * sub, (k % n_sub + 1) * sub)
        for o in range(S5_OCT):
            for s in range(S5_T):
                c0 = (o * S5_T + s) * LANES
                gbuf_ref[k, o, pl.ds(s, cps, stride=S5_T), :] = (
                    g_ref[k * cps:(k + 1) * cps, c0:c0 + LANES])
        g = jnp.concatenate([gbuf_ref[k, o] for o in range(S5_OCT)], axis=1)
        s5 = g * jax.nn.sigmoid(_dot(g.astype(BF16), wglu_ref[...]))
        oc = (_dot(attn_ref[bl, rows], wout_ref[0:DA_WIDTH, :])
              + _dot(s5.astype(BF16), wout_ref[DA_WIDTH:D_MODEL, :]))
        x1 = x_ref[bl, rows] + mod_ref[0, 2:3, :] * oc
        h = _rms(x1, ng_ref[...]) * (1.0 + mod_ref[0, 4:5, :]) + mod_ref[0, 3:4, :]
        o_ref[bl, rows] = x1 + mod_ref[0, 5:6, :] * _mlp(h, w1_ref, w2_ref)


def _post0(x, attn, g, mods, mod_row, norm_g, w_glu, w_out, ff, layer, tiling):
    B, L, _ = x.shape
    bb, tm, sub = tiling
    w1, w2 = ff
    row_spec = lambda w: pl.BlockSpec((bb, tm, w), lambda b, i: (b, i, 0))
    return pl.pallas_call(
        functools.partial(_post0_kernel, sub=sub),
        grid=(B // bb, L // tm),
        in_specs=[row_spec(D_MODEL), row_spec(DA_WIDTH), _chunk_spec(L, bb, tm),
                  pl.BlockSpec((1, N_MOD, D_MODEL), lambda b, i: (mod_row(b), 0, 0)),
                  _const_spec((1, D_MODEL)),
                  _const_spec((S5_WIDTH, S5_WIDTH)),
                  _const_spec((D_MODEL, D_MODEL)),
                  _layer_spec((D_MODEL, FF_HIDDEN), layer),
                  _layer_spec((FF_HIDDEN, D_MODEL), layer)],
        out_specs=row_spec(D_MODEL),
        out_shape=jax.ShapeDtypeStruct((B, L, D_MODEL), F32),
        scratch_shapes=[pltpu.VMEM((bb * tm // sub, S5_OCT, sub, LANES), F32)],
        compiler_params=_params(2),
        name="mix_out_mlp",
    )(x, attn, g, mods, norm_g.reshape(1, D_MODEL), w_glu, w_out, w1, w2)


def _layer1_kernel(x_ref, xp_ref, xn_ref, mod_ref, modp_ref, ng1_ref, ng2_ref, pw_ref, ps_ref,
                   w1_ref, w2_ref, fg_ref, o_ref, x1a_ref, ha_ref, x1b_ref, hb_ref,
                   *, seq_len, n_steps):
    s = pl.program_id(0)
    bb, tm = x_ref.shape[0], x_ref.shape[1]
    H = POOL_HALO
    n = tm + 2 * H
    tile = jnp.minimum(s, n_steps - 1)
    first = (tile % (seq_len // tm)) * tm

    def pool(x1_ref, h_ref):
        for bl in range(bb):
            x = x_ref[bl]
            xe = jnp.concatenate([xp_ref[bl], x, xn_ref[bl]], axis=0)
            he = _rms(xe, ng1_ref[...]) * (1.0 + mod_ref[0, 1:2, :]) + mod_ref[0, 0:1, :]
            pos_e = first - H + lax.broadcasted_iota(jnp.int32, (n, 1), 0)
            he = jnp.where((pos_e >= 0) & (pos_e < seq_len), he, 0.0)
            pos = first + lax.broadcasted_iota(jnp.int32, (tm, 1), 0)
            yield
            zs = []
            for gi, w in enumerate(POOL_WINDOWS):
                hg = he[:, gi * POOL_CH:(gi + 1) * POOL_CH]
                half = w // 2
                fwd, k = hg, 1
                while 2 * k < w:
                    fwd = fwd + pltpu.roll(fwd, n - k, axis=0)
                    k *= 2
                if half % SUBLANES == 0:
                    total = fwd[H - half:H - half + tm] + fwd[H:H + tm]
                else:
                    total = (fwd + pltpu.roll(fwd, half, axis=0))[H:H + tm]
                count = jnp.clip(pos + half, 0, seq_len) - jnp.clip(pos - half, 0, seq_len)
                z = total * (1.0 / count.astype(F32)) - hg[H:H + tm]
                zs.append(_dot(z.astype(BF16), pw_ref[gi]))
                yield
            oc = jnp.concatenate(zs, axis=1) * ps_ref[...]
            x1 = x + mod_ref[0, 2:3, :] * oc
            h = _rms(x1, ng2_ref[...]) * (1.0 + mod_ref[0, 4:5, :]) + mod_ref[0, 3:4, :]
            x1_ref[bl * tm:(bl + 1) * tm, :] = x1
            h_ref[bl * tm:(bl + 1) * tm, :] = h.astype(BF16)
            yield

    def mlp(x1_ref, h_ref):
        hb = h_ref[...]
        acc = None
        for j in range(FF_HIDDEN // FF_CHUNK):
            a = _dot(hb, w1_ref[:, j * FF_CHUNK:(j + 1) * FF_CHUNK])
            a = jnp.square(jnp.maximum(a, 0.0)).astype(BF16)
            yield
            p = _dot(a, w2_ref[j * FF_CHUNK:(j + 1) * FF_CHUNK, :])
            acc = p if acc is None else acc + p
            yield
        y = _rms(x1_ref[...] + modp_ref[0, 5:6, :] * acc, fg_ref[...])
        for bl in range(bb):
            o_ref[bl] = y[bl * tm:(bl + 1) * tm]

    pool_per_mlp = bb

    @pl.when(s == 0)
    def _():
        _interleave((pool(x1a_ref, ha_ref), 1, 0))

    @pl.when((s > 0) & (s % 2 == 1))
    def _():
        _interleave((mlp(x1a_ref, ha_ref), 1, 0), (pool(x1b_ref, hb_ref), pool_per_mlp, 0))

    @pl.when((s > 0) & (s % 2 == 0))
    def _():
        _interleave((mlp(x1b_ref, hb_ref), 1, 0), (pool(x1a_ref, ha_ref), pool_per_mlp, 0))


def _layer1(x, mods, mod_row, norm_g, pool_w, pool_scale, ff, layer, final_g, tiling):
    B, L, _ = x.shape
    bb, tm = tiling
    w1, w2 = ff
    H = POOL_HALO
    per = tm // H
    last = L // H - 1
    n_i = L // tm
    n_steps = (B // bb) * n_i

    def at(shift, fn):
        def index(s):
            t = jnp.clip(s - shift, 0, n_steps - 1)
            return fn(t // n_i, t % n_i)
        return index

    row_map = lambda b, i: (b, i, 0)
    mod_map = lambda b, i: (mod_row(b), 0, 0)
    rows = bb * tm
    return pl.pallas_call(
        functools.partial(_layer1_kernel, seq_len=L, n_steps=n_steps),
        grid=(n_steps + 1,),
        in_specs=[pl.BlockSpec((bb, tm, D_MODEL), at(0, row_map)),
                  pl.BlockSpec((bb, H, D_MODEL),
                               at(0, lambda b, i: (b, jnp.maximum(i * per - 1, 0), 0))),
                  pl.BlockSpec((bb, H, D_MODEL),
                               at(0, lambda b, i: (b, jnp.minimum((i + 1) * per, last), 0))),
                  pl.BlockSpec((1, N_MOD, D_MODEL), at(0, mod_map)),
                  pl.BlockSpec((1, N_MOD, D_MODEL), at(1, mod_map)),
                  _const_spec((1, D_MODEL)), _const_spec((1, D_MODEL)),
                  _const_spec((len(POOL_WINDOWS), POOL_CH, POOL_CH)),
                  _const_spec((1, D_MODEL)),
                  _layer_spec((D_MODEL, FF_HIDDEN), layer),
                  _layer_spec((FF_HIDDEN, D_MODEL), layer),
                  _const_spec((1, D_MODEL))],
        out_specs=pl.BlockSpec((bb, tm, D_MODEL), at(1, row_map)),
        out_shape=jax.ShapeDtypeStruct((B, L, D_MODEL), F32),
        scratch_shapes=[pltpu.VMEM((rows, D_MODEL), F32), pltpu.VMEM((rows, D_MODEL), BF16)] * 2,
        compiler_params=_params(1),
        name="pool_mlp_norm",
    )(x, x, x, mods, mods, norm_g[0].reshape(1, D_MODEL), norm_g[1].reshape(1, D_MODEL),
      pool_w, pool_scale.reshape(1, D_MODEL), w1, w2, final_g.reshape(1, D_MODEL))


def kernel(x_prompt, x_sample, cache_k, cache_v, state_s5, c, c_ctx, mod_w, mod_b, norm_g, mix_w_in, mix_w_out, diff_lambda_qk, diff_subln_g, s5_lambda_re, s5_lambda_im, s5_log_dt, s5_b_re, s5_b_im, s5_c_re, s5_c_im, s5_d, s5_w_glu, pool_w, pool_scale, ff_w1, ff_w2, final_norm_g):
    B_ctx, L_ctx, _ = x_prompt.shape
    B_lat, L_lat, _ = x_sample.shape
    ctx_row = B_lat
    n_rows = 8
    cc = jnp.concatenate([c, c_ctx[None], jnp.zeros((n_rows - B_lat - 1, D_MODEL), F32)], axis=0)
    mods = _modulation(cc, mod_w, mod_b).reshape(DEPTH, n_rows, N_MOD, D_MODEL)
    lat_row = lambda b: b
    ctx_mod_row = lambda b: ctx_row

    j = 0
    lam_init = 0.8 - 0.6 * math.exp(-0.3 * 0)
    w_in = mix_w_in[j].astype(BF16)
    w_out = mix_w_out[j].astype(BF16)
    w_glu = s5_w_glu[j].astype(BF16)
    ff = (ff_w1.astype(BF16), ff_w2.astype(BF16))
    s5w = _s5_weights(s5_lambda_re[j], s5_lambda_im[j], s5_log_dt[j], s5_b_re[j], s5_b_im[j],
                      s5_c_re[j], s5_c_im[j])

    qc, kc, vc, uc, kc32, vc32 = _premix(x_prompt, mods[0], ctx_mod_row, norm_g[0, 0], w_in,
                                         None, (4, 256), True)
    ql, kl, vl, ul = _premix(x_sample, mods[0], lat_row, norm_g[0, 0], w_in,
                             _rope_tables(L_lat), (1, 1024), False)

    ac = _attention(qc, kc, vc, None, diff_lambda_qk[j], diff_subln_g[j], lam_init, 256, DA_HEADS)
    P = cache_k.shape[2]
    cache = (cache_k[:, j].reshape(B_lat, P, DA_WIDTH), cache_v[:, j].reshape(B_lat, P, DA_WIDTH))
    al = _attention(ql, kl, vl, cache, diff_lambda_qk[j], diff_subln_g[j], lam_init, 1024, 1)

    lat_split = 1
    h0_ctx = jnp.zeros((2, S5_OCT, 1, B_ctx, S5_SW), F32)
    h0_lat = state_s5[:, j].reshape(B_lat, 2, S5_OCT, S5_GPO, S5_STATE, 2)
    h0_lat = h0_lat.transpose(1, 2, 0, 5, 3, 4).reshape(
        2, S5_OCT, lat_split, B_lat // lat_split, S5_SW)
    gc, fin_c = _s5_mixer(uc, s5w, h0_ctx, s5_d[j], B_ctx, 1)
    gl, _ = _s5_mixer(ul, s5w, h0_lat, s5_d[j], B_lat, lat_split)

    ctx = _post0(x_prompt, ac, gc, mods[0], ctx_mod_row, norm_g[0, 1], w_glu, w_out, ff, 0,
                 CTX_TILING)
    lat = _post0(x_sample, al, gl, mods[0], lat_row, norm_g[0, 1], w_glu, w_out, ff, 0,
                 LAT_TILING)

    pw = pool_w[0].astype(BF16)
    y_prompt = _layer1(ctx, mods[1], ctx_mod_row, norm_g[1], pw, pool_scale[0], ff, 1,
                       final_norm_g, CTX_POOL_TILING)
    y_sample = _layer1(lat, mods[1], lat_row, norm_g[1], pw, pool_scale[0], ff, 1,
                       final_norm_g, LAT_POOL_TILING)

    new_cache_k = kc32.reshape(B_ctx, 1, L_ctx, DA_HEADS, 2 * DA_QK)
    new_cache_v = vc32.reshape(B_ctx, 1, L_ctx, DA_HEADS, DA_V)
    st = fin_c.reshape(2, S5_OCT, B_ctx, 2, S5_GPO, S5_STATE)
    new_state = st.transpose(2, 0, 1, 4, 5, 3).reshape(B_ctx, 1, 2, S5_GROUPS, S5_STATE, 2)
    return (y_prompt, y_sample, new_cache_k, new_cache_v, new_state)
```

```python
import functools
import math

import jax
import jax.numpy as jnp
import numpy as np
from jax import lax
from jax.experimental import pallas as pl
from jax.experimental.pallas import tpu as pltpu

D_MODEL = 1024
DEPTH = 2
GRID_W = 64
DA_HEADS = 4
DA_QK = 64
DA_V = 2 * DA_QK
DA_WIDTH = DA_HEADS * DA_V
S5_WIDTH = D_MODEL - DA_WIDTH
S5_CH = 16
S5_GROUPS = S5_WIDTH // S5_CH
S5_STATE = 64
IN_WIDTH = 3 * DA_WIDTH + S5_WIDTH
POOL_WINDOWS = (2, 4, 8, 16)
POOL_CH = D_MODEL // len(POOL_WINDOWS)
FF_HIDDEN = 4 * D_MODEL
N_MOD = 6
ROPE_BASE = 10000.0
EPS = 1e-6

SUBLANES = 8
LANES = 128
S5_T = 8
S5_OCT = S5_WIDTH // LANES
S5_GPO = LANES // S5_CH
S5_ROWS = 512
S5_SW = 2 * S5_GPO * S5_STATE
Q_SCALE = DA_QK ** -0.5 * math.log2(math.e)
CTX_TILING = (2, 256, 256)
LAT_TILING = (1, 1024, 512)
CTX_POOL_TILING = (2, 256)
LAT_POOL_TILING = (1, 512)
PREMIX_SUB = 512
ATTN_SUB = 256
POOL_HALO = 16
FF_CHUNK = 1024
VMEM_LIMIT = 56 * 1024 * 1024

F32 = jnp.float32
BF16 = jnp.bfloat16


def _const_spec(shape):
    nd = len(shape)
    return pl.BlockSpec(shape, lambda *_: (0,) * nd, pipeline_mode=pl.Buffered(1))


def _layer_spec(shape, layer):
    return pl.BlockSpec((None,) + shape, lambda *_: (layer, 0, 0), pipeline_mode=pl.Buffered(1))


def _params(n_axes):
    return pltpu.CompilerParams(dimension_semantics=("arbitrary",) * n_axes,
                                vmem_limit_bytes=VMEM_LIMIT)


def _rms(x, g):
    return x * lax.rsqrt(jnp.mean(x * x, axis=-1, keepdims=True) + EPS) * g


def _dot(a, b):
    return jnp.dot(a, b, preferred_element_type=F32)


def _mod_kernel(c_ref, w_ref, b_ref, o_ref):
    c = c_ref[...]
    s = (c * jax.nn.sigmoid(c)).astype(BF16)
    o_ref[0] = _dot(s, w_ref[0].astype(BF16)) + b_ref[0]


def _modulation(cc, mod_w, mod_b):
    rows = cc.shape[0]
    tn = 1536
    n = N_MOD * D_MODEL
    return pl.pallas_call(
        _mod_kernel,
        grid=(DEPTH, n // tn),
        in_specs=[pl.BlockSpec((rows, D_MODEL), lambda l, j: (0, 0)),
                  pl.BlockSpec((1, D_MODEL, tn), lambda l, j: (l, 0, j)),
                  pl.BlockSpec((1, 1, tn), lambda l, j: (l, 0, j))],
        out_specs=pl.BlockSpec((1, rows, tn), lambda l, j: (l, 0, j)),
        out_shape=jax.ShapeDtypeStruct((DEPTH, rows, n), F32),
        compiler_params=_params(2),
        name="modulation",
    )(cc, mod_w, mod_b.reshape(DEPTH, 1, n))


def _premix_kernel(*refs, rope, cache_out):
    x_ref, mod_ref, g_ref, w_ref = refs[:4]
    refs = refs[4:]
    if rope:
        cos_ref, sa_ref, sb_ref = refs[:3]
        refs = refs[3:]
    q_ref, k_ref, v_ref, u_ref = refs[:4]
    if cache_out:
        kf_ref, vf_ref = refs[4:6]
    ubuf_ref = refs[-1]
    bb, tm = x_ref.shape[0], x_ref.shape[1]
    sub = PREMIX_SUB
    cps = sub // S5_T

    def store(ref, r0, c0, val):
        piece = min(sub, tm)
        for p in range(sub // piece):
            bl, t0 = divmod(r0 + p * piece, tm)
            ref[bl, t0:t0 + piece, c0:c0 + val.shape[1]] = val[p * piece:(p + 1) * piece]

    def store_head(ref, r0, head, val):
        piece = min(sub, tm)
        for p in range(sub // piece):
            bl, t0 = divmod(r0 + p * piece, tm)
            ref[bl, 0, t0:t0 + piece, head, :] = val[p * piece:(p + 1) * piece]

    def problem(k):
        r0 = k * sub
        piece = min(sub, tm)
        x = jnp.concatenate([x_ref[divmod(r0 + p * piece, tm)[0],
                                   pl.ds(divmod(r0 + p * piece, tm)[1], piece), :]
                             for p in range(sub // piece)], axis=0)
        h = _rms(x, g_ref[...]) * (1.0 + mod_ref[0, 1:2, :]) + mod_ref[0, 0:1, :]
        hb = h.astype(BF16)
        yield
        z = _dot(hb, w_ref[:, 0:2 * DA_WIDTH])
        yield
        for j in range(2 * DA_HEADS):
            s = z[:, j * LANES:(j + 1) * LANES]
            if cache_out and j >= DA_HEADS:
                store_head(kf_ref, r0, j - DA_HEADS, s)
            if rope:
                t = slice(r0, r0 + sub)
                s = (s * cos_ref[t, :] + pltpu.roll(s, LANES - 16, axis=1) * sa_ref[t, :]
                     + pltpu.roll(s, 16, axis=1) * sb_ref[t, :])
            if j < DA_HEADS:
                store(q_ref, r0, j * LANES, (s * Q_SCALE).astype(BF16))
            else:
                store(k_ref, r0, (j - DA_HEADS) * LANES, s.astype(BF16))
        yield
        z = _dot(hb, w_ref[:, 2 * DA_WIDTH:IN_WIDTH])
        yield
        v = z[:, 0:DA_WIDTH]
        store(v_ref, r0, 0, v.astype(BF16))
        if cache_out:
            for hd in range(DA_HEADS):
                store_head(vf_ref, r0, hd, v[:, hd * LANES:(hd + 1) * LANES])
        for o in range(S5_OCT):
            ubuf_ref[k, o] = z[:, DA_WIDTH + o * LANES:DA_WIDTH + (o + 1) * LANES]
        for o in range(S5_OCT):
            for s in range(S5_T):
                c0 = (o * S5_T + s) * LANES
                u_ref[k * cps:(k + 1) * cps, c0:c0 + LANES] = (
                    ubuf_ref[k, o, pl.ds(s, cps, stride=S5_T), :])

    _interleave(*[(problem(k), 1, k) for k in range(bb * tm // sub)])


def _chunk_spec(L, bb, tm):
    assert bb == 1 or tm == L
    return pl.BlockSpec((bb * tm // S5_T, S5_WIDTH * S5_T), lambda b, i: (b * (L // tm) + i, 0))


def _premix(x, mods, mod_row, norm_g, w_in, rope_tabs, tiling, cache_out):
    B, L, _ = x.shape
    bb, tm = tiling
    rope = rope_tabs is not None
    row_spec = lambda w: pl.BlockSpec((bb, tm, w), lambda b, i: (b, i, 0))
    in_specs = [row_spec(D_MODEL),
                pl.BlockSpec((1, N_MOD, D_MODEL), lambda b, i: (mod_row(b), 0, 0)),
                _const_spec((1, D_MODEL)),
                _const_spec((D_MODEL, IN_WIDTH))]
    args = [x, mods, norm_g.reshape(1, D_MODEL), w_in]
    if rope:
        in_specs += [pl.BlockSpec((tm, LANES), lambda b, i: (i, 0))] * 3
        args += list(rope_tabs)
    out_specs = [row_spec(DA_WIDTH)] * 3 + [_chunk_spec(L, bb, tm)]
    out_shape = [jax.ShapeDtypeStruct((B, L, DA_WIDTH), BF16)] * 3 + [
        jax.ShapeDtypeStruct((B * L // S5_T, S5_WIDTH * S5_T), F32)]
    if cache_out:
        out_specs += [pl.BlockSpec((bb, 1, tm, DA_HEADS, LANES), lambda b, i: (b, 0, i, 0, 0))] * 2
        out_shape += [jax.ShapeDtypeStruct((B, 1, L, DA_HEADS, LANES), F32)] * 2
    return pl.pallas_call(
        functools.partial(_premix_kernel, rope=rope, cache_out=cache_out),
        grid=(B // bb, L // tm),
        in_specs=in_specs, out_specs=out_specs, out_shape=out_shape,
        scratch_shapes=[pltpu.VMEM((bb * tm // PREMIX_SUB, S5_OCT, PREMIX_SUB, LANES), F32)],
        compiler_params=_params(2),
        name="premix_rope" if rope else "premix",
    )(*args)


def _rope_tables(L):
    rows = L // GRID_W
    row = np.repeat(np.arange(rows), GRID_W).astype(np.float64)
    col = np.tile(np.arange(GRID_W), rows).astype(np.float64)
    half = DA_QK // 2
    inv = 1.0 / (ROPE_BASE ** (np.arange(0, half, 2, dtype=np.float64) / half))
    ang_r, ang_c = row[:, None] * inv, col[:, None] * inv
    zero = np.zeros_like(ang_r)
    cos64 = np.concatenate([np.cos(ang_r)] * 2 + [np.cos(ang_c)] * 2, axis=1)
    sa64 = np.concatenate([-np.sin(ang_r), zero, -np.sin(ang_c), zero], axis=1)
    sb64 = np.concatenate([zero, np.sin(ang_r), zero, np.sin(ang_c)], axis=1)
    return tuple(jnp.asarray(np.tile(t, (1, LANES // DA_QK)), dtype=F32)
                 for t in (cos64, sa64, sb64))


def _attn_kernel(*refs, lam_init, has_cache):
    lqk_ref, g_ref, q_ref, k_ref, v_ref = refs[:5]
    if has_cache:
        ck_ref, cv_ref, o_ref = refs[5:8]
    else:
        o_ref = refs[5]
    lqk = lqk_ref[...]
    lam = (jnp.exp(jnp.sum(lqk[0:1] * lqk[1:2], axis=-1, keepdims=True))
           - jnp.exp(jnp.sum(lqk[2:3] * lqk[3:4], axis=-1, keepdims=True)) + lam_init)
    tq = q_ref.shape[1]
    lane = lax.broadcasted_iota(jnp.int32, (ATTN_SUB, LANES), 1)
    zero = jnp.zeros((ATTN_SUB, LANES), BF16)
    nt = (((1,), (1,)), ((), ()))
    mxu_sum = has_cache
    ones = lambda v: jnp.concatenate([v, jnp.ones_like(v)], axis=1) if mxu_sum else v

    def problem(cols, rows, segs):
        q = q_ref[0, rows, cols]
        outs = []
        for qm in (jnp.where(lane < DA_QK, q, zero), jnp.where(lane >= DA_QK, q, zero)):
            ss = [lax.dot_general(qm, k, nt, preferred_element_type=F32) for k, _ in segs]
            yield
            m = functools.reduce(jnp.maximum, [jnp.max(s, axis=-1, keepdims=True) for s in ss])
            es = [jnp.exp2(s - m) for s in ss]
            if not mxu_sum:
                l = functools.reduce(jnp.add, [jnp.sum(e, axis=-1, keepdims=True) for e in es])
            es = [e.astype(BF16) for e in es]
            yield
            o = functools.reduce(jnp.add, [_dot(e, v) for e, (_, v) in zip(es, segs)])
            if mxu_sum:
                l = o[:, DA_V:DA_V + 1]
            outs.append(o[:, :DA_V] / l)
            yield
        o = outs[0] - lam * outs[1]
        o_ref[0, rows, cols] = (_rms(o, g_ref[...]) * (1.0 - lam_init)).astype(BF16)

    problems = []
    for hh in range(q_ref.shape[2] // LANES):
        cols = slice(hh * LANES, (hh + 1) * LANES)
        segs = [(k_ref[0, :, cols], ones(v_ref[0, :, cols]))]
        if has_cache:
            segs.append((ck_ref[0, :, cols].astype(BF16), ones(cv_ref[0, :, cols].astype(BF16))))
        for r in range(tq // ATTN_SUB):
            problems.append(problem(cols, slice(r * ATTN_SUB, (r + 1) * ATTN_SUB), segs))
    _interleave(*[(p, 1, i) for i, p in enumerate(problems)])


def _attention(q, k, v, cache, lambda_qk, subln_g, lam_init, tq, heads):
    B, L, _ = q.shape
    has_cache = cache is not None
    q_spec = pl.BlockSpec((1, tq, heads * LANES), lambda b, h, i: (b, i, h))
    kv_spec = lambda n: pl.BlockSpec((1, n, heads * LANES), lambda b, h, i: (b, 0, h))
    in_specs = [_const_spec((4, DA_QK)), _const_spec((1, DA_V)), q_spec, kv_spec(L), kv_spec(L)]
    args = [lambda_qk, subln_g.reshape(1, DA_V), q, k, v]
    if has_cache:
        P = cache[0].shape[1]
        in_specs += [kv_spec(P), kv_spec(P)]
        args += list(cache)
    return pl.pallas_call(
        functools.partial(_attn_kernel, lam_init=lam_init, has_cache=has_cache),
        grid=(B, DA_HEADS // heads, L // tq),
        in_specs=in_specs, out_specs=q_spec,
        out_shape=jax.ShapeDtypeStruct((B, L, DA_WIDTH), BF16),
        compiler_params=_params(3),
        name="diff_attn_cache" if has_cache else "diff_attn",
    )(*args)


def _s5_weights(lam_re, lam_im, log_dt, b_re, b_im, c_re, c_im):
    T, G, P, H = S5_T, S5_GROUPS, S5_STATE, S5_CH
    dt = jnp.exp(log_dt)[..., None]
    lr, li = lam_re * dt, lam_im * dt
    kk = jnp.arange(T + 1, dtype=F32)[None, :, None, None]
    mag = jnp.exp(kk * lr[:, None])
    pw_re = mag * jnp.cos(kk * li[:, None])
    pw_im = mag * jnp.sin(kk * li[:, None])
    a_re, a_im = pw_re[:, 1], pw_im[:, 1]
    den = lam_re * lam_re + lam_im * lam_im
    f_re = ((a_re - 1.0) * lam_re + a_im * lam_im) / den
    f_im = (a_im * lam_re - (a_re - 1.0) * lam_im) / den
    bt_re, bt_im = jnp.swapaxes(b_re, -1, -2), jnp.swapaxes(b_im, -1, -2)
    bb_re = f_re[:, :, None, :] * bt_re - f_im[:, :, None, :] * bt_im
    bb_im = f_re[:, :, None, :] * bt_im + f_im[:, :, None, :] * bt_re

    def per_dir(x, fwd, bwd):
        return jnp.stack([fwd(x[0]), bwd(x[1])], axis=0)

    flip = lambda x: x[::-1]
    pr = per_dir(pw_re, lambda x: flip(x[:T]), lambda x: x[:T])[:, :, :, None, :]
    pi = per_dir(pw_im, lambda x: flip(x[:T]), lambda x: x[:T])[:, :, :, None, :]
    br, bi = bb_re[:, None], bb_im[:, None]
    fc = jnp.stack([pr * br - pi * bi, pr * bi + pi * br], axis=1)
    fc = fc.reshape(2, 2, T, S5_OCT, LANES, P)
    fc = jnp.concatenate([fc, fc], axis=-1)

    def lanes_gh(x):
        x = x.reshape(x.shape[:-2] + (S5_OCT, S5_GPO, P))
        return jnp.repeat(jnp.swapaxes(x, -1, -2), H, axis=-1)

    def c_lanes(c):
        c = c.reshape(2, S5_OCT, S5_GPO, H, P).transpose(0, 1, 4, 2, 3)
        return c.reshape(2, 1, S5_OCT, P, LANES)

    fwd_pows = lambda x: jnp.concatenate([x[1:], x[:1]])
    bwd_pows = lambda x: jnp.concatenate([flip(x[1:]), x[:1]])
    per = lanes_gh(per_dir(pw_re, fwd_pows, bwd_pows))
    pei = lanes_gh(per_dir(pw_im, fwd_pows, bwd_pows))
    cr, ci = c_lanes(c_re), c_lanes(c_im)
    ec = jnp.stack([cr * per - ci * pei, -(cr * pei + ci * per)], axis=1)
    ec = jnp.concatenate([ec, ec], axis=-2)

    at = jnp.stack([pw_re[:, T], pw_im[:, T]], axis=1)
    at = at.reshape(2, 2, S5_OCT, S5_GPO * P).transpose(0, 2, 1, 3)
    return fc.astype(BF16), ec.astype(BF16), at


def _s5_kernel(*refs, n_seq, n_chunk):
    T = S5_T
    (x_ref, fc_ref, ec_ref, at_ref, h0_ref, d_ref, out_ref, fin_ref,
     sf_ref, sb_ref, wt_ref, wf_ref, we_ref, c0_ref) = refs
    n_tile = S5_SW // LANES
    nh = n_tile // 2

    @pl.when(pl.program_id(1) == 0)
    def _():
        _s5_expand(fc_ref, ec_ref, wt_ref, wf_ref, we_ref, c0_ref)

    M = x_ref.shape[0]
    blocks = [slice(r, r + S5_ROWS) for r in range(0, M, S5_ROWS)]
    seq_pad = sf_ref.shape[1] // n_chunk

    def seq_rows(rows):
        return [(r // n_chunk, slice(r - rows.start, r - rows.start + n_chunk))
                for r in range(rows.start, rows.stop, n_chunk)]

    def to_tiles(ref, rows, val):
        for b, local in seq_rows(rows):
            for j in range(n_tile):
                ref[j, pl.ds(b, n_chunk, stride=seq_pad), :] = val[local, j * LANES:(j + 1) * LANES]

    def from_tiles(ref, rows):
        return jnp.concatenate(
            [jnp.concatenate([ref[j, pl.ds(b, n_chunk, stride=seq_pad), :].astype(BF16)
                              for j in range(n_tile)], axis=1)
             for b, _ in seq_rows(rows)], axis=0)

    for rows in blocks:
        xb = x_ref[rows, :].astype(BF16)
        to_tiles(sf_ref, rows, _dot(xb, wf_ref[0]))
        to_tiles(sb_ref, rows, _dot(xb, wf_ref[1]))

    def coeff(d):
        return [(at_ref[d, 0, 0:1, j * LANES:(j + 1) * LANES],
                 at_ref[d, 0, 1:2, j * LANES:(j + 1) * LANES]) for j in range(nh)]

    def advance(ref, rows, a, state):
        out = []
        for j in range(nh):
            (ar, ai), (hr, hi) = a[j], state[j]
            sr, si = ref[j, rows, :], ref[nh + j, rows, :]
            ref[j, rows, :] = hr
            ref[nh + j, rows, :] = hi
            out.append((ar * hr - ai * hi + sr, ar * hi + ai * hr + si))
        return out

    af, ab = coeff(0), coeff(1)

    def step(i, carry):
        hf, hb = carry
        fwd = pl.multiple_of(i * seq_pad, seq_pad)
        bwd = pl.multiple_of((n_chunk - 1 - i) * seq_pad, seq_pad)
        hf = advance(sf_ref, pl.ds(fwd, n_seq), af, hf)
        hb = advance(sb_ref, pl.ds(bwd, n_seq), ab, hb)
        return hf, hb

    def initial(d):
        return [(h0_ref[d, 0, 0, :, j * LANES:(j + 1) * LANES],
                 h0_ref[d, 0, 0, :, (nh + j) * LANES:(nh + j + 1) * LANES]) for j in range(nh)]

    finals = lax.fori_loop(0, n_chunk, step, (initial(0), initial(1)))
    for d in range(2):
        for j in range(nh):
            fin_ref[d, 0, 0, :, j * LANES:(j + 1) * LANES] = finals[d][j][0]
            fin_ref[d, 0, 0, :, (nh + j) * LANES:(nh + j + 1) * LANES] = finals[d][j][1]

    d = jnp.concatenate([d_ref[0]] * T, axis=1)
    for rows in blocks:
        x = x_ref[rows, :]
        y = (_dot(x.astype(BF16), wt_ref[...]) + _dot(from_tiles(sf_ref, rows), we_ref[0])
             + _dot(from_tiles(sb_ref, rows), we_ref[1]))
        out_ref[rows, :] = jax.nn.gelu(y + d * x)


def _s5_expand(fc_ref, ec_ref, wt_ref, wf_ref, we_ref, c0_ref):
    T = S5_T
    nh = S5_SW // LANES // 2
    row = lax.broadcasted_iota(jnp.int32, (LANES, LANES), 0)
    lane = lax.broadcasted_iota(jnp.int32, (LANES, LANES), 1)
    gpt = LANES // S5_STATE
    own_f = [(row // S5_CH) == (gpt * j + lane // S5_STATE) for j in range(nh)]
    own_e = [(lane // S5_CH) == (gpt * j + row // S5_STATE) for j in range(nh)]
    for d in range(2):
        for ri in range(2):
            for s in range(T):
                f, e = fc_ref[d, ri, s, 0].astype(F32), ec_ref[d, ri, s, 0].astype(F32)
                rows = slice(s * LANES, (s + 1) * LANES)
                for j in range(nh):
                    cols = slice((ri * nh + j) * LANES, (ri * nh + j + 1) * LANES)
                    wf_ref[d, rows, cols] = jnp.where(own_f[j], f, 0.0).astype(BF16)
                    we_ref[d, cols, rows] = jnp.where(own_e[j], e, 0.0).astype(BF16)
            e0 = ec_ref[d, ri, T, 0].astype(F32)
            for j in range(nh):
                cols = slice((ri * nh + j) * LANES, (ri * nh + j + 1) * LANES)
                c0_ref[d, cols, :] = jnp.where(own_e[j], e0, 0.0).astype(BF16)

    bbar = (wf_ref[0, (T - 1) * LANES:T * LANES, :], wf_ref[1, 0:LANES, :])

    def lag_kernel(d, tau):
        if tau == 0:
            return _dot(bbar[d], c0_ref[d])
        t = tau - 1 if d == 0 else T - tau
        return _dot(bbar[d], we_ref[d, :, t * LANES:(t + 1) * LANES])

    lag_tiles = ([lag_kernel(1, T - 1 - l) for l in range(T - 1)]
                 + [lag_kernel(0, 0) + lag_kernel(1, 0)]
                 + [lag_kernel(0, l) for l in range(1, T)])
    for s in range(T):
        for t in range(T):
            wt_ref[s * LANES:(s + 1) * LANES, t * LANES:(t + 1) * LANES] = (
                lag_tiles[t - s + T - 1].astype(BF16))


def _s5_mixer(u, weights, h0, d_skip, n_seq, n_split):
    T = S5_T
    M_all = u.shape[0]
    M = M_all // n_split
    B = n_seq // n_split
    n_chunk = M // B
    seq_pad = -(-B // SUBLANES) * SUBLANES
    fc, ec, at = weights
    TL = T * LANES
    x_spec = pl.BlockSpec((M, TL), lambda o, j: (j, o))
    tile_spec = lambda n: pl.BlockSpec((2, 2, n, 1, LANES, LANES), lambda o, j: (0, 0, 0, o, 0, 0))
    st_spec = pl.BlockSpec((2, 1, 1, B, S5_SW), lambda o, j: (0, o, j, 0, 0))
    g, fin = pl.pallas_call(
        functools.partial(_s5_kernel, n_seq=B, n_chunk=n_chunk),
        grid=(S5_OCT, n_split),
        in_specs=[x_spec,
                  tile_spec(T), tile_spec(T + 1),
                  pl.BlockSpec((2, 1, 2, S5_SW // 2), lambda o, j: (0, o, 0, 0)),
                  st_spec,
                  pl.BlockSpec((1, 1, LANES), lambda o, j: (o, 0, 0))],
        out_specs=[x_spec, st_spec],
        out_shape=[jax.ShapeDtypeStruct((M_all, S5_OCT * TL), F32),
                   jax.ShapeDtypeStruct((2, S5_OCT, n_split, B, S5_SW), F32)],
        scratch_shapes=[pltpu.VMEM((S5_SW // LANES, n_chunk * seq_pad, LANES), F32)] * 2 + [
            pltpu.VMEM((TL, TL), BF16), pltpu.VMEM((2, TL, S5_SW), BF16),
            pltpu.VMEM((2, S5_SW, TL), BF16), pltpu.VMEM((2, S5_SW, LANES), BF16)],
        compiler_params=_params(2),
        name="s5_mixer",
    )(u, fc, ec, at, h0, d_skip.reshape(S5_OCT, 1, LANES))
    return g, fin


def _mlp(h, w1_ref, w2_ref):
    hb = h.astype(BF16)
    acc = None
    for j in range(FF_HIDDEN // FF_CHUNK):
        a = _dot(hb, w1_ref[:, j * FF_CHUNK:(j + 1) * FF_CHUNK])
        a = jnp.square(jnp.maximum(a, 0.0)).astype(BF16)
        p = _dot(a, w2_ref[j * FF_CHUNK:(j + 1) * FF_CHUNK, :])
        acc = p if acc is None else acc + p
    return acc


def _interleave(*staged):
    live = list(staged)
    rnd = 0
    while live:
        for item in list(live):
            gen, per_round, first = item
            if rnd < first:
                continue
            for _ in range(per_round):
                if next(gen, _DONE) is _DONE:
                    live.remove(item)
                    break
        rnd += 1


_DONE = object()


def _post0_kernel(x_ref, attn_ref, g_ref, mod_ref, ng_ref, wglu_ref, wout_ref, w1_ref, w2_ref, o_ref,
                  gbuf_ref, *, sub):
    bb, tm = x_ref.shape[0], x_ref.shape[1]
    n_sub = tm // sub
    cps = sub // S5_T
    for k in range(bb * n_sub):
        bl, rows = k // n_sub, slice((k % n_sub) * sub, (k % n_sub + 1) * sub)
        for o in range(S5_OCT):
            for s in range(S5_T):
                c0 = (o * S5_T + s) * LANES
                gbuf_ref[k, o, pl.ds(s, cps, stride=S5_T), :] = (
                    g_ref[k * cps:(k + 1) * cps, c0:c0 + LANES])
        g = jnp.concatenate([gbuf_ref[k, o] for o in range(S5_OCT)], axis=1)
        s5 = g * jax.nn.sigmoid(_dot(g.astype(BF16), wglu_ref[...]))
        oc = (_dot(attn_ref[bl, rows], wout_ref[0:DA_WIDTH, :])
              + _dot(s5.astype(BF16), wout_ref[DA_WIDTH:D_MODEL, :]))
        x1 = x_ref[bl, rows] + mod_ref[0, 2:3, :] * oc
        h = _rms(x1, ng_ref[...]) * (1.0 + mod_ref[0, 4:5, :]) + mod_ref[0, 3:4, :]
        o_ref[bl, rows] = x1 + mod_ref[0, 5:6, :] * _mlp(h, w1_ref, w2_ref)


def _post0(x, attn, g, mods, mod_row, norm_g, w_glu, w_out, ff, layer, tiling):
    B, L, _ = x.shape
    bb, tm, sub = tiling
    w1, w2 = ff
    row_spec = lambda w: pl.BlockSpec((bb, tm, w), lambda b, i: (b, i, 0))
    return pl.pallas_call(
        functools.partial(_post0_kernel, sub=sub),
        grid=(B // bb, L // tm),
        in_specs=[row_spec(D_MODEL), row_spec(DA_WIDTH), _chunk_spec(L, bb, tm),
                  pl.BlockSpec((1, N_MOD, D_MODEL), lambda b, i: (mod_row(b), 0, 0)),
                  _const_spec((1, D_MODEL)),
                  _const_spec((S5_WIDTH, S5_WIDTH)),
                  _const_spec((D_MODEL, D_MODEL)),
                  _layer_spec((D_MODEL, FF_HIDDEN), layer),
                  _layer_spec((FF_HIDDEN, D_MODEL), layer)],
        out_specs=row_spec(D_MODEL),
        out_shape=jax.ShapeDtypeStruct((B, L, D_MODEL), F32),
        scratch_shapes=[pltpu.VMEM((bb * tm // sub, S5_OCT, sub, LANES), F32)],
        compiler_params=_params(2),
        name="mix_out_mlp",
    )(x, attn, g, mods, norm_g.reshape(1, D_MODEL), w_glu, w_out, w1, w2)


def _layer1_kernel(x_ref, xp_ref, xn_ref, mod_ref, modp_ref, ng1_ref, ng2_ref, pw_ref, ps_ref,
                   w1_ref, w2_ref, fg_ref, o_ref, x1a_ref, ha_ref, x1b_ref, hb_ref,
                   *, seq_len, n_steps):
    s = pl.program_id(0)
    bb, tm = x_ref.shape[0], x_ref.shape[1]
    H = POOL_HALO
    n = tm + 2 * H
    tile = jnp.minimum(s, n_steps - 1)
    first = (tile % (seq_len // tm)) * tm

    def pool(x1_ref, h_ref):
        for bl in range(bb):
            x = x_ref[bl]
            xe = jnp.concatenate([xp_ref[bl], x, xn_ref[bl]], axis=0)
            he = _rms(xe, ng1_ref[...]) * (1.0 + mod_ref[0, 1:2, :]) + mod_ref[0, 0:1, :]
            pos_e = first - H + lax.broadcasted_iota(jnp.int32, (n, 1), 0)
            he = jnp.where((pos_e >= 0) & (pos_e < seq_len), he, 0.0)
            pos = first + lax.broadcasted_iota(jnp.int32, (tm, 1), 0)
            yield
            zs = []
            for gi, w in enumerate(POOL_WINDOWS):
                hg = he[:, gi * POOL_CH:(gi + 1) * POOL_CH]
                half = w // 2
                fwd, k = hg, 1
                while 2 * k < w:
                    fwd = fwd + pltpu.roll(fwd, n - k, axis=0)
                    k *= 2
                if half % SUBLANES == 0:
                    total = fwd[H - half:H - half + tm] + fwd[H:H + tm]
                else:
                    total = (fwd + pltpu.roll(fwd, half, axis=0))[H:H + tm]
                count = jnp.clip(pos + half, 0, seq_len) - jnp.clip(pos - half, 0, seq_len)
                z = total * (1.0 / count.astype(F32)) - hg[H:H + tm]
                zs.append(_dot(z.astype(BF16), pw_ref[gi]))
                yield
            oc = jnp.concatenate(zs, axis=1) * ps_ref[...]
            x1 = x + mod_ref[0, 2:3, :] * oc
            h = _rms(x1, ng2_ref[...]) * (1.0 + mod_ref[0, 4:5, :]) + mod_ref[0, 3:4, :]
            x1_ref[bl * tm:(bl + 1) * tm, :] = x1
            h_ref[bl * tm:(bl + 1) * tm, :] = h.astype(BF16)
            yield

    def mlp(x1_ref, h_ref):
        hb = h_ref[...]
        acc = None
        for j in range(FF_HIDDEN // FF_CHUNK):
            a = _dot(hb, w1_ref[:, j * FF_CHUNK:(j + 1) * FF_CHUNK])
            a = jnp.square(jnp.maximum(a, 0.0)).astype(BF16)
            yield
            p = _dot(a, w2_ref[j * FF_CHUNK:(j + 1) * FF_CHUNK, :])
            acc = p if acc is None else acc + p
            yield
        y = _rms(x1_ref[...] + modp_ref[0, 5:6, :] * acc, fg_ref[...])
        for bl in range(bb):
            o_ref[bl] = y[bl * tm:(bl + 1) * tm]

    pool_per_mlp = bb

    @pl.when(s == 0)
    def _():
        _interleave((pool(x1a_ref, ha_ref), 1, 0))

    @pl.when((s > 0) & (s % 2 == 1))
    def _():
        _interleave((mlp(x1a_ref, ha_ref), 1, 0), (pool(x1b_ref, hb_ref), pool_per_mlp, 0))

    @pl.when((s > 0) & (s % 2 == 0))
    def _():
        _interleave((mlp(x1b_ref, hb_ref), 1, 0), (pool(x1a_ref, ha_ref), pool_per_mlp, 0))


def _layer1(x, mods, mod_row, norm_g, pool_w, pool_scale, ff, layer, final_g, tiling):
    B, L, _ = x.shape
    bb, tm = tiling
    w1, w2 = ff
    H = POOL_HALO
    per = tm // H
    last = L // H - 1
    n_i = L // tm
    n_steps = (B // bb) * n_i

    def at(shift, fn):
        def index(s):
            t = jnp.clip(s - shift, 0, n_steps - 1)
            return fn(t // n_i, t % n_i)
        return index

    row_map = lambda b, i: (b, i, 0)
    mod_map = lambda b, i: (mod_row(b), 0, 0)
    rows = bb * tm
    return pl.pallas_call(
        functools.partial(_layer1_kernel, seq_len=L, n_steps=n_steps),
        grid=(n_steps + 1,),
        in_specs=[pl.BlockSpec((bb, tm, D_MODEL), at(0, row_map)),
                  pl.BlockSpec((bb, H, D_MODEL),
                               at(0, lambda b, i: (b, jnp.maximum(i * per - 1, 0), 0))),
                  pl.BlockSpec((bb, H, D_MODEL),
                               at(0, lambda b, i: (b, jnp.minimum((i + 1) * per, last), 0))),
                  pl.BlockSpec((1, N_MOD, D_MODEL), at(0, mod_map)),
                  pl.BlockSpec((1, N_MOD, D_MODEL), at(1, mod_map)),
                  _const_spec((1, D_MODEL)), _const_spec((1, D_MODEL)),
                  _const_spec((len(POOL_WINDOWS), POOL_CH, POOL_CH)),
                  _const_spec((1, D_MODEL)),
                  _layer_spec((D_MODEL, FF_HIDDEN), layer),
                  _layer_spec((FF_HIDDEN, D_MODEL), layer),
                  _const_spec((1, D_MODEL))],
        out_specs=pl.BlockSpec((bb, tm, D_MODEL), at(1, row_map)),
        out_shape=jax.ShapeDtypeStruct((B, L, D_MODEL), F32),
        scratch_shapes=[pltpu.VMEM((rows, D_MODEL), F32), pltpu.VMEM((rows, D_MODEL), BF16)] * 2,
        compiler_params=_params(1),
        name="pool_mlp_norm",
    )(x, x, x, mods, mods, norm_g[0].reshape(1, D_MODEL), norm_g[1].reshape(1, D_MODEL),
      pool_w, pool_scale.reshape(1, D_MODEL), w1, w2, final_g.reshape(1, D_MODEL))


def kernel(x_prompt, x_sample, cache_k, cache_v, state_s5, c, c_ctx, mod_w, mod_b, norm_g, mix_w_in, mix_w_out, diff_lambda_qk, diff_subln_g, s5_lambda_re, s5_lambda_im, s5_log_dt, s5_b_re, s5_b_im, s5_c_re, s5_c_im, s5_d, s5_w_glu, pool_w, pool_scale, ff_w1, ff_w2, final_norm_g):
    B_ctx, L_ctx, _ = x_prompt.shape
    B_lat, L_lat, _ = x_sample.shape
    ctx_row = B_lat
    n_rows = 8
    cc = jnp.concatenate([c, c_ctx[None], jnp.zeros((n_rows - B_lat - 1, D_MODEL), F32)], axis=0)
    mods = _modulation(cc, mod_w, mod_b).reshape(DEPTH, n_rows, N_MOD, D_MODEL)
    lat_row = lambda b: b
    ctx_mod_row = lambda b: ctx_row

    j = 0
    lam_init = 0.8 - 0.6 * math.exp(-0.3 * 0)
    w_in = mix_w_in[j].astype(BF16)
    w_out = mix_w_out[j].astype(BF16)
    w_glu = s5_w_glu[j].astype(BF16)
    ff = (ff_w1.astype(BF16), ff_w2.astype(BF16))
    s5w = _s5_weights(s5_lambda_re[j], s5_lambda_im[j], s5_log_dt[j], s5_b_re[j], s5_b_im[j],
                      s5_c_re[j], s5_c_im[j])

    qc, kc, vc, uc, kc32, vc32 = _premix(x_prompt, mods[0], ctx_mod_row, norm_g[0, 0], w_in,
                                         None, (4, 256), True)
    ql, kl, vl, ul = _premix(x_sample, mods[0], lat_row, norm_g[0, 0], w_in,
                             _rope_tables(L_lat), (1, 1024), False)

    ac = _attention(qc, kc, vc, None, diff_lambda_qk[j], diff_subln_g[j], lam_init, 256, DA_HEADS)
    P = cache_k.shape[2]
    cache = (cache_k[:, j].reshape(B_lat, P, DA_WIDTH), cache_v[:, j].reshape(B_lat, P, DA_WIDTH))
    al = _attention(ql, kl, vl, cache, diff_lambda_qk[j], diff_subln_g[j], lam_init, 1024, 1)

    lat_split = 1
    h0_ctx = jnp.zeros((2, S5_OCT, 1, B_ctx, S5_SW), F32)
    h0_lat = state_s5[:, j].reshape(B_lat, 2, S5_OCT, S5_GPO, S5_STATE, 2)
    h0_lat = h0_lat.transpose(1, 2, 0, 5, 3, 4).reshape(
        2, S5_OCT, lat_split, B_lat // lat_split, S5_SW)
    gc, fin_c = _s5_mixer(uc, s5w, h0_ctx, s5_d[j], B_ctx, 1)
    gl, _ = _s5_mixer(ul, s5w, h0_lat, s5_d[j], B_lat, lat_split)

    ctx = _post0(x_prompt, ac, gc, mods[0], ctx_mod_row, norm_g[0, 1], w_glu, w_out, ff, 0,
                 CTX_TILING)
    lat = _post0(x_sample, al, gl, mods[0], lat_row, norm_g[0, 1], w_glu, w_out, ff, 0,
                 LAT_TILING)

    pw = pool_w[0].astype(BF16)
    y_prompt = _layer1(ctx, mods[1], ctx_mod_row, norm_g[1], pw, pool_scale[0], ff, 1,
                       final_norm_g, CTX_POOL_TILING)
    y_sample = _layer1(lat, mods[1], lat_row, norm_g[1], pw, pool_scale[0], ff, 1,
                       final_norm_g, LAT_POOL_TILING)

    new_cache_k, new_cache_v = kc32, vc32
    st = fin_c.reshape(2, S5_OCT, B_ctx, 2, S5_GPO, S5_STATE)
    new_state = st.transpose(2, 0, 1, 4, 5, 3).reshape(B_ctx, 1, 2, S5_GROUPS, S5_STATE, 2)
    return (y_prompt, y_sample, new_cache_k, new_cache_v, new_state)
```

```python
import functools
import math

import jax
import jax.numpy as jnp
import numpy as np
from jax import lax
from jax.experimental import pallas as pl
from jax.experimental.pallas import tpu as pltpu

D_MODEL = 1024
DEPTH = 2
GRID_W = 64
DA_HEADS = 4
DA_QK = 64
DA_V = 2 * DA_QK
DA_WIDTH = DA_HEADS * DA_V
S5_WIDTH = D_MODEL - DA_WIDTH
S5_CH = 16
S5_GROUPS = S5_WIDTH // S5_CH
S5_STATE = 64
IN_WIDTH = 3 * DA_WIDTH + S5_WIDTH
POOL_WINDOWS = (2, 4, 8, 16)
POOL_CH = D_MODEL // len(POOL_WINDOWS)
FF_HIDDEN = 4 * D_MODEL
N_MOD = 6
ROPE_BASE = 10000.0
EPS = 1e-6

SUBLANES = 8
LANES = 128
S5_T = 8
S5_OCT = S5_WIDTH // LANES
S5_GPO = LANES // S5_CH
S5_ROWS = 512
S5_SW = 2 * S5_GPO * S5_STATE
Q_SCALE = DA_QK ** -0.5 * math.log2(math.e)
CTX_TILING = (2, 256, 256)
LAT_TILING = (1, 1024, 1024)
CTX_POOL_TILING = (2, 256)
LAT_POOL_TILING = (1, 512)
PREMIX_SUB = 512
ATTN_SUB = 256
POOL_HALO = 16
FF_CHUNK = 1024
VMEM_LIMIT = 56 * 1024 * 1024

F32 = jnp.float32
BF16 = jnp.bfloat16


def _const_spec(shape):
    nd = len(shape)
    return pl.BlockSpec(shape, lambda *_: (0,) * nd, pipeline_mode=pl.Buffered(1))


def _layer_spec(shape, layer):
    return pl.BlockSpec((None,) + shape, lambda *_: (layer, 0, 0), pipeline_mode=pl.Buffered(1))


def _params(n_axes):
    return pltpu.CompilerParams(dimension_semantics=("arbitrary",) * n_axes,
                                vmem_limit_bytes=VMEM_LIMIT)


def _rms(x, g):
    return x * lax.rsqrt(jnp.mean(x * x, axis=-1, keepdims=True) + EPS) * g


def _dot(a, b):
    return jnp.dot(a, b, preferred_element_type=F32)


def _mod_kernel(c_ref, w_ref, b_ref, o_ref):
    c = c_ref[...]
    s = (c * jax.nn.sigmoid(c)).astype(BF16)
    o_ref[0] = _dot(s, w_ref[0].astype(BF16)) + b_ref[0]


def _modulation(cc, mod_w, mod_b):
    rows = cc.shape[0]
    tn = 1536
    n = N_MOD * D_MODEL
    return pl.pallas_call(
        _mod_kernel,
        grid=(DEPTH, n // tn),
        in_specs=[pl.BlockSpec((rows, D_MODEL), lambda l, j: (0, 0)),
                  pl.BlockSpec((1, D_MODEL, tn), lambda l, j: (l, 0, j)),
                  pl.BlockSpec((1, 1, tn), lambda l, j: (l, 0, j))],
        out_specs=pl.BlockSpec((1, rows, tn), lambda l, j: (l, 0, j)),
        out_shape=jax.ShapeDtypeStruct((DEPTH, rows, n), F32),
        compiler_params=_params(2),
        name="modulation",
    )(cc, mod_w, mod_b.reshape(DEPTH, 1, n))


def _premix_kernel(*refs, rope, cache_out):
    x_ref, mod_ref, g_ref, w_ref = refs[:4]
    refs = refs[4:]
    if rope:
        cos_ref, sa_ref, sb_ref = refs[:3]
        refs = refs[3:]
    q_ref, k_ref, v_ref, u_ref = refs[:4]
    if cache_out:
        kf_ref, vf_ref = refs[4:6]
    ubuf_ref = refs[-1]
    bb, tm = x_ref.shape[0], x_ref.shape[1]
    sub = PREMIX_SUB
    cps = sub // S5_T

    def store(ref, r0, c0, val):
        piece = min(sub, tm)
        for p in range(sub // piece):
            bl, t0 = divmod(r0 + p * piece, tm)
            ref[bl, t0:t0 + piece, c0:c0 + val.shape[1]] = val[p * piece:(p + 1) * piece]

    def store_head(ref, r0, head, val):
        piece = min(sub, tm)
        for p in range(sub // piece):
            bl, t0 = divmod(r0 + p * piece, tm)
            ref[bl, 0, t0:t0 + piece, head, :] = val[p * piece:(p + 1) * piece]

    def problem(k):
        r0 = k * sub
        piece = min(sub, tm)
        x = jnp.concatenate([x_ref[divmod(r0 + p * piece, tm)[0],
                                   pl.ds(divmod(r0 + p * piece, tm)[1], piece), :]
                             for p in range(sub // piece)], axis=0)
        h = _rms(x, g_ref[...]) * (1.0 + mod_ref[0, 1:2, :]) + mod_ref[0, 0:1, :]
        hb = h.astype(BF16)
        yield
        z = _dot(hb, w_ref[:, 0:2 * DA_WIDTH])
        yield
        for j in range(2 * DA_HEADS):
            s = z[:, j * LANES:(j + 1) * LANES]
            if cache_out and j >= DA_HEADS:
                store_head(kf_ref, r0, j - DA_HEADS, s)
            if rope:
                t = slice(r0, r0 + sub)
                s = (s * cos_ref[t, :] + pltpu.roll(s, LANES - 16, axis=1) * sa_ref[t, :]
                     + pltpu.roll(s, 16, axis=1) * sb_ref[t, :])
            if j < DA_HEADS:
                store(q_ref, r0, j * LANES, (s * Q_SCALE).astype(BF16))
            else:
                store(k_ref, r0, (j - DA_HEADS) * LANES, s.astype(BF16))
        yield
        z = _dot(hb, w_ref[:, 2 * DA_WIDTH:IN_WIDTH])
        yield
        v = z[:, 0:DA_WIDTH]
        store(v_ref, r0, 0, v.astype(BF16))
        if cache_out:
            for hd in range(DA_HEADS):
                store_head(vf_ref, r0, hd, v[:, hd * LANES:(hd + 1) * LANES])
        for o in range(S5_OCT):
            ubuf_ref[k, o] = z[:, DA_WIDTH + o * LANES:DA_WIDTH + (o + 1) * LANES]
        for o in range(S5_OCT):
            for s in range(S5_T):
                c0 = (o * S5_T + s) * LANES
                u_ref[k * cps:(k + 1) * cps, c0:c0 + LANES] = (
                    ubuf_ref[k, o, pl.ds(s, cps, stride=S5_T), :])

    _interleave(*[(problem(k), 1, k) for k in range(bb * tm // sub)])


def _chunk_spec(L, bb, tm):
    assert bb == 1 or tm == L
    return pl.BlockSpec((bb * tm // S5_T, S5_WIDTH * S5_T), lambda b, i: (b * (L // tm) + i, 0))


def _premix(x, mods, mod_row, norm_g, w_in, rope_tabs, tiling, cache_out):
    B, L, _ = x.shape
    bb, tm = tiling
    rope = rope_tabs is not None
    row_spec = lambda w: pl.BlockSpec((bb, tm, w), lambda b, i: (b, i, 0))
    in_specs = [row_spec(D_MODEL),
                pl.BlockSpec((1, N_MOD, D_MODEL), lambda b, i: (mod_row(b), 0, 0)),
                _const_spec((1, D_MODEL)),
                _const_spec((D_MODEL, IN_WIDTH))]
    args = [x, mods, norm_g.reshape(1, D_MODEL), w_in]
    if rope:
        in_specs += [pl.BlockSpec((tm, LANES), lambda b, i: (i, 0))] * 3
        args += list(rope_tabs)
    out_specs = [row_spec(DA_WIDTH)] * 3 + [_chunk_spec(L, bb, tm)]
    out_shape = [jax.ShapeDtypeStruct((B, L, DA_WIDTH), BF16)] * 3 + [
        jax.ShapeDtypeStruct((B * L // S5_T, S5_WIDTH * S5_T), F32)]
    if cache_out:
        out_specs += [pl.BlockSpec((bb, 1, tm, DA_HEADS, LANES), lambda b, i: (b, 0, i, 0, 0))] * 2
        out_shape += [jax.ShapeDtypeStruct((B, 1, L, DA_HEADS, LANES), F32)] * 2
    return pl.pallas_call(
        functools.partial(_premix_kernel, rope=rope, cache_out=cache_out),
        grid=(B // bb, L // tm),
        in_specs=in_specs, out_specs=out_specs, out_shape=out_shape,
        scratch_shapes=[pltpu.VMEM((bb * tm // PREMIX_SUB, S5_OCT, PREMIX_SUB, LANES), F32)],
        compiler_params=_params(2),
        name="premix_rope" if rope else "premix",
    )(*args)


def _rope_tables(L):
    rows = L // GRID_W
    row = np.repeat(np.arange(rows), GRID_W).astype(np.float64)
    col = np.tile(np.arange(GRID_W), rows).astype(np.float64)
    half = DA_QK // 2
    inv = 1.0 / (ROPE_BASE ** (np.arange(0, half, 2, dtype=np.float64) / half))
    ang_r, ang_c = row[:, None] * inv, col[:, None] * inv
    zero = np.zeros_like(ang_r)
    cos64 = np.concatenate([np.cos(ang_r)] * 2 + [np.cos(ang_c)] * 2, axis=1)
    sa64 = np.concatenate([-np.sin(ang_r), zero, -np.sin(ang_c), zero], axis=1)
    sb64 = np.concatenate([zero, np.sin(ang_r), zero, np.sin(ang_c)], axis=1)
    return tuple(jnp.asarray(np.tile(t, (1, LANES // DA_QK)), dtype=F32)
                 for t in (cos64, sa64, sb64))


def _attn_kernel(*refs, lam_init, has_cache):
    lqk_ref, g_ref, q_ref, k_ref, v_ref = refs[:5]
    if has_cache:
        ck_ref, cv_ref, o_ref = refs[5:8]
    else:
        o_ref = refs[5]
    lqk = lqk_ref[...]
    lam = (jnp.exp(jnp.sum(lqk[0:1] * lqk[1:2], axis=-1, keepdims=True))
           - jnp.exp(jnp.sum(lqk[2:3] * lqk[3:4], axis=-1, keepdims=True)) + lam_init)
    tq = q_ref.shape[1]
    lane = lax.broadcasted_iota(jnp.int32, (ATTN_SUB, LANES), 1)
    zero = jnp.zeros((ATTN_SUB, LANES), BF16)
    nt = (((1,), (1,)), ((), ()))
    mxu_sum = has_cache
    ones = lambda v: jnp.concatenate([v, jnp.ones_like(v)], axis=1) if mxu_sum else v

    def problem(cols, rows, segs):
        q = q_ref[0, rows, cols]
        outs = []
        for qm in (jnp.where(lane < DA_QK, q, zero), jnp.where(lane >= DA_QK, q, zero)):
            ss = [lax.dot_general(qm, k, nt, preferred_element_type=F32) for k, _ in segs]
            yield
            m = functools.reduce(jnp.maximum, [jnp.max(s, axis=-1, keepdims=True) for s in ss])
            es = [jnp.exp2(s - m) for s in ss]
            if not mxu_sum:
                l = functools.reduce(jnp.add, [jnp.sum(e, axis=-1, keepdims=True) for e in es])
            es = [e.astype(BF16) for e in es]
            yield
            o = functools.reduce(jnp.add, [_dot(e, v) for e, (_, v) in zip(es, segs)])
            if mxu_sum:
                l = o[:, DA_V:DA_V + 1]
            outs.append(o[:, :DA_V] / l)
            yield
        o = outs[0] - lam * outs[1]
        o_ref[0, rows, cols] = (_rms(o, g_ref[...]) * (1.0 - lam_init)).astype(BF16)

    problems = []
    for hh in range(q_ref.shape[2] // LANES):
        cols = slice(hh * LANES, (hh + 1) * LANES)
        segs = [(k_ref[0, :, cols], ones(v_ref[0, :, cols]))]
        if has_cache:
            segs.append((ck_ref[0, :, cols].astype(BF16), ones(cv_ref[0, :, cols].astype(BF16))))
        for r in range(tq // ATTN_SUB):
            problems.append(problem(cols, slice(r * ATTN_SUB, (r + 1) * ATTN_SUB), segs))
    _interleave(*[(p, 1, i) for i, p in enumerate(problems)])


def _attention(q, k, v, cache, lambda_qk, subln_g, lam_init, tq, heads):
    B, L, _ = q.shape
    has_cache = cache is not None
    q_spec = pl.BlockSpec((1, tq, heads * LANES), lambda b, h, i: (b, i, h))
    kv_spec = lambda n: pl.BlockSpec((1, n, heads * LANES), lambda b, h, i: (b, 0, h))
    in_specs = [_const_spec((4, DA_QK)), _const_spec((1, DA_V)), q_spec, kv_spec(L), kv_spec(L)]
    args = [lambda_qk, subln_g.reshape(1, DA_V), q, k, v]
    if has_cache:
        P = cache[0].shape[1]
        in_specs += [kv_spec(P), kv_spec(P)]
        args += list(cache)
    return pl.pallas_call(
        functools.partial(_attn_kernel, lam_init=lam_init, has_cache=has_cache),
        grid=(B, DA_HEADS // heads, L // tq),
        in_specs=in_specs, out_specs=q_spec,
        out_shape=jax.ShapeDtypeStruct((B, L, DA_WIDTH), BF16),
        compiler_params=_params(3),
        name="diff_attn_cache" if has_cache else "diff_attn",
    )(*args)


def _s5_weights(lam_re, lam_im, log_dt, b_re, b_im, c_re, c_im):
    T, G, P, H = S5_T, S5_GROUPS, S5_STATE, S5_CH
    dt = jnp.exp(log_dt)[..., None]
    lr, li = lam_re * dt, lam_im * dt
    kk = jnp.arange(T + 1, dtype=F32)[None, :, None, None]
    mag = jnp.exp(kk * lr[:, None])
    pw_re = mag * jnp.cos(kk * li[:, None])
    pw_im = mag * jnp.sin(kk * li[:, None])
    a_re, a_im = pw_re[:, 1], pw_im[:, 1]
    den = lam_re * lam_re + lam_im * lam_im
    f_re = ((a_re - 1.0) * lam_re + a_im * lam_im) / den
    f_im = (a_im * lam_re - (a_re - 1.0) * lam_im) / den
    bt_re, bt_im = jnp.swapaxes(b_re, -1, -2), jnp.swapaxes(b_im, -1, -2)
    bb_re = f_re[:, :, None, :] * bt_re - f_im[:, :, None, :] * bt_im
    bb_im = f_re[:, :, None, :] * bt_im + f_im[:, :, None, :] * bt_re

    def per_dir(x, fwd, bwd):
        return jnp.stack([fwd(x[0]), bwd(x[1])], axis=0)

    flip = lambda x: x[::-1]
    pr = per_dir(pw_re, lambda x: flip(x[:T]), lambda x: x[:T])[:, :, :, None, :]
    pi = per_dir(pw_im, lambda x: flip(x[:T]), lambda x: x[:T])[:, :, :, None, :]
    br, bi = bb_re[:, None], bb_im[:, None]
    fc = jnp.stack([pr * br - pi * bi, pr * bi + pi * br], axis=1)
    fc = fc.reshape(2, 2, T, S5_OCT, LANES, P)
    fc = jnp.concatenate([fc, fc], axis=-1)

    def lanes_gh(x):
        x = x.reshape(x.shape[:-2] + (S5_OCT, S5_GPO, P))
        return jnp.repeat(jnp.swapaxes(x, -1, -2), H, axis=-1)

    def c_lanes(c):
        c = c.reshape(2, S5_OCT, S5_GPO, H, P).transpose(0, 1, 4, 2, 3)
        return c.reshape(2, 1, S5_OCT, P, LANES)

    fwd_pows = lambda x: jnp.concatenate([x[1:], x[:1]])
    bwd_pows = lambda x: jnp.concatenate([flip(x[1:]), x[:1]])
    per = lanes_gh(per_dir(pw_re, fwd_pows, bwd_pows))
    pei = lanes_gh(per_dir(pw_im, fwd_pows, bwd_pows))
    cr, ci = c_lanes(c_re), c_lanes(c_im)
    ec = jnp.stack([cr * per - ci * pei, -(cr * pei + ci * per)], axis=1)
    ec = jnp.concatenate([ec, ec], axis=-2)

    at = jnp.stack([pw_re[:, T], pw_im[:, T]], axis=1)
    at = at.reshape(2, 2, S5_OCT, S5_GPO * P).transpose(0, 2, 1, 3)
    return fc.astype(BF16), ec.astype(BF16), at


def _s5_kernel(*refs, n_seq, n_chunk):
    T = S5_T
    (x_ref, fc_ref, ec_ref, at_ref, h0_ref, d_ref, out_ref, fin_ref,
     sf_ref, sb_ref, wt_ref, wf_ref, we_ref, c0_ref) = refs
    n_tile = S5_SW // LANES
    nh = n_tile // 2

    @pl.when(pl.program_id(1) == 0)
    def _():
        _s5_expand(fc_ref, ec_ref, wt_ref, wf_ref, we_ref, c0_ref)

    M = x_ref.shape[0]
    blocks = [slice(r, r + S5_ROWS) for r in range(0, M, S5_ROWS)]
    seq_pad = sf_ref.shape[1] // n_chunk

    def seq_rows(rows):
        return [(r // n_chunk, slice(r - rows.start, r - rows.start + n_chunk))
                for r in range(rows.start, rows.stop, n_chunk)]

    def to_tiles(ref, rows, val):
        for b, local in seq_rows(rows):
            for j in range(n_tile):
                ref[j, pl.ds(b, n_chunk, stride=seq_pad), :] = val[local, j * LANES:(j + 1) * LANES]

    def from_tiles(ref, rows):
        return jnp.concatenate(
            [jnp.concatenate([ref[j, pl.ds(b, n_chunk, stride=seq_pad), :].astype(BF16)
                              for j in range(n_tile)], axis=1)
             for b, _ in seq_rows(rows)], axis=0)

    for rows in blocks:
        xb = x_ref[rows, :].astype(BF16)
        to_tiles(sf_ref, rows, _dot(xb, wf_ref[0]))
        to_tiles(sb_ref, rows, _dot(xb, wf_ref[1]))

    def coeff(d):
        return [(at_ref[d, 0, 0:1, j * LANES:(j + 1) * LANES],
                 at_ref[d, 0, 1:2, j * LANES:(j + 1) * LANES]) for j in range(nh)]

    def advance(ref, rows, a, state):
        out = []
        for j in range(nh):
            (ar, ai), (hr, hi) = a[j], state[j]
            sr, si = ref[j, rows, :], ref[nh + j, rows, :]
            ref[j, rows, :] = hr
            ref[nh + j, rows, :] = hi
            out.append((ar * hr - ai * hi + sr, ar * hi + ai * hr + si))
        return out

    af, ab = coeff(0), coeff(1)

    def step(i, carry):
        hf, hb = carry
        fwd = pl.multiple_of(i * seq_pad, seq_pad)
        bwd = pl.multiple_of((n_chunk - 1 - i) * seq_pad, seq_pad)
        hf = advance(sf_ref, pl.ds(fwd, n_seq), af, hf)
        hb = advance(sb_ref, pl.ds(bwd, n_seq), ab, hb)
        return hf, hb

    def initial(d):
        return [(h0_ref[d, 0, 0, :, j * LANES:(j + 1) * LANES],
                 h0_ref[d, 0, 0, :, (nh + j) * LANES:(nh + j + 1) * LANES]) for j in range(nh)]

    finals = lax.fori_loop(0, n_chunk, step, (initial(0), initial(1)))
    for d in range(2):
        for j in range(nh):
            fin_ref[d, 0, 0, :, j * LANES:(j + 1) * LANES] = finals[d][j][0]
            fin_ref[d, 0, 0, :, (nh + j) * LANES:(nh + j + 1) * LANES] = finals[d][j][1]

    d = jnp.concatenate([d_ref[0]] * T, axis=1)
    for rows in blocks:
        x = x_ref[rows, :]
        y = (_dot(x.astype(BF16), wt_ref[...]) + _dot(from_tiles(sf_ref, rows), we_ref[0])
             + _dot(from_tiles(sb_ref, rows), we_ref[1]))
        out_ref[rows, :] = jax.nn.gelu(y + d * x)


def _s5_expand(fc_ref, ec_ref, wt_ref, wf_ref, we_ref, c0_ref):
    T = S5_T
    nh = S5_SW // LANES // 2
    row = lax.broadcasted_iota(jnp.int32, (LANES, LANES), 0)
    lane = lax.broadcasted_iota(jnp.int32, (LANES, LANES), 1)
    gpt = LANES // S5_STATE
    zero = jnp.zeros((LANES, LANES), BF16)
    own_f = [(row // S5_CH) == (gpt * j + lane // S5_STATE) for j in range(nh)]
    own_e = [(lane // S5_CH) == (gpt * j + row // S5_STATE) for j in range(nh)]
    for d in range(2):
        for ri in range(2):
            for s in range(T):
                f, e = fc_ref[d, ri, s, 0], ec_ref[d, ri, s, 0]
                rows = slice(s * LANES, (s + 1) * LANES)
                for j in range(nh):
                    cols = slice((ri * nh + j) * LANES, (ri * nh + j + 1) * LANES)
                    wf_ref[d, rows, cols] = jnp.where(own_f[j], f, zero)
                    we_ref[d, cols, rows] = jnp.where(own_e[j], e, zero)
            e0 = ec_ref[d, ri, T, 0]
            for j in range(nh):
                cols = slice((ri * nh + j) * LANES, (ri * nh + j + 1) * LANES)
                c0_ref[d, cols, :] = jnp.where(own_e[j], e0, zero)

    bbar = (wf_ref[0, (T - 1) * LANES:T * LANES, :], wf_ref[1, 0:LANES, :])

    def lag_kernel(d, tau):
        if tau == 0:
            return _dot(bbar[d], c0_ref[d])
        t = tau - 1 if d == 0 else T - tau
        return _dot(bbar[d], we_ref[d, :, t * LANES:(t + 1) * LANES])

    lag_tiles = ([lag_kernel(1, T - 1 - l) for l in range(T - 1)]
                 + [lag_kernel(0, 0) + lag_kernel(1, 0)]
                 + [lag_kernel(0, l) for l in range(1, T)])
    for s in range(T):
        for t in range(T):
            wt_ref[s * LANES:(s + 1) * LANES, t * LANES:(t + 1) * LANES] = (
                lag_tiles[t - s + T - 1].astype(BF16))


def _s5_mixer(u, weights, h0, d_skip, n_seq, n_split):
    T = S5_T
    M_all = u.shape[0]
    M = M_all // n_split
    B = n_seq // n_split
    n_chunk = M // B
    seq_pad = -(-B // SUBLANES) * SUBLANES
    fc, ec, at = weights
    TL = T * LANES
    x_spec = pl.BlockSpec((M, TL), lambda o, j: (j, o))
    tile_spec = lambda n: pl.BlockSpec((2, 2, n, 1, LANES, LANES), lambda o, j: (0, 0, 0, o, 0, 0))
    st_spec = pl.BlockSpec((2, 1, 1, B, S5_SW), lambda o, j: (0, o, j, 0, 0))
    g, fin = pl.pallas_call(
        functools.partial(_s5_kernel, n_seq=B, n_chunk=n_chunk),
        grid=(S5_OCT, n_split),
        in_specs=[x_spec,
                  tile_spec(T), tile_spec(T + 1),
                  pl.BlockSpec((2, 1, 2, S5_SW // 2), lambda o, j: (0, o, 0, 0)),
                  st_spec,
                  pl.BlockSpec((1, 1, LANES), lambda o, j: (o, 0, 0))],
        out_specs=[x_spec, st_spec],
        out_shape=[jax.ShapeDtypeStruct((M_all, S5_OCT * TL), F32),
                   jax.ShapeDtypeStruct((2, S5_OCT, n_split, B, S5_SW), F32)],
        scratch_shapes=[pltpu.VMEM((S5_SW // LANES, n_chunk * seq_pad, LANES), F32)] * 2 + [
            pltpu.VMEM((TL, TL), BF16), pltpu.VMEM((2, TL, S5_SW), BF16),
            pltpu.VMEM((2, S5_SW, TL), BF16), pltpu.VMEM((2, S5_SW, LANES), BF16)],
        compiler_params=_params(2),
        name="s5_mixer",
    )(u, fc, ec, at, h0, d_skip.reshape(S5_OCT, 1, LANES))
    return g, fin


def _mlp(h, w1_ref, w2_ref):
    hb = h.astype(BF16)
    acc = None
    for j in range(FF_HIDDEN // FF_CHUNK):
        a = _dot(hb, w1_ref[:, j * FF_CHUNK:(j + 1) * FF_CHUNK])
        a = jnp.square(jnp.maximum(a, 0.0)).astype(BF16)
        p = _dot(a, w2_ref[j * FF_CHUNK:(j + 1) * FF_CHUNK, :])
        acc = p if acc is None else acc + p
    return acc


def _interleave(*staged):
    live = list(staged)
    rnd = 0
    while live:
        for item in list(live):
            gen, per_round, first = item
            if rnd < first:
                continue
            for _ in range(per_round):
                if next(gen, _DONE) is _DONE:
                    live.remove(item)
                    break
        rnd += 1


_DONE = object()


def _post0_kernel(x_ref, attn_ref, g_ref, mod_ref, ng_ref, wglu_ref, wout_ref, w1_ref, w2_ref, o_ref,
                  gbuf_ref, *, sub):
    bb, tm = x_ref.shape[0], x_ref.shape[1]
    n_sub = tm // sub
    cps = sub // S5_T
    for k in range(bb * n_sub):
        bl, rows = k // n_sub, slice((k % n_sub) * sub, (k % n_sub + 1) * sub)
        for o in range(S5_OCT):
            for s in range(S5_T):
                c0 = (o * S5_T + s) * LANES
                gbuf_ref[k, o, pl.ds(s, cps, stride=S5_T), :] = (
                    g_ref[k * cps:(k + 1) * cps, c0:c0 + LANES])
        g = jnp.concatenate([gbuf_ref[k, o] for o in range(S5_OCT)], axis=1)
        s5 = g * jax.nn.sigmoid(_dot(g.astype(BF16), wglu_ref[...]))
        oc = (_dot(attn_ref[bl, rows], wout_ref[0:DA_WIDTH, :])
              + _dot(s5.astype(BF16), wout_ref[DA_WIDTH:D_MODEL, :]))
        x1 = x_ref[bl, rows] + mod_ref[0, 2:3, :] * oc
        h = _rms(x1, ng_ref[...]) * (1.0 + mod_ref[0, 4:5, :]) + mod_ref[0, 3:4, :]
        o_ref[bl, rows] = x1 + mod_ref[0, 5:6, :] * _mlp(h, w1_ref, w2_ref)


def _post0(x, attn, g, mods, mod_row, norm_g, w_glu, w_out, ff, layer, tiling):
    B, L, _ = x.shape
    bb, tm, sub = tiling
    w1, w2 = ff
    row_spec = lambda w: pl.BlockSpec((bb, tm, w), lambda b, i: (b, i, 0))
    return pl.pallas_call(
        functools.partial(_post0_kernel, sub=sub),
        grid=(B // bb, L // tm),
        in_specs=[row_spec(D_MODEL), row_spec(DA_WIDTH), _chunk_spec(L, bb, tm),
                  pl.BlockSpec((1, N_MOD, D_MODEL), lambda b, i: (mod_row(b), 0, 0)),
                  _const_spec((1, D_MODEL)),
                  _const_spec((S5_WIDTH, S5_WIDTH)),
                  _const_spec((D_MODEL, D_MODEL)),
                  _layer_spec((D_MODEL, FF_HIDDEN), layer),
                  _layer_spec((FF_HIDDEN, D_MODEL), layer)],
        out_specs=row_spec(D_MODEL),
        out_shape=jax.ShapeDtypeStruct((B, L, D_MODEL), F32),
        scratch_shapes=[pltpu.VMEM((bb * tm // sub, S5_OCT, sub, LANES), F32)],
        compiler_params=_params(2),
        name="mix_out_mlp",
    )(x, attn, g, mods, norm_g.reshape(1, D_MODEL), w_glu, w_out, w1, w2)


def _layer1_kernel(x_ref, xp_ref, xn_ref, mod_ref, modp_ref, ng1_ref, ng2_ref, pw_ref, ps_ref,
                   w1_ref, w2_ref, fg_ref, o_ref, x1a_ref, ha_ref, x1b_ref, hb_ref,
                   *, seq_len, n_steps):
    s = pl.program_id(0)
    bb, tm = x_ref.shape[0], x_ref.shape[1]
    H = POOL_HALO
    n = tm + 2 * H
    tile = jnp.minimum(s, n_steps - 1)
    first = (tile % (seq_len // tm)) * tm

    def pool(x1_ref, h_ref):
        for bl in range(bb):
            x = x_ref[bl]
            xe = jnp.concatenate([xp_ref[bl], x, xn_ref[bl]], axis=0)
            he = _rms(xe, ng1_ref[...]) * (1.0 + mod_ref[0, 1:2, :]) + mod_ref[0, 0:1, :]
            pos_e = first - H + lax.broadcasted_iota(jnp.int32, (n, 1), 0)
            he = jnp.where((pos_e >= 0) & (pos_e < seq_len), he, 0.0)
            pos = first + lax.broadcasted_iota(jnp.int32, (tm, 1), 0)
            yield
            zs = []
            for gi, w in enumerate(POOL_WINDOWS):
                hg = he[:, gi * POOL_CH:(gi + 1) * POOL_CH]
                half = w // 2
                fwd, k = hg, 1
                while 2 * k < w:
                    fwd = fwd + pltpu.roll(fwd, n - k, axis=0)
                    k *= 2
                if half % SUBLANES == 0:
                    total = fwd[H - half:H - half + tm] + fwd[H:H + tm]
                else:
                    total = (fwd + pltpu.roll(fwd, half, axis=0))[H:H + tm]
                count = jnp.clip(pos + half, 0, seq_len) - jnp.clip(pos - half, 0, seq_len)
                z = total * (1.0 / count.astype(F32)) - hg[H:H + tm]
                zs.append(_dot(z.astype(BF16), pw_ref[gi]))
                yield
            oc = jnp.concatenate(zs, axis=1) * ps_ref[...]
            x1 = x + mod_ref[0, 2:3, :] * oc
            h = _rms(x1, ng2_ref[...]) * (1.0 + mod_ref[0, 4:5, :]) + mod_ref[0, 3:4, :]
            x1_ref[bl * tm:(bl + 1) * tm, :] = x1
            h_ref[bl * tm:(bl + 1) * tm, :] = h.astype(BF16)
            yield

    def mlp(x1_ref, h_ref):
        hb = h_ref[...]
        acc = None
        for j in range(FF_HIDDEN // FF_CHUNK):
            a = _dot(hb, w1_ref[:, j * FF_CHUNK:(j + 1) * FF_CHUNK])
            a = jnp.square(jnp.maximum(a, 0.0)).astype(BF16)
            yield
            p = _dot(a, w2_ref[j * FF_CHUNK:(j + 1) * FF_CHUNK, :])
            acc = p if acc is None else acc + p
            yield
        y = _rms(x1_ref[...] + modp_ref[0, 5:6, :] * acc, fg_ref[...])
        for bl in range(bb):
            o_ref[bl] = y[bl * tm:(bl + 1) * tm]

    pool_per_mlp = bb

    @pl.when(s == 0)
    def _():
        _interleave((pool(x1a_ref, ha_ref), 1, 0))

    @pl.when((s > 0) & (s % 2 == 1))
    def _():
        _interleave((mlp(x1a_ref, ha_ref), 1, 0), (pool(x1b_ref, hb_ref), pool_per_mlp, 0))

    @pl.when((s > 0) & (s % 2 == 0))
    def _():
        _interleave((mlp(x1b_ref, hb_ref), 1, 0), (pool(x1a_ref, ha_ref), pool_per_mlp, 0))


def _layer1(x, mods, mod_row, norm_g, pool_w, pool_scale, ff, layer, final_g, tiling):
    B, L, _ = x.shape
    bb, tm = tiling
    w1, w2 = ff
    H = POOL_HALO
    per = tm // H
    last = L // H - 1
    n_i = L // tm
    n_steps = (B // bb) * n_i

    def at(shift, fn):
        def index(s):
            t = jnp.clip(s - shift, 0, n_steps - 1)
            return fn(t // n_i, t % n_i)
        return index

    row_map = lambda b, i: (b, i, 0)
    mod_map = lambda b, i: (mod_row(b), 0, 0)
    rows = bb * tm
    return pl.pallas_call(
        functools.partial(_layer1_kernel, seq_len=L, n_steps=n_steps),
        grid=(n_steps + 1,),
        in_specs=[pl.BlockSpec((bb, tm, D_MODEL), at(0, row_map)),
                  pl.BlockSpec((bb, H, D_MODEL),
                               at(0, lambda b, i: (b, jnp.maximum(i * per - 1, 0), 0))),
                  pl.BlockSpec((bb, H, D_MODEL),
                               at(0, lambda b, i: (b, jnp.minimum((i + 1) * per, last), 0))),
                  pl.BlockSpec((1, N_MOD, D_MODEL), at(0, mod_map)),
                  pl.BlockSpec((1, N_MOD, D_MODEL), at(1, mod_map)),
                  _const_spec((1, D_MODEL)), _const_spec((1, D_MODEL)),
                  _const_spec((len(POOL_WINDOWS), POOL_CH, POOL_CH)),
                  _const_spec((1, D_MODEL)),
                  _layer_spec((D_MODEL, FF_HIDDEN), layer),
                  _layer_spec((FF_HIDDEN, D_MODEL), layer),
                  _const_spec((1, D_MODEL))],
        out_specs=pl.BlockSpec((bb, tm, D_MODEL), at(1, row_map)),
        out_shape=jax.ShapeDtypeStruct((B, L, D_MODEL), F32),
        scratch_shapes=[pltpu.VMEM((rows, D_MODEL), F32), pltpu.VMEM((rows, D_MODEL), BF16)] * 2,
        compiler_params=_params(1),
        name="pool_mlp_norm",
    )(x, x, x, mods, mods, norm_g[0].reshape(1, D_MODEL), norm_g[1].reshape(1, D_MODEL),
      pool_w, pool_scale.reshape(1, D_MODEL), w1, w2, final_g.reshape(1, D_MODEL))


def kernel(x_prompt, x_sample, cache_k, cache_v, state_s5, c, c_ctx, mod_w, mod_b, norm_g, mix_w_in, mix_w_out, diff_lambda_qk, diff_subln_g, s5_lambda_re, s5_lambda_im, s5_log_dt, s5_b_re, s5_b_im, s5_c_re, s5_c_im, s5_d, s5_w_glu, pool_w, pool_scale, ff_w1, ff_w2, final_norm_g):
    B_ctx, L_ctx, _ = x_prompt.shape
    B_lat, L_lat, _ = x_sample.shape
    ctx_row = B_lat
    n_rows = 8
    cc = jnp.concatenate([c, c_ctx[None], jnp.zeros((n_rows - B_lat - 1, D_MODEL), F32)], axis=0)
    mods = _modulation(cc, mod_w, mod_b).reshape(DEPTH, n_rows, N_MOD, D_MODEL)
    lat_row = lambda b: b
    ctx_mod_row = lambda b: ctx_row

    j = 0
    lam_init = 0.8 - 0.6 * math.exp(-0.3 * 0)
    w_in = mix_w_in[j].astype(BF16)
    w_out = mix_w_out[j].astype(BF16)
    w_glu = s5_w_glu[j].astype(BF16)
    ff = (ff_w1.astype(BF16), ff_w2.astype(BF16))
    s5w = _s5_weights(s5_lambda_re[j], s5_lambda_im[j], s5_log_dt[j], s5_b_re[j], s5_b_im[j],
                      s5_c_re[j], s5_c_im[j])

    qc, kc, vc, uc, kc32, vc32 = _premix(x_prompt, mods[0], ctx_mod_row, norm_g[0, 0], w_in,
                                         None, (4, 256), True)
    ql, kl, vl, ul = _premix(x_sample, mods[0], lat_row, norm_g[0, 0], w_in,
                             _rope_tables(L_lat), (1, 1024), False)

    ac = _attention(qc, kc, vc, None, diff_lambda_qk[j], diff_subln_g[j], lam_init, 256, DA_HEADS)
    P = cache_k.shape[2]
    cache = (cache_k[:, j].reshape(B_lat, P, DA_WIDTH), cache_v[:, j].reshape(B_lat, P, DA_WIDTH))
    al = _attention(ql, kl, vl, cache, diff_lambda_qk[j], diff_subln_g[j], lam_init, 1024, 1)

    lat_split = 1
    h0_ctx = jnp.zeros((2, S5_OCT, 1, B_ctx, S5_SW), F32)
    h0_lat = state_s5[:, j].reshape(B_lat, 2, S5_OCT, S5_GPO, S5_STATE, 2)
    h0_lat = h0_lat.transpose(1, 2, 0, 5, 3, 4).reshape(
        2, S5_OCT, lat_split, B_lat // lat_split, S5_SW)
    gc, fin_c = _s5_mixer(uc, s5w, h0_ctx, s5_d[j], B_ctx, 1)
    gl, _ = _s5_mixer(ul, s5w, h0_lat, s5_d[j], B_lat, lat_split)

    ctx = _post0(x_prompt, ac, gc, mods[0], ctx_mod_row, norm_g[0, 1], w_glu, w_out, ff, 0,
                 CTX_TILING)
    lat = _post0(x_sample, al, gl, mods[0], lat_row, norm_g[0, 1], w_glu, w_out, ff, 0,
                 LAT_TILING)

    pw = pool_w[0].astype(BF16)
    y_prompt = _layer1(ctx, mods[1], ctx_mod_row, norm_g[1], pw, pool_scale[0], ff, 1,
                       final_norm_g, CTX_POOL_TILING)
    y_sample = _layer1(lat, mods[1], lat_row, norm_g[1], pw, pool_scale[0], ff, 1,
                       final_norm_g, LAT_POOL_TILING)

    new_cache_k, new_cache_v = kc32, vc32
    st = fin_c.reshape(2, S5_OCT, B_ctx, 2, S5_GPO, S5_STATE)
    new_state = st.transpose(2, 0, 1, 4, 5, 3).reshape(B_ctx, 1, 2, S5_GROUPS, S5_STATE, 2)
    return (y_prompt, y_sample, new_cache_k, new_cache_v, new_state)
```
